```python
import numpy as np
import jax
import jax.numpy as jnp
from jax import lax

D_MODEL = 1024
BATCH = 2
SEQ = 8192
DEPTH = 2

HEAD_DIM = 64
MIX_WIDTH = D_MODEL
N_HEADS_DIL = (MIX_WIDTH // 2) // HEAD_DIM
DIL_CONFIGS = ((128, 1), (512, 4), (2048, 16))
N_HEADS_MLA = (MIX_WIDTH // 2) // HEAD_DIM
MLA_NOPE = HEAD_DIM
MLA_ROPE = HEAD_DIM // 2
MLA_V = HEAD_DIM
Q_LORA = D_MODEL // 4
KV_LORA = D_MODEL // 8
D_FF = 4 * D_MODEL
ROPE_THETA = 10000.0
EPS = 1e-6
Q_BLOCK = 128
NEG_INF = -1e30
IN_SPLITS = (N_HEADS_DIL * HEAD_DIM, N_HEADS_DIL * HEAD_DIM, N_HEADS_DIL * HEAD_DIM, Q_LORA, KV_LORA, MLA_ROPE)
W_IN_COLS = sum(IN_SPLITS)

kernel_name = 'hybrid_dilated_mla_adaln_encoder'


def rms_norm(x, g):
    xf = x.astype(jnp.float32)
    y = xf * lax.rsqrt(jnp.mean(xf * xf, axis=-1, keepdims=True) + EPS)
    return (y * g.astype(jnp.float32)).astype(x.dtype)


def alibi_slopes(n):
    return jnp.asarray([2.0 ** (-8.0 * (h + 1) / n) for h in range(n)], dtype=jnp.float32)


def rope(x, pos):
    half = x.shape[-1] // 2
    inv = ROPE_THETA ** (-jnp.arange(half, dtype=jnp.float32) / half)
    ang = pos.astype(jnp.float32)[..., None] * inv
    ang = ang.reshape(ang.shape[:2] + (1,) * (x.ndim - 3) + (half,))
    cos, sin = jnp.cos(ang), jnp.sin(ang)
    xf = x.astype(jnp.float32)
    x1, x2 = xf[..., :half], xf[..., half:]
    return jnp.concatenate([x1 * cos - x2 * sin, x2 * cos + x1 * sin], axis=-1).astype(x.dtype)


def dilated_branch(q, k, v, pos, slopes, window, dil):
    B, S, H, Dh = q.shape
    half = window // (2 * dil)
    blk = half
    span = dil * blk
    Sp = -(-S // span) * span
    L = Sp // dil
    nb = L // blk
    pad = Sp - S

    def to_strided(a):
        a = jnp.pad(a, ((0, 0), (0, pad), (0, 0), (0, 0)))
        return a.reshape(B, L, dil, H, Dh).transpose(0, 2, 3, 1, 4)

    def band(a, axis, fill):
        pw = [(0, 0)] * a.ndim
        pw[axis] = (blk, blk)
        a = jnp.pad(a, pw, constant_values=fill)
        a = a.reshape(a.shape[:axis] + (nb + 2, blk) + a.shape[axis + 1:])
        parts = [lax.slice_in_dim(a, s, s + nb, axis=axis) for s in range(3)]
        return jnp.concatenate(parts, axis=axis + 1)

    qs, ks, vs = to_strided(q), to_strided(k), to_strided(v)
    pos_s = jnp.pad(pos, ((0, 0), (0, pad))).reshape(B, L, dil).transpose(0, 2, 1)
    val_s = (jnp.arange(Sp) < S).reshape(L, dil).T

    qb = qs.reshape(B, dil, H, nb, blk, Dh)
    kw = band(ks, 3, 0)
    vw = band(vs, 3, 0)
    pk = band(pos_s, 2, 0)
    vk = band(val_s, 1, False)
    pq = pos_s.reshape(B, dil, nb, blk)

    s = jnp.einsum('brhnid,brhnjd->brhnij', qb, kw, preferred_element_type=jnp.float32) * (Dh ** -0.5)
    dist = jnp.abs(pq[..., :, None] - pk[..., None, :]).astype(jnp.float32)
    s = s - slopes[None, None, :, None, None, None] * dist[:, :, None]
    rel = jnp.arange(3 * blk)[None, :] - blk - jnp.arange(blk)[:, None]
    mask = (jnp.abs(rel) <= half)[None] & vk[:, :, None, :]
    s = jnp.where(mask[None, :, None], s, NEG_INF)
    m = jnp.max(s, axis=-1, keepdims=True)
    p = jnp.exp(s - m)
    l = jnp.sum(p, axis=-1, keepdims=True)
    o = jnp.einsum('brhnij,brhnjd->brhnid', p, vw.astype(jnp.float32)) / l
    lse = (m + jnp.log(l))[..., 0]
    o = o.reshape(B, dil, H, L, Dh).transpose(0, 3, 1, 2, 4).reshape(B, Sp, H, Dh)[:, :S]
    lse = lse.reshape(B, dil, H, L).transpose(0, 3, 1, 2).reshape(B, Sp, H)[:, :S]
    return o, lse


def dilated_attention(q, k, v, pos, slopes):
    outs, lses = [], []
    for window, dil in DIL_CONFIGS:
        o, lse = dilated_branch(q, k, v, pos, slopes, window, dil)
        outs.append(o)
        lses.append(lse)
    wts = jax.nn.softmax(jnp.stack(lses), axis=0)
    return jnp.sum(jnp.stack(outs) * wts[..., None], axis=0)


def latent_attention(cq, ckv, k_rope, pos, g_cq, w_q_up, g_ckv, w_kv_up, g_qn, g_qr, g_kn, g_kr):
    B, S, _ = cq.shape
    H = N_HEADS_MLA
    q = (rms_norm(cq, g_cq) @ w_q_up).reshape(B, S, H, MLA_NOPE + MLA_ROPE)
    q_nope = rms_norm(q[..., :MLA_NOPE], g_qn)
    q_rot = rope(rms_norm(q[..., MLA_NOPE:], g_qr), pos)
    kv = (rms_norm(ckv, g_ckv) @ w_kv_up).reshape(B, S, H, MLA_NOPE + MLA_V)
    k_nope = rms_norm(kv[..., :MLA_NOPE], g_kn)
    v = kv[..., MLA_NOPE:]
    k_rot = rope(rms_norm(k_rope, g_kr), pos)
    scale = (MLA_NOPE + MLA_ROPE) ** -0.5
    nq = S // Q_BLOCK
    qn_b = q_nope.reshape(B, nq, Q_BLOCK, H, MLA_NOPE).transpose(1, 0, 2, 3, 4)
    qr_b = q_rot.reshape(B, nq, Q_BLOCK, H, MLA_ROPE).transpose(1, 0, 2, 3, 4)

    def block(args):
        qn, qr = args
        s = (jnp.einsum('bqhd,bkhd->bhqk', qn, k_nope, preferred_element_type=jnp.float32)
             + jnp.einsum('bqhr,bkr->bhqk', qr, k_rot, preferred_element_type=jnp.float32)) * scale
        p = jax.nn.softmax(s, axis=-1)
        return jnp.einsum('bhqk,bkhd->bqhd', p.astype(v.dtype), v)

    o = lax.map(block, (qn_b, qr_b))
    return o.transpose(1, 0, 2, 3, 4).reshape(B, S, H * MLA_V)


def setup_inputs(seed: int = 0) -> dict:
    key = jax.random.key(seed)
    k = jax.random.split(key, 21)
    Ld = DEPTH
    f32 = jnp.float32

    def nrm(i, shape, fan_in, mult=1.0):
        return jax.random.normal(k[i], shape, f32) * (mult * fan_in ** -0.5)

    def gain(i, shape):
        return 1.0 + 0.02 * jax.random.normal(k[i], shape, f32)

    x = jax.random.normal(k[0], (BATCH, SEQ, D_MODEL), f32)
    c = jax.random.normal(k[1], (BATCH, D_MODEL), f32)
    offs = jax.random.randint(k[2], (BATCH, 1), 0, 1024, dtype=jnp.int32)
    positions = jnp.arange(SEQ, dtype=jnp.int32)[None, :] + offs
    return {
        'x': x,
        'c': c,
        'positions': positions,
        'w_mod': nrm(3, (Ld, D_MODEL, 6 * D_MODEL), D_MODEL, 0.5),
        'b_mod': 0.01 * jax.random.normal(k[4], (Ld, 6 * D_MODEL), f32),
        'g_norm_mix': gain(5, (Ld, D_MODEL)),
        'w_in': nrm(6, (Ld, D_MODEL, W_IN_COLS), D_MODEL),
        'g_q_dil': gain(7, (Ld, HEAD_DIM)),
        'g_k_dil': gain(8, (Ld, HEAD_DIM)),
        'g_cq': gain(9, (Ld, Q_LORA)),
        'w_q_up': nrm(10, (Ld, Q_LORA, N_HEADS_MLA * (MLA_NOPE + MLA_ROPE)), Q_LORA),
        'g_ckv': gain(11, (Ld, KV_LORA)),
        'w_kv_up': nrm(12, (Ld, KV_LORA, N_HEADS_MLA * (MLA_NOPE + MLA_V)), KV_LORA),
        'g_q_nope': gain(13, (Ld, MLA_NOPE)),
        'g_q_rope': gain(14, (Ld, MLA_ROPE)),
        'g_k_nope': gain(15, (Ld, MLA_NOPE)),
        'g_k_rope': gain(16, (Ld, MLA_ROPE)),
        'w_out': nrm(17, (Ld, MIX_WIDTH, D_MODEL), MIX_WIDTH),
        'g_norm_mlp': gain(18, (Ld, D_MODEL)),
        'w_mlp_in': nrm(19, (Ld, D_MODEL, D_FF), D_MODEL),
        'w_mlp_out': nrm(20, (Ld, D_FF, D_MODEL), D_FF),
    }


def reference(x, c, positions, w_mod, b_mod, g_norm_mix, w_in, g_q_dil, g_k_dil, g_cq, w_q_up,
              g_ckv, w_kv_up, g_q_nope, g_q_rope, g_k_nope, g_k_rope, w_out, g_norm_mlp,
              w_mlp_in, w_mlp_out):
    B, S, _ = x.shape
    slopes = alibi_slopes(N_HEADS_DIL)
    cuts = [int(v) for v in np.cumsum(IN_SPLITS)[:-1]]
    c_act = jax.nn.silu(c)
    for l in range(DEPTH):
        mod = c_act @ w_mod[l] + b_mod[l]
        sh1, sc1, gt1, sh2, sc2, gt2 = jnp.split(mod, 6, axis=-1)

        h = rms_norm(x, g_norm_mix[l]) * (1.0 + sc1[:, None]) + sh1[:, None]
        proj = h @ w_in[l]
        q_d, k_d, v_d, cq, ckv, k_rope = jnp.split(proj, cuts, axis=-1)
        q_d = rms_norm(q_d.reshape(B, S, N_HEADS_DIL, HEAD_DIM), g_q_dil[l])
        k_d = rms_norm(k_d.reshape(B, S, N_HEADS_DIL, HEAD_DIM), g_k_dil[l])
        v_d = v_d.reshape(B, S, N_HEADS_DIL, HEAD_DIM)
        o_dil = dilated_attention(q_d, k_d, v_d, positions, slopes).reshape(B, S, -1).astype(x.dtype)
        o_mla = latent_attention(cq, ckv, k_rope, positions, g_cq[l], w_q_up[l], g_ckv[l], w_kv_up[l],
                                 g_q_nope[l], g_q_rope[l], g_k_nope[l], g_k_rope[l]).astype(x.dtype)
        mix = jnp.concatenate([o_dil, o_mla], axis=-1) @ w_out[l]
        x = x + gt1[:, None] * mix

        h = rms_norm(x, g_norm_mlp[l]) * (1.0 + sc2[:, None]) + sh2[:, None]
        y = jnp.square(jax.nn.relu(h @ w_mlp_in[l])) @ w_mlp_out[l]
        x = x + gt2[:, None] * y
    return x
```

```python
import functools
import math

import jax
import jax.numpy as jnp
from jax import lax
from jax.experimental import pallas as pl
from jax.experimental.pallas import tpu as pltpu

HEAD_DIM = 64
N_HEADS = 8
HALF_WIDTH = N_HEADS * HEAD_DIM
ROPE_DIM = HEAD_DIM // 2
DIL_CONFIGS = ((128, 1), (512, 4), (2048, 16))
ROPE_THETA = 10000.0
EPS = 1e-6
NEG_INF = -1e30
LANES = 128
LOG2E = math.log2(math.e)
VMEM_LIMIT = 56 * 1024 * 1024

BF16 = jnp.bfloat16
F32 = jnp.float32


def _params(n_axes):
    return pltpu.CompilerParams(
        dimension_semantics=("arbitrary",) * n_axes, vmem_limit_bytes=VMEM_LIMIT)


def _lane_iota(shape):
    return lax.broadcasted_iota(jnp.int32, shape, len(shape) - 1)


def _dot_t(a, b):
    return lax.dot_general(a, b, (((1,), (1,)), ((), ())), preferred_element_type=F32)


def _dot(a, b):
    return jnp.dot(a, b, preferred_element_type=F32)


def _rms(x, width):
    return x * lax.rsqrt(jnp.sum(x * x, axis=-1, keepdims=True) * (1.0 / width) + EPS)


def _swap_rope_halves(y):
    up = pltpu.roll(y, LANES - ROPE_DIM // 2, axis=1)
    down = pltpu.roll(y, ROPE_DIM // 2, axis=1)
    return jnp.where(_lane_iota(y.shape) < HEAD_DIM + ROPE_DIM // 2, up, down)


def _rope_table_kernel(pos_ref, inv_ref, sign_ref, cos_ref, sin_ref):
    ang = pos_ref[0].astype(F32) * inv_ref[...]
    cos_ref[0] = jnp.cos(ang)
    sin_ref[0] = jnp.sin(ang) * sign_ref[...]


def _rope_tables(positions, ts=1024):
    B, S = positions.shape
    half = ROPE_DIM // 2
    inv = ROPE_THETA ** (-jnp.arange(half, dtype=F32) / half)
    zeros = jnp.zeros((HEAD_DIM,), F32)
    inv_row = jnp.concatenate([zeros, inv, inv, jnp.zeros((LANES - HEAD_DIM - ROPE_DIM,), F32)])[None]
    sign_row = jnp.concatenate([zeros, -jnp.ones((half,), F32), jnp.ones((half,), F32),
                                jnp.zeros((LANES - HEAD_DIM - ROPE_DIM,), F32)])[None]
    out = jax.ShapeDtypeStruct((B, S, LANES), F32)
    return pl.pallas_call(
        _rope_table_kernel,
        grid=(B, S // ts),
        in_specs=[pl.BlockSpec((1, ts, 1), lambda b, i: (b, i, 0)),
                  pl.BlockSpec((1, LANES), lambda b, i: (0, 0)),
                  pl.BlockSpec((1, LANES), lambda b, i: (0, 0))],
        out_specs=[pl.BlockSpec((1, ts, LANES), lambda b, i: (b, i, 0))] * 2,
        out_shape=[out, out],
        compiler_params=_params(2),
        name="rope_tables",
    )(positions[..., None], inv_row, sign_row)


def _mod_kernel(ct_ref, w_ref, b_ref, o_ref):
    ct = ct_ref[...]
    act = ct * (1.0 / (1.0 + jnp.exp(-ct)))
    w = w_ref[0]
    for b in range(ct.shape[1]):
        o_ref[0, b:b + 1, :] = jnp.sum(act[:, b:b + 1] * w, axis=0, keepdims=True) + b_ref[0]


def _modulation(c, w_mod, b_mod, tn=1536):
    depth, D, n_out = w_mod.shape
    B = c.shape[0]
    return pl.pallas_call(
        _mod_kernel,
        grid=(depth, n_out // tn),
        in_specs=[pl.BlockSpec((D, B), lambda l, j: (0, 0)),
                  pl.BlockSpec((1, D, tn), lambda l, j: (l, 0, j)),
                  pl.BlockSpec((1, 1, tn), lambda l, j: (l, 0, j))],
        out_specs=pl.BlockSpec((1, B, tn), lambda l, j: (l, 0, j)),
        out_shape=jax.ShapeDtypeStruct((depth, B, n_out), F32),
        compiler_params=_params(2),
        name="modulation",
    )(c.T, w_mod, b_mod[:, None, :])


def _head_pair_norm(t):
    sq = t * t
    low = _lane_iota(t.shape) < HEAD_DIM
    tot = jnp.sum(sq, axis=-1, keepdims=True)
    lo = jnp.sum(jnp.where(low, sq, 0.0), axis=-1, keepdims=True)
    ss = jnp.where(low, lo, tot - lo)
    return t * lax.rsqrt(ss * (1.0 / HEAD_DIM) + EPS)


def _inproj_kernel(x_ref, mod_ref, gmix_ref, win_ref, gqd_ref, gkd_ref, gcq_ref, wq_ref,
                   gckv_ref, wk_ref, wv_ref, gq_ref, gk_ref, gkr_ref, cos_ref, sin_ref,
                   qd_ref, kd_ref, vd_ref, qc_ref, kc_ref, vm_ref, *, q_scale):
    D = x_ref.shape[-1]
    x = x_ref[0]
    h = _rms(x, D) * gmix_ref[...]
    h = (h * (1.0 + mod_ref[0, 1:2, :]) + mod_ref[0, 0:1, :]).astype(BF16)
    cos = cos_ref[0]
    sin = sin_ref[0]
    low = _lane_iota(cos.shape) < HEAD_DIM
    W = HALF_WIDTH

    for name, out_ref, g_ref, scale in (("q", qd_ref, gqd_ref, HEAD_DIM ** -0.5), ("k", kd_ref, gkd_ref, 1.0)):
        base = 0 if name == "q" else W
        for p in range(W // LANES):
            t = _dot(h, win_ref[:, base + p * LANES: base + (p + 1) * LANES])
            out_ref[0, :, p * LANES:(p + 1) * LANES] = (_head_pair_norm(t) * (g_ref[...] * scale)).astype(BF16)
    vd_ref[0] = _dot(h, win_ref[:, 2 * W:3 * W]).astype(BF16)

    q_lora = wq_ref.shape[0]
    kv_lora = wk_ref.shape[0]
    o_cq = 3 * W
    o_ckv = o_cq + q_lora
    o_kr = o_ckv + kv_lora
    cq = _dot(h, win_ref[:, o_cq:o_ckv])
    cqn = (_rms(cq, q_lora) * gcq_ref[...]).astype(BF16)
    for hh in range(N_HEADS):
        t = _dot(cqn, wq_ref[:, hh * LANES:(hh + 1) * LANES])
        sq = t * t
        tot = jnp.sum(sq, axis=-1, keepdims=True)
        nope = jnp.sum(jnp.where(low, sq, 0.0), axis=-1, keepdims=True)
        r = jnp.where(low, lax.rsqrt(nope * (1.0 / HEAD_DIM) + EPS),
                      lax.rsqrt((tot - nope) * (1.0 / ROPE_DIM) + EPS))
        y = t * r * gq_ref[...]
        y = y * cos + _swap_rope_halves(y) * sin
        qc_ref[0, :, hh * LANES:(hh + 1) * LANES] = (y * q_scale).astype(BF16)

    ckv = _dot(h, win_ref[:, o_ckv:o_kr])
    ckvn = (_rms(ckv, kv_lora) * gckv_ref[...]).astype(BF16)
    vm_ref[0] = _dot(ckvn, wv_ref[...]).astype(BF16)
    kr = _dot(h, win_ref[:, o_kr:o_kr + LANES])
    kr = _rms(kr, ROPE_DIM) * gkr_ref[...]
    kr = kr * cos + _swap_rope_halves(kr) * sin
    for hh in range(N_HEADS):
        t = _dot(ckvn, wk_ref[:, hh * LANES:(hh + 1) * LANES])
        kc_ref[0, :, hh * LANES:(hh + 1) * LANES] = (_rms(t, HEAD_DIM) * gk_ref[...] + kr).astype(BF16)


def _const_spec(shape):
    return pl.BlockSpec(shape, lambda b, i: (0,) * len(shape), pipeline_mode=pl.Buffered(1))


def _inproj(x, mod_l, cos_t, sin_t, p, tm=512):
    B, S, D = x.shape
    W = HALF_WIDTH
    consts = [p["g_norm_mix"], p["w_in"], p["g_q_dil"], p["g_k_dil"], p["g_cq"], p["w_q_up"],
              p["g_ckv"], p["w_k_up"], p["w_v_up"], p["g_q"], p["g_k"], p["g_kr"]]
    tok = lambda width: pl.BlockSpec((1, tm, width), lambda b, i: (b, i, 0))
    out_widths = (W, W, W, N_HEADS * LANES, N_HEADS * LANES, W)
    q_scale = (HEAD_DIM + ROPE_DIM) ** -0.5 * LOG2E
    return pl.pallas_call(
        functools.partial(_inproj_kernel, q_scale=q_scale),
        grid=(B, S // tm),
        in_specs=[tok(D), pl.BlockSpec((1, 6, D), lambda b, i: (b, 0, 0))]
        + [_const_spec(a.shape) for a in consts] + [tok(LANES), tok(LANES)],
        out_specs=[tok(w) for w in out_widths],
        out_shape=[jax.ShapeDtypeStruct((B, S, w), BF16) for w in out_widths],
        compiler_params=_params(2),
        name="inproj",
    )(x, mod_l, *consts, cos_t, sin_t)


def _alibi_slope(h):
    return 2.0 ** (-8.0 * (h + 1) / N_HEADS)


def _dilated_kernel(*refs, half, n_kblk, length, merge):
    q_ref = refs[0]
    k_refs = refs[1:1 + n_kblk]
    v_refs = refs[1 + n_kblk:1 + 2 * n_kblk]
    pq_ref, pk_ref = refs[1 + 2 * n_kblk:3 + 2 * n_kblk]
    rest = refs[3 + 2 * n_kblk:]
    qb = q_ref.shape[1]
    nkeys = n_kblk * half
    n = pl.program_id(2)

    q = q_ref[0]
    k_all = jnp.concatenate([r[0] for r in k_refs], axis=0)
    v_all = jnp.concatenate([r[0] for r in v_refs], axis=0)
    dist = jnp.abs(pq_ref[0, 0] - pk_ref[0, 0, 0]).astype(F32)
    row = lax.broadcasted_iota(jnp.int32, (qb, nkeys), 0)
    col = lax.broadcasted_iota(jnp.int32, (qb, nkeys), 1)
    key_u = n * qb - half + col
    mask = (jnp.abs(col - half - row) <= half) & (key_u >= 0) & (key_u < length)
    low = _lane_iota((qb, LANES)) < HEAD_DIM

    for p in range(HALF_WIDTH // LANES):
        sl = slice(p * LANES, (p + 1) * LANES)
        qp, kp, vp = q[:, sl], k_all[:, sl], v_all[:, sl]
        outs, lses = [], []
        for hh in range(2):
            qm = jnp.where(low if hh == 0 else ~low, qp, jnp.zeros_like(qp))
            s = _dot_t(qm, kp) - _alibi_slope(2 * p + hh) * dist
            s = jnp.where(mask, s, NEG_INF)
            m = jnp.max(s, axis=-1, keepdims=True)
            e = jnp.exp(s - m)
            l = jnp.sum(e, axis=-1, keepdims=True)
            outs.append(_dot(e.astype(BF16), vp) / l)
            lses.append(m + jnp.log(l))
        o = jnp.where(low, outs[0], outs[1])
        lse = jnp.where(low, lses[0], lses[1])
        if merge:
            others = [(rest[2 * c][0, :, sl], rest[2 * c + 1][0, :, sl]) for c in range((len(rest) - 1) // 2)]
            top = lse
            for _, lc in others:
                top = jnp.maximum(top, lc)
            w = jnp.exp(lse - top)
            num, den = o * w, w
            for oc, lc in others:
                w = jnp.exp(lc - top)
                num, den = num + oc * w, den + w
            rest[-1][0, :, sl] = (num / den).astype(rest[-1].dtype)
        else:
            rest[0][0, :, sl] = o
            rest[1][0, :, sl] = lse


def _strided_positions(positions, dil, qb, half):
    B, S = positions.shape
    L = S // dil
    pos_s = positions.reshape(B, L, dil).transpose(0, 2, 1)
    padded = jnp.pad(pos_s, ((0, 0), (0, 0), (half, qb)))
    nb = L // qb
    parts = [padded[:, :, off:off + L].reshape(B, dil, nb, qb)[..., :w]
             for off, w in ((0, qb), (qb, 2 * half))]
    return pos_s[..., None], jnp.concatenate(parts, axis=-1)[:, :, :, None, :]


def _dilated_config(qd, kd, vd, pos_q, pos_k, window, dil, prev=None, qb=128):
    B, S, W = qd.shape
    half = window // (2 * dil)
    L = S // dil
    assert S % dil == 0 and L % qb == 0 and qb % half == 0 and 2 * half <= qb
    n_kblk = qb // half + 2
    kblocks = L // half
    view = lambda a: a.reshape(B, L, dil * W)
    q_spec = pl.BlockSpec((1, qb, W), lambda b, r, n: (b, n, r))

    def kv_spec(j):
        def index(b, r, n):
            return (b, jnp.clip(n * (qb // half) - 1 + j, 0, kblocks - 1), r)
        return pl.BlockSpec((1, half, W), index)

    in_specs = ([q_spec] + [kv_spec(j) for j in range(n_kblk)] * 2
                + [pl.BlockSpec((1, 1, qb, 1), lambda b, r, n: (b, r, n, 0)),
                   pl.BlockSpec((1, 1, 1, 1, qb + 2 * half), lambda b, r, n: (b, r, n, 0, 0))])
    args = [view(qd)] + [view(kd)] * n_kblk + [view(vd)] * n_kblk + [pos_q, pos_k]
    merge = prev is not None
    if merge:
        assert dil == 1
        in_specs += [q_spec] * len(prev)
        args += list(prev)
        out_specs = q_spec
        out_shape = jax.ShapeDtypeStruct((B, L, dil * W), BF16)
    else:
        out_specs = [q_spec, q_spec]
        out_shape = [jax.ShapeDtypeStruct((B, L, dil * W), F32)] * 2
    out = pl.pallas_call(
        functools.partial(_dilated_kernel, half=half, n_kblk=n_kblk, length=L, merge=merge),
        grid=(B, dil, L // qb),
        in_specs=in_specs, out_specs=out_specs, out_shape=out_shape,
        compiler_params=_params(3),
        name=f"dilated_d{dil}",
    )(*args)
    if merge:
        return out.reshape(B, S, W)
    return [o.reshape(B, S, W) for o in out]


def _mla_kernel(q_ref, k_ref, v_ref, o_ref, *, tk):
    tq = q_ref.shape[1]
    n_chunks = k_ref.shape[1] // tk
    qs = [q_ref[0, :, hh * LANES:(hh + 1) * LANES] for hh in range(2)]

    def body(c, carry):
        start = pl.multiple_of(c * tk, tk)
        v = v_ref[0, pl.ds(start, tk), :]
        new = []
        for hh in range(2):
            m, l, acc = carry[hh]
            k = k_ref[0, pl.ds(start, tk), hh * LANES:(hh + 1) * LANES]
            s = _dot_t(qs[hh], k)
            m_new = jnp.maximum(m, jnp.max(s, axis=-1, keepdims=True))
            alpha = jnp.exp2(m - m_new)
            e = jnp.exp2(s - m_new)
            l = alpha * l + jnp.sum(e, axis=-1, keepdims=True)
            acc = alpha * acc + _dot(e.astype(BF16), v)
            new.append((m_new, l, acc))
        return tuple(new)

    init = tuple((jnp.full((tq, 1), -jnp.inf, F32), jnp.zeros((tq, 1), F32), jnp.zeros((tq, LANES), F32))
                 for _ in range(2))
    (_, l0, a0), (_, l1, a1) = lax.fori_loop(0, n_chunks, body, init)
    low = _lane_iota((tq, LANES)) < HEAD_DIM
    o_ref[0] = jnp.where(low, a0 / l0, a1 / l1).astype(o_ref.dtype)


def _latent_attention(qc, kc, vm, tq=512, tk=512):
    B, S, _ = qc.shape
    pair = 2 * LANES
    return pl.pallas_call(
        functools.partial(_mla_kernel, tk=tk),
        grid=(B, N_HEADS // 2, S // tq),
        in_specs=[pl.BlockSpec((1, tq, pair), lambda b, p, i: (b, i, p)),
                  pl.BlockSpec((1, S, pair), lambda b, p, i: (b, 0, p)),
                  pl.BlockSpec((1, S, LANES), lambda b, p, i: (b, 0, p))],
        out_specs=pl.BlockSpec((1, tq, LANES), lambda b, p, i: (b, i, p)),
        out_shape=jax.ShapeDtypeStruct((B, S, HALF_WIDTH), BF16),
        compiler_params=_params(3),
        name="latent_attention",
    )(qc, kc, vm)


def _mlp_kernel(x_ref, od_ref, om_ref, mod_ref, wout_ref, g_ref, w1_ref, w2_ref, o_ref, *, tf):
    D = x_ref.shape[-1]
    W = od_ref.shape[-1]
    mix = _dot(od_ref[0], wout_ref[0:W, :]) + _dot(om_ref[0], wout_ref[W:2 * W, :])
    x1 = x_ref[0] + mod_ref[0, 2:3, :] * mix
    h = _rms(x1, D) * g_ref[...]
    h = (h * (1.0 + mod_ref[0, 4:5, :]) + mod_ref[0, 3:4, :]).astype(BF16)
    y = jnp.zeros_like(x1)
    for c in range(w1_ref.shape[1] // tf):
        a = jnp.maximum(_dot(h, w1_ref[:, c * tf:(c + 1) * tf]), 0.0)
        y = y + _dot((a * a).astype(BF16), w2_ref[c * tf:(c + 1) * tf, :])
    o_ref[0] = x1 + mod_ref[0, 5:6, :] * y


def _outproj_mlp(x, o_dil, o_mla, mod_l, p, tm=512, tf=512):
    B, S, D = x.shape
    tok = lambda width: pl.BlockSpec((1, tm, width), lambda b, i: (b, i, 0))
    consts = [p["w_out"], p["g_norm_mlp"], p["w_mlp_in"], p["w_mlp_out"]]
    return pl.pallas_call(
        functools.partial(_mlp_kernel, tf=tf),
        grid=(B, S // tm),
        in_specs=[tok(D), tok(HALF_WIDTH), tok(HALF_WIDTH), pl.BlockSpec((1, 6, D), lambda b, i: (b, 0, 0))]
        + [_const_spec(a.shape) for a in consts],
        out_specs=tok(D),
        out_shape=jax.ShapeDtypeStruct((B, S, D), F32),
        compiler_params=_params(2),
        name="outproj_mlp",
    )(x, o_dil, o_mla, mod_l, *consts)


def _layer_params(l, g_norm_mix, w_in, g_q_dil, g_k_dil, g_cq, w_q_up, g_ckv, w_kv_up, g_q_nope,
                  g_q_rope, g_k_nope, g_k_rope, w_out, g_norm_mlp, w_mlp_in, w_mlp_out):
    W = HALF_WIDTH
    q_lora, kv_lora = g_cq.shape[-1], g_ckv.shape[-1]
    D = w_in.shape[1]
    pad = LANES - HEAD_DIM - ROPE_DIM
    o_kr = 3 * W + q_lora + kv_lora
    w_kr = jnp.pad(w_in[l][:, o_kr:o_kr + ROPE_DIM], ((0, 0), (HEAD_DIM, pad)))
    wq = w_q_up[l].reshape(q_lora, N_HEADS, HEAD_DIM + ROPE_DIM)
    wq = jnp.pad(wq, ((0, 0), (0, 0), (0, pad))).reshape(q_lora, N_HEADS * LANES)
    wkv = w_kv_up[l].reshape(kv_lora, N_HEADS, 2 * HEAD_DIM)
    wk = jnp.pad(wkv[:, :, :HEAD_DIM], ((0, 0), (0, 0), (0, LANES - HEAD_DIM))).reshape(kv_lora, N_HEADS * LANES)
    wv = wkv[:, :, HEAD_DIM:].reshape(kv_lora, W)
    row = lambda v: v.astype(F32)[None, :]
    return {
        "g_norm_mix": row(g_norm_mix[l]),
        "w_in": jnp.concatenate([w_in[l][:, :o_kr], w_kr], axis=1).astype(BF16),
        "g_q_dil": row(jnp.tile(g_q_dil[l], 2)),
        "g_k_dil": row(jnp.tile(g_k_dil[l], 2)),
        "g_cq": row(g_cq[l]),
        "w_q_up": wq.astype(BF16),
        "g_ckv": row(g_ckv[l]),
        "w_k_up": wk.astype(BF16),
        "w_v_up": wv.astype(BF16),
        "g_q": row(jnp.pad(jnp.concatenate([g_q_nope[l], g_q_rope[l]]), (0, pad))),
        "g_k": row(jnp.pad(g_k_nope[l], (0, LANES - HEAD_DIM))),
        "g_kr": row(jnp.pad(g_k_rope[l], (HEAD_DIM, pad))),
        "w_out": w_out[l].astype(BF16),
        "g_norm_mlp": row(g_norm_mlp[l]),
        "w_mlp_in": w_mlp_in[l].astype(BF16),
        "w_mlp_out": w_mlp_out[l].astype(BF16),
    }


def kernel(x, c, positions, w_mod, b_mod, g_norm_mix, w_in, g_q_dil, g_k_dil, g_cq, w_q_up, g_ckv, w_kv_up, g_q_nope, g_q_rope, g_k_nope, g_k_rope, w_out, g_norm_mlp, w_mlp_in, w_mlp_out):
    B, S, D = x.shape
    depth = w_mod.shape[0]
    qb = 128
    cos_t, sin_t = _rope_tables(positions)
    mod = _modulation(c, w_mod, b_mod).reshape(depth, B, 6, D)
    pos_views = [_strided_positions(positions, dil, qb, window // (2 * dil)) for window, dil in DIL_CONFIGS]
    for l in range(depth):
        p = _layer_params(l, g_norm_mix, w_in, g_q_dil, g_k_dil, g_cq, w_q_up, g_ckv, w_kv_up, g_q_nope,
                          g_q_rope, g_k_nope, g_k_rope, w_out, g_norm_mlp, w_mlp_in, w_mlp_out)
        qd, kd, vd, qc, kc, vm = _inproj(x, mod[l], cos_t, sin_t, p)
        prev = []
        for (window, dil), (pos_q, pos_k) in list(zip(DIL_CONFIGS, pos_views))[:0:-1]:
            prev += _dilated_config(qd, kd, vd, pos_q, pos_k, window, dil, qb=qb)
        (window, dil), (pos_q, pos_k) = DIL_CONFIGS[0], pos_views[0]
        o_dil = _dilated_config(qd, kd, vd, pos_q, pos_k, window, dil, prev=prev, qb=qb)
        o_mla = _latent_attention(qc, kc, vm)
        x = _outproj_mlp(x, o_dil, o_mla, mod[l], p)
    return x
```

```python
import functools
import math

import jax
import jax.numpy as jnp
from jax import lax
from jax.experimental import pallas as pl
from jax.experimental.pallas import tpu as pltpu

HEAD_DIM = 64
N_HEADS = 8
HALF_WIDTH = N_HEADS * HEAD_DIM
ROPE_DIM = HEAD_DIM // 2
DIL_CONFIGS = ((128, 1), (512, 4), (2048, 16))
ROPE_THETA = 10000.0
EPS = 1e-6
NEG_INF = -1e30
LANES = 128
LOG2E = math.log2(math.e)
VMEM_LIMIT = 56 * 1024 * 1024
MAX_RAW_SCORE = 60.0

BF16 = jnp.bfloat16
F32 = jnp.float32


def _params(n_axes):
    return pltpu.CompilerParams(
        dimension_semantics=("arbitrary",) * n_axes, vmem_limit_bytes=VMEM_LIMIT)


def _lane_iota(shape):
    return lax.broadcasted_iota(jnp.int32, shape, len(shape) - 1)


def _dot_t(a, b):
    return lax.dot_general(a, b, (((1,), (1,)), ((), ())), preferred_element_type=F32)


def _dot(a, b):
    return jnp.dot(a, b, preferred_element_type=F32)


def _rms(x, width):
    return x * lax.rsqrt(jnp.sum(x * x, axis=-1, keepdims=True) * (1.0 / width) + EPS)


def _swap_rope_halves(y):
    up = pltpu.roll(y, LANES - ROPE_DIM // 2, axis=1)
    down = pltpu.roll(y, ROPE_DIM // 2, axis=1)
    return jnp.where(_lane_iota(y.shape) < HEAD_DIM + ROPE_DIM // 2, up, down)


def _rope_table_kernel(pos_ref, inv_ref, sign_ref, cos_ref, sin_ref):
    ang = pos_ref[0].astype(F32) * inv_ref[...]
    cos_ref[0] = jnp.cos(ang)
    sin_ref[0] = jnp.sin(ang) * sign_ref[...]


def _rope_tables(positions, ts=1024):
    B, S = positions.shape
    half = ROPE_DIM // 2
    inv = ROPE_THETA ** (-jnp.arange(half, dtype=F32) / half)
    zeros = jnp.zeros((HEAD_DIM,), F32)
    inv_row = jnp.concatenate([zeros, inv, inv, jnp.zeros((LANES - HEAD_DIM - ROPE_DIM,), F32)])[None]
    sign_row = jnp.concatenate([zeros, -jnp.ones((half,), F32), jnp.ones((half,), F32),
                                jnp.zeros((LANES - HEAD_DIM - ROPE_DIM,), F32)])[None]
    out = jax.ShapeDtypeStruct((B, S, LANES), F32)
    return pl.pallas_call(
        _rope_table_kernel,
        grid=(B, S // ts),
        in_specs=[pl.BlockSpec((1, ts, 1), lambda b, i: (b, i, 0)),
                  pl.BlockSpec((1, LANES), lambda b, i: (0, 0)),
                  pl.BlockSpec((1, LANES), lambda b, i: (0, 0))],
        out_specs=[pl.BlockSpec((1, ts, LANES), lambda b, i: (b, i, 0))] * 2,
        out_shape=[out, out],
        compiler_params=_params(2),
        name="rope_tables",
    )(positions[..., None], inv_row, sign_row)


def _mod_kernel(ct_ref, w_ref, b_ref, o_ref):
    ct = ct_ref[...]
    act = ct * (1.0 / (1.0 + jnp.exp(-ct)))
    w = w_ref[0]
    for b in range(ct.shape[1]):
        o_ref[0, b:b + 1, :] = jnp.sum(act[:, b:b + 1] * w, axis=0, keepdims=True) + b_ref[0]


def _modulation(c, w_mod, b_mod, tn=1536):
    depth, D, n_out = w_mod.shape
    B = c.shape[0]
    return pl.pallas_call(
        _mod_kernel,
        grid=(depth, n_out // tn),
        in_specs=[pl.BlockSpec((D, B), lambda l, j: (0, 0)),
                  pl.BlockSpec((1, D, tn), lambda l, j: (l, 0, j)),
                  pl.BlockSpec((1, 1, tn), lambda l, j: (l, 0, j))],
        out_specs=pl.BlockSpec((1, B, tn), lambda l, j: (l, 0, j)),
        out_shape=jax.ShapeDtypeStruct((depth, B, n_out), F32),
        compiler_params=_params(2),
        name="modulation",
    )(c.T, w_mod, b_mod[:, None, :])


def _head_pair_norm(t):
    sq = t * t
    low = _lane_iota(t.shape) < HEAD_DIM
    tot = jnp.sum(sq, axis=-1, keepdims=True)
    lo = jnp.sum(jnp.where(low, sq, 0.0), axis=-1, keepdims=True)
    ss = jnp.where(low, lo, tot - lo)
    return t * lax.rsqrt(ss * (1.0 / HEAD_DIM) + EPS)


def _inproj_kernel(x_ref, mod_ref, gmix_ref, win_ref, gqd_ref, gkd_ref, gcq_ref, wq_ref,
                   gckv_ref, wk_ref, wv_ref, vone_ref, gq_ref, gk_ref, gkr_ref, cos_ref, sin_ref,
                   qd_ref, kd_ref, vd_ref, qc_ref, kc_ref, vm_ref, *, q_scale):
    D = x_ref.shape[-1]
    x = x_ref[0]
    h = _rms(x, D) * gmix_ref[...]
    h = (h * (1.0 + mod_ref[0, 1:2, :]) + mod_ref[0, 0:1, :]).astype(BF16)
    cos = cos_ref[0]
    sin = sin_ref[0]
    low = _lane_iota(cos.shape) < HEAD_DIM
    W = HALF_WIDTH

    for name, out_ref, g_ref, scale in (("q", qd_ref, gqd_ref, HEAD_DIM ** -0.5), ("k", kd_ref, gkd_ref, 1.0)):
        base = 0 if name == "q" else W
        for p in range(W // LANES):
            if p % 2 == 0:
                t2 = _dot(h, win_ref[:, base + p * LANES: base + (p + 2) * LANES])
            t = t2[:, (p % 2) * LANES:(p % 2 + 1) * LANES]
            out_ref[0, :, p * LANES:(p + 1) * LANES] = (_head_pair_norm(t) * (g_ref[...] * scale)).astype(BF16)
    vd_ref[0] = _dot(h, win_ref[:, 2 * W:3 * W]).astype(BF16)

    q_lora = wq_ref.shape[0]
    kv_lora = wk_ref.shape[0]
    o_cq = 3 * W
    o_ckv = o_cq + q_lora
    o_kr = o_ckv + kv_lora
    cq = _dot(h, win_ref[:, o_cq:o_ckv])
    cqn = (_rms(cq, q_lora) * gcq_ref[...]).astype(BF16)
    for hh in range(N_HEADS):
        if hh % 2 == 0:
            t2 = _dot(cqn, wq_ref[:, hh * LANES:(hh + 2) * LANES])
        t = t2[:, (hh % 2) * LANES:(hh % 2 + 1) * LANES]
        sq = t * t
        tot = jnp.sum(sq, axis=-1, keepdims=True)
        nope = jnp.sum(jnp.where(low, sq, 0.0), axis=-1, keepdims=True)
        r = jnp.where(low, lax.rsqrt(nope * (1.0 / HEAD_DIM) + EPS),
                      lax.rsqrt((tot - nope) * (1.0 / ROPE_DIM) + EPS))
        y = t * r * gq_ref[...]
        y = y * cos + _swap_rope_halves(y) * sin
        qc_ref[0, :, hh * LANES:(hh + 1) * LANES] = (y * q_scale).astype(BF16)

    ckv_kr = _dot(h, win_ref[:, o_ckv:o_kr + LANES])
    ckvn = (_rms(ckv_kr[:, :kv_lora], kv_lora) * gckv_ref[...]).astype(BF16)
    vm_ref[0] = (_dot(ckvn, wv_ref[...]) + vone_ref[...]).astype(BF16)
    kr = ckv_kr[:, kv_lora:]
    kr = _rms(kr, ROPE_DIM) * gkr_ref[...]
    kr = kr * cos + _swap_rope_halves(kr) * sin
    for hh in range(N_HEADS):
        if hh % 2 == 0:
            t2 = _dot(ckvn, wk_ref[:, hh * LANES:(hh + 2) * LANES])
        t = t2[:, (hh % 2) * LANES:(hh % 2 + 1) * LANES]
        kc_ref[0, :, hh * LANES:(hh + 1) * LANES] = (_rms(t, HEAD_DIM) * gk_ref[...] + kr).astype(BF16)


def _const_spec(shape):
    return pl.BlockSpec(shape, lambda b, i: (0,) * len(shape), pipeline_mode=pl.Buffered(1))


def _inproj(x, mod_l, cos_t, sin_t, p, tm=512):
    B, S, D = x.shape
    W = HALF_WIDTH
    consts = [p["g_norm_mix"], p["w_in"], p["g_q_dil"], p["g_k_dil"], p["g_cq"], p["w_q_up"],
              p["g_ckv"], p["w_k_up"], p["w_v_up"], p["v_ones"], p["g_q"], p["g_k"], p["g_kr"]]
    tok = lambda width: pl.BlockSpec((1, tm, width), lambda b, i: (b, i, 0))
    out_widths = (W, W, W, N_HEADS * LANES, N_HEADS * LANES, N_HEADS * LANES)
    q_scale = (HEAD_DIM + ROPE_DIM) ** -0.5 * LOG2E
    return pl.pallas_call(
        functools.partial(_inproj_kernel, q_scale=q_scale),
        grid=(B, S // tm),
        in_specs=[tok(D), pl.BlockSpec((1, 6, D), lambda b, i: (b, 0, 0))]
        + [_const_spec(a.shape) for a in consts] + [tok(LANES), tok(LANES)],
        out_specs=[tok(w) for w in out_widths],
        out_shape=[jax.ShapeDtypeStruct((B, S, w), BF16) for w in out_widths],
        compiler_params=_params(2),
        name="inproj",
    )(x, mod_l, *consts, cos_t, sin_t)


def _alibi_slope(h):
    return 2.0 ** (-8.0 * (h + 1) / N_HEADS)


def _dilated_kernel(*refs, half, n_kblk, length, merge):
    q_ref = refs[0]
    k_refs = refs[1:1 + n_kblk]
    v_refs = refs[1 + n_kblk:1 + 2 * n_kblk]
    pq_ref, pk_ref = refs[1 + 2 * n_kblk:3 + 2 * n_kblk]
    rest = refs[3 + 2 * n_kblk:]
    qb = q_ref.shape[1]
    nkeys = n_kblk * half
    n = pl.program_id(2)

    q = q_ref[0]
    k_all = jnp.concatenate([r[0] for r in k_refs], axis=0)
    v_all = jnp.concatenate([r[0] for r in v_refs], axis=0)
    dist = jnp.abs(pq_ref[0, 0] - pk_ref[0, 0, 0]).astype(F32)
    row = lax.broadcasted_iota(jnp.int32, (qb, nkeys), 0)
    col = lax.broadcasted_iota(jnp.int32, (qb, nkeys), 1)
    key_u = n * qb - half + col
    mask = (jnp.abs(col - half - row) <= half) & (key_u >= 0) & (key_u < length)
    low = _lane_iota((qb, LANES)) < HEAD_DIM

    for p in range(HALF_WIDTH // LANES):
        sl = slice(p * LANES, (p + 1) * LANES)
        qp, kp, vp = q[:, sl], k_all[:, sl], v_all[:, sl]
        outs, lses = [], []
        for hh in range(2):
            qm = jnp.where(low if hh == 0 else ~low, qp, jnp.zeros_like(qp))
            s = _dot_t(qm, kp) - _alibi_slope(2 * p + hh) * dist
            s = jnp.where(mask, s, NEG_INF)
            m = jnp.max(s, axis=-1, keepdims=True)
            e = jnp.exp(s - m)
            l = jnp.sum(e, axis=-1, keepdims=True)
            outs.append(_dot(e.astype(BF16), vp) / l)
            lses.append(m + jnp.log(l))
        o = jnp.where(low, outs[0], outs[1])
        lse = jnp.where(low, lses[0], lses[1])
        if merge:
            others = [(rest[2 * c][0, :, sl], rest[2 * c + 1][0, :, sl]) for c in range((len(rest) - 1) // 2)]
            top = lse
            for _, lc in others:
                top = jnp.maximum(top, lc)
            w = jnp.exp(lse - top)
            num, den = o * w, w
            for oc, lc in others:
                w = jnp.exp(lc - top)
                num, den = num + oc * w, den + w
            rest[-1][0, :, sl] = (num / den).astype(rest[-1].dtype)
        else:
            rest[0][0, :, sl] = o
            rest[1][0, :, sl] = lse


def _strided_positions(positions, dil, qb, half):
    B, S = positions.shape
    L = S // dil
    pos_s = positions.reshape(B, L, dil).transpose(0, 2, 1)
    padded = jnp.pad(pos_s, ((0, 0), (0, 0), (half, qb)))
    nb = L // qb
    parts = [padded[:, :, off:off + L].reshape(B, dil, nb, qb)[..., :w]
             for off, w in ((0, qb), (qb, 2 * half))]
    return pos_s[..., None], jnp.concatenate(parts, axis=-1)[:, :, :, None, :]


def _dilated_config(qd, kd, vd, pos_q, pos_k, window, dil, prev=None, qb=128):
    B, S, W = qd.shape
    half = window // (2 * dil)
    L = S // dil
    assert S % dil == 0 and L % qb == 0 and qb % half == 0 and 2 * half <= qb
    n_kblk = qb // half + 2
    kblocks = L // half
    view = lambda a: a.reshape(B, L, dil * W)
    q_spec = pl.BlockSpec((1, qb, W), lambda b, r, n: (b, n, r))

    def kv_spec(j):
        def index(b, r, n):
            return (b, jnp.clip(n * (qb // half) - 1 + j, 0, kblocks - 1), r)
        return pl.BlockSpec((1, half, W), index)

    in_specs = ([q_spec] + [kv_spec(j) for j in range(n_kblk)] * 2
                + [pl.BlockSpec((1, 1, qb, 1), lambda b, r, n: (b, r, n, 0)),
                   pl.BlockSpec((1, 1, 1, 1, qb + 2 * half), lambda b, r, n: (b, r, n, 0, 0))])
    args = [view(qd)] + [view(kd)] * n_kblk + [view(vd)] * n_kblk + [pos_q, pos_k]
    merge = prev is not None
    if merge:
        assert dil == 1
        in_specs += [q_spec] * len(prev)
        args += list(prev)
        out_specs = q_spec
        out_shape = jax.ShapeDtypeStruct((B, L, dil * W), BF16)
    else:
        out_specs = [q_spec, q_spec]
        out_shape = [jax.ShapeDtypeStruct((B, L, dil * W), F32)] * 2
    out = pl.pallas_call(
        functools.partial(_dilated_kernel, half=half, n_kblk=n_kblk, length=L, merge=merge),
        grid=(B, dil, L // qb),
        in_specs=in_specs, out_specs=out_specs, out_shape=out_shape,
        compiler_params=_params(3),
        name=f"dilated_d{dil}",
    )(*args)
    if merge:
        return out.reshape(B, S, W)
    return [o.reshape(B, S, W) for o in out]


def _mla_chunk_refs(q_ref, k_ref, v_ref, tk):
    qs = [q_ref[0, :, hh * LANES:(hh + 1) * LANES] for hh in range(2)]

    def chunk(c, hh):
        start = pl.multiple_of(c * tk, tk)
        sl = slice(hh * LANES, (hh + 1) * LANES)
        return qs[hh], k_ref[0, pl.ds(start, tk), sl], v_ref[0, pl.ds(start, tk), sl]

    return chunk


def _mla_bounded_kernel(q_ref, k_ref, v_ref, o_ref, *, tk):
    tq = q_ref.shape[1]
    chunk = _mla_chunk_refs(q_ref, k_ref, v_ref, tk)

    def body(c, accs):
        new = []
        for hh in range(2):
            q, k, v = chunk(c, hh)
            e = jnp.exp2(_dot_t(q, k)).astype(BF16)
            new.append(accs[hh] + _dot(e, v))
        return tuple(new)

    zero = jnp.zeros((tq, LANES), F32)
    a0, a1 = lax.fori_loop(0, k_ref.shape[1] // tk, body, (zero, zero))
    low = _lane_iota((tq, LANES)) < HEAD_DIM
    o_ref[0] = jnp.where(low, a0 / a0[:, HEAD_DIM:HEAD_DIM + 1], a1 / a1[:, 0:1]).astype(o_ref.dtype)


def _mla_online_kernel(q_ref, k_ref, v_ref, o_ref, *, tk):
    tq = q_ref.shape[1]
    chunk = _mla_chunk_refs(q_ref, k_ref, v_ref, tk)

    def body(c, carry):
        new = []
        for hh in range(2):
            m, l, acc = carry[hh]
            q, k, v = chunk(c, hh)
            s = _dot_t(q, k)
            m_new = jnp.maximum(m, jnp.max(s, axis=-1, keepdims=True))
            alpha = jnp.exp2(m - m_new)
            e = jnp.exp2(s - m_new)
            l = alpha * l + jnp.sum(e, axis=-1, keepdims=True)
            acc = alpha * acc + _dot(e.astype(BF16), v)
            new.append((m_new, l, acc))
        return tuple(new)

    init = tuple((jnp.full((tq, 1), -jnp.inf, F32), jnp.zeros((tq, 1), F32), jnp.zeros((tq, LANES), F32))
                 for _ in range(2))
    (_, l0, a0), (_, l1, a1) = lax.fori_loop(0, k_ref.shape[1] // tk, body, init)
    low = _lane_iota((tq, LANES)) < HEAD_DIM
    o_ref[0] = jnp.where(low, a0 / l0, a1 / l1).astype(o_ref.dtype)


def _latent_attention(qc, kc, vm, score_bound, tq=512, tk_bounded=2048, tk_online=512):
    B, S, _ = qc.shape
    pair = 2 * LANES

    def call(body, name, tk):
        return pl.pallas_call(
            functools.partial(body, tk=tk),
            grid=(B, N_HEADS // 2, S // tq),
            in_specs=[pl.BlockSpec((1, tq, pair), lambda b, p, i: (b, i, p)),
                      pl.BlockSpec((1, S, pair), lambda b, p, i: (b, 0, p)),
                      pl.BlockSpec((1, S, pair), lambda b, p, i: (b, 0, p))],
            out_specs=pl.BlockSpec((1, tq, LANES), lambda b, p, i: (b, i, p)),
            out_shape=jax.ShapeDtypeStruct((B, S, HALF_WIDTH), BF16),
            compiler_params=_params(3),
            name=name,
        )

    return lax.cond(score_bound <= MAX_RAW_SCORE,
                    call(_mla_bounded_kernel, "latent_attention", tk_bounded),
                    call(_mla_online_kernel, "latent_attention_online", tk_online),
                    qc, kc, vm)


def _mlp_kernel(x_ref, od_ref, om_ref, mod_ref, wout_ref, g_ref, w1_ref, w2_ref, o_ref, *, tf):
    D = x_ref.shape[-1]
    W = od_ref.shape[-1]
    mix = _dot(od_ref[0], wout_ref[0:W, :]) + _dot(om_ref[0], wout_ref[W:2 * W, :])
    x1 = x_ref[0] + mod_ref[0, 2:3, :] * mix
    h = _rms(x1, D) * g_ref[...]
    h = (h * (1.0 + mod_ref[0, 4:5, :]) + mod_ref[0, 3:4, :]).astype(BF16)
    y = jnp.zeros_like(x1)
    for c in range(w1_ref.shape[1] // tf):
        a = jnp.maximum(_dot(h, w1_ref[:, c * tf:(c + 1) * tf]), 0.0)
        y = y + _dot((a * a).astype(BF16), w2_ref[c * tf:(c + 1) * tf, :])
    o_ref[0] = x1 + mod_ref[0, 5:6, :] * y


def _outproj_mlp(x, o_dil, o_mla, mod_l, p, tm=512, tf=512):
    B, S, D = x.shape
    tok = lambda width: pl.BlockSpec((1, tm, width), lambda b, i: (b, i, 0))
    consts = [p["w_out"], p["g_norm_mlp"], p["w_mlp_in"], p["w_mlp_out"]]
    return pl.pallas_call(
        functools.partial(_mlp_kernel, tf=tf),
        grid=(B, S // tm),
        in_specs=[tok(D), tok(HALF_WIDTH), tok(HALF_WIDTH), pl.BlockSpec((1, 6, D), lambda b, i: (b, 0, 0))]
        + [_const_spec(a.shape) for a in consts],
        out_specs=tok(D),
        out_shape=jax.ShapeDtypeStruct((B, S, D), F32),
        compiler_params=_params(2),
        name="outproj_mlp",
    )(x, o_dil, o_mla, mod_l, *consts)


def _layer_params(l, g_norm_mix, w_in, g_q_dil, g_k_dil, g_cq, w_q_up, g_ckv, w_kv_up, g_q_nope,
                  g_q_rope, g_k_nope, g_k_rope, w_out, g_norm_mlp, w_mlp_in, w_mlp_out):
    W = HALF_WIDTH
    q_lora, kv_lora = g_cq.shape[-1], g_ckv.shape[-1]
    D = w_in.shape[1]
    pad = LANES - HEAD_DIM - ROPE_DIM
    o_kr = 3 * W + q_lora + kv_lora
    w_kr = jnp.pad(w_in[l][:, o_kr:o_kr + ROPE_DIM], ((0, 0), (HEAD_DIM, pad)))
    wq = w_q_up[l].reshape(q_lora, N_HEADS, HEAD_DIM + ROPE_DIM)
    wq = jnp.pad(wq, ((0, 0), (0, 0), (0, pad))).reshape(q_lora, N_HEADS * LANES)
    wkv = w_kv_up[l].reshape(kv_lora, N_HEADS, 2 * HEAD_DIM)
    wk = jnp.pad(wkv[:, :, :HEAD_DIM], ((0, 0), (0, 0), (0, LANES - HEAD_DIM))).reshape(kv_lora, N_HEADS * LANES)
    wv = wkv[:, :, HEAD_DIM:].reshape(kv_lora, N_HEADS // 2, 2, HEAD_DIM)
    wv = jnp.stack([jnp.pad(wv[:, :, 0], ((0, 0), (0, 0), (0, HEAD_DIM))),
                    jnp.pad(wv[:, :, 1], ((0, 0), (0, 0), (HEAD_DIM, 0)))], axis=2).reshape(kv_lora, N_HEADS * LANES)
    v_ones = jnp.zeros((N_HEADS // 2, 2, LANES), F32).at[:, 0, HEAD_DIM].set(1.0).at[:, 1, 0].set(1.0)
    gmax2 = lambda gn, gr: HEAD_DIM * jnp.max(gn[l] ** 2) + ROPE_DIM * jnp.max(gr[l] ** 2)
    score_bound = 1.02 * LOG2E * (HEAD_DIM + ROPE_DIM) ** -0.5 * jnp.sqrt(
        gmax2(g_q_nope, g_q_rope) * gmax2(g_k_nope, g_k_rope))
    row = lambda v: v.astype(F32)[None, :]
    return {
        "g_norm_mix": row(g_norm_mix[l]),
        "w_in": jnp.concatenate([w_in[l][:, :o_kr], w_kr], axis=1).astype(BF16),
        "g_q_dil": row(jnp.tile(g_q_dil[l], 2)),
        "g_k_dil": row(jnp.tile(g_k_dil[l], 2)),
        "g_cq": row(g_cq[l]),
        "w_q_up": wq.astype(BF16),
        "g_ckv": row(g_ckv[l]),
        "w_k_up": wk.astype(BF16),
        "w_v_up": wv.astype(BF16),
        "v_ones": v_ones.reshape(1, N_HEADS * LANES),
        "score_bound": score_bound,
        "g_q": row(jnp.pad(jnp.concatenate([g_q_nope[l], g_q_rope[l]]), (0, pad))),
        "g_k": row(jnp.pad(g_k_nope[l], (0, LANES - HEAD_DIM))),
        "g_kr": row(jnp.pad(g_k_rope[l], (HEAD_DIM, pad))),
        "w_out": w_out[l].astype(BF16),
        "g_norm_mlp": row(g_norm_mlp[l]),
        "w_mlp_in": w_mlp_in[l].astype(BF16),
        "w_mlp_out": w_mlp_out[l].astype(BF16),
    }


def kernel(x, c, positions, w_mod, b_mod, g_norm_mix, w_in, g_q_dil, g_k_dil, g_cq, w_q_up, g_ckv, w_kv_up, g_q_nope, g_q_rope, g_k_nope, g_k_rope, w_out, g_norm_mlp, w_mlp_in, w_mlp_out):
    B, S, D = x.shape
    depth = w_mod.shape[0]
    qb = 128
    cos_t, sin_t = _rope_tables(positions)
    mod = _modulation(c, w_mod, b_mod).reshape(depth, B, 6, D)
    pos_views = [_strided_positions(positions, dil, qb, window // (2 * dil)) for window, dil in DIL_CONFIGS]
    for l in range(depth):
        p = _layer_params(l, g_norm_mix, w_in, g_q_dil, g_k_dil, g_cq, w_q_up, g_ckv, w_kv_up, g_q_nope,
                          g_q_rope, g_k_nope, g_k_rope, w_out, g_norm_mlp, w_mlp_in, w_mlp_out)
        qd, kd, vd, qc, kc, vm = _inproj(x, mod[l], cos_t, sin_t, p)
        prev = []
        for (window, dil), (pos_q, pos_k) in list(zip(DIL_CONFIGS, pos_views))[:0:-1]:
            prev += _dilated_config(qd, kd, vd, pos_q, pos_k, window, dil, qb=qb)
        (window, dil), (pos_q, pos_k) = DIL_CONFIGS[0], pos_views[0]
        o_dil = _dilated_config(qd, kd, vd, pos_q, pos_k, window, dil, prev=prev, qb=qb)
        o_mla = _latent_attention(qc, kc, vm, p["score_bound"])
        x = _outproj_mlp(x, o_dil, o_mla, mod[l], p)
    return x
```

```python
import functools
import math

import jax
import jax.numpy as jnp
from jax import lax
from jax.experimental import pallas as pl
from jax.experimental.pallas import tpu as pltpu

HEAD_DIM = 64
N_HEADS = 8
HALF_WIDTH = N_HEADS * HEAD_DIM
ROPE_DIM = HEAD_DIM // 2
DIL_CONFIGS = ((128, 1), (512, 4), (2048, 16))
ROPE_THETA = 10000.0
EPS = 1e-6
NEG_INF = -1e30
LANES = 128
BF16_ROWS = 16
VT_ROWS = HEAD_DIM + BF16_ROWS
LOG2E = math.log2(math.e)
VMEM_LIMIT = 56 * 1024 * 1024
MAX_RAW_SCORE = 60.0

BF16 = jnp.bfloat16
F32 = jnp.float32


def _params(n_axes):
    return pltpu.CompilerParams(
        dimension_semantics=("arbitrary",) * n_axes, vmem_limit_bytes=VMEM_LIMIT)


def _lane_iota(shape):
    return lax.broadcasted_iota(jnp.int32, shape, len(shape) - 1)


def _dot_t(a, b):
    return lax.dot_general(a, b, (((1,), (1,)), ((), ())), preferred_element_type=F32)


def _dot(a, b):
    return jnp.dot(a, b, preferred_element_type=F32)


def _rms(x, width):
    return x * lax.rsqrt(jnp.sum(x * x, axis=-1, keepdims=True) * (1.0 / width) + EPS)


def _swap_rope_halves(y):
    up = pltpu.roll(y, LANES - ROPE_DIM // 2, axis=1)
    down = pltpu.roll(y, ROPE_DIM // 2, axis=1)
    return jnp.where(_lane_iota(y.shape) < HEAD_DIM + ROPE_DIM // 2, up, down)


def _rope_table_kernel(pos_ref, inv_ref, sign_ref, cos_ref, sin_ref):
    ang = pos_ref[0].astype(F32) * inv_ref[...]
    cos_ref[0] = jnp.cos(ang)
    sin_ref[0] = jnp.sin(ang) * sign_ref[...]


def _rope_tables(positions, ts=1024):
    B, S = positions.shape
    half = ROPE_DIM // 2
    inv = ROPE_THETA ** (-jnp.arange(half, dtype=F32) / half)
    zeros = jnp.zeros((HEAD_DIM,), F32)
    inv_row = jnp.concatenate([zeros, inv, inv, jnp.zeros((LANES - HEAD_DIM - ROPE_DIM,), F32)])[None]
    sign_row = jnp.concatenate([zeros, -jnp.ones((half,), F32), jnp.ones((half,), F32),
                                jnp.zeros((LANES - HEAD_DIM - ROPE_DIM,), F32)])[None]
    out = jax.ShapeDtypeStruct((B, S, LANES), F32)
    return pl.pallas_call(
        _rope_table_kernel,
        grid=(B, S // ts),
        in_specs=[pl.BlockSpec((1, ts, 1), lambda b, i: (b, i, 0)),
                  pl.BlockSpec((1, LANES), lambda b, i: (0, 0)),
                  pl.BlockSpec((1, LANES), lambda b, i: (0, 0))],
        out_specs=[pl.BlockSpec((1, ts, LANES), lambda b, i: (b, i, 0))] * 2,
        out_shape=[out, out],
        compiler_params=_params(2),
        name="rope_tables",
    )(positions[..., None], inv_row, sign_row)


def _mod_kernel(ct_ref, w_ref, b_ref, o_ref):
    ct = ct_ref[...]
    act = ct * (1.0 / (1.0 + jnp.exp(-ct)))
    w = w_ref[0]
    for b in range(ct.shape[1]):
        o_ref[0, b:b + 1, :] = jnp.sum(act[:, b:b + 1] * w, axis=0, keepdims=True) + b_ref[0]


def _modulation(c, w_mod, b_mod, tn=1536):
    depth, D, n_out = w_mod.shape
    B = c.shape[0]
    return pl.pallas_call(
        _mod_kernel,
        grid=(depth, n_out // tn),
        in_specs=[pl.BlockSpec((D, B), lambda l, j: (0, 0)),
                  pl.BlockSpec((1, D, tn), lambda l, j: (l, 0, j)),
                  pl.BlockSpec((1, 1, tn), lambda l, j: (l, 0, j))],
        out_specs=pl.BlockSpec((1, B, tn), lambda l, j: (l, 0, j)),
        out_shape=jax.ShapeDtypeStruct((depth, B, n_out), F32),
        compiler_params=_params(2),
        name="modulation",
    )(c.T, w_mod, b_mod[:, None, :])


def _head_pair_norm(t):
    sq = t * t
    low = _lane_iota(t.shape) < HEAD_DIM
    tot = jnp.sum(sq, axis=-1, keepdims=True)
    lo = jnp.sum(jnp.where(low, sq, 0.0), axis=-1, keepdims=True)
    ss = jnp.where(low, lo, tot - lo)
    return t * lax.rsqrt(ss * (1.0 / HEAD_DIM) + EPS)


def _inproj_kernel(x_ref, mod_ref, gmix_ref, win_ref, gqd_ref, gkd_ref, gcq_ref, wq_ref,
                   gckv_ref, wk_ref, wvt_ref, vone_ref, gq_ref, gk_ref, gkr_ref, cos_ref, sin_ref,
                   qd_ref, kd_ref, vd_ref, qc_ref, kc_ref, vt_ref, *, q_scale):
    D = x_ref.shape[-1]
    x = x_ref[0]
    h = _rms(x, D) * gmix_ref[...]
    h = (h * (1.0 + mod_ref[0, 1:2, :]) + mod_ref[0, 0:1, :]).astype(BF16)
    cos = cos_ref[0]
    sin = sin_ref[0]
    low = _lane_iota(cos.shape) < HEAD_DIM
    W = HALF_WIDTH

    for name, out_ref, g_ref, scale in (("q", qd_ref, gqd_ref, HEAD_DIM ** -0.5), ("k", kd_ref, gkd_ref, 1.0)):
        base = 0 if name == "q" else W
        for p in range(W // LANES):
            if p % 2 == 0:
                t2 = _dot(h, win_ref[:, base + p * LANES: base + (p + 2) * LANES])
            t = t2[:, (p % 2) * LANES:(p % 2 + 1) * LANES]
            out_ref[0, :, p * LANES:(p + 1) * LANES] = (_head_pair_norm(t) * (g_ref[...] * scale)).astype(BF16)
    vd_ref[0] = _dot(h, win_ref[:, 2 * W:3 * W]).astype(BF16)

    q_lora = wq_ref.shape[0]
    kv_lora = wk_ref.shape[0]
    o_cq = 3 * W
    o_ckv = o_cq + q_lora
    o_kr = o_ckv + kv_lora
    cq = _dot(h, win_ref[:, o_cq:o_ckv])
    cqn = (_rms(cq, q_lora) * gcq_ref[...]).astype(BF16)
    for hh in range(N_HEADS):
        if hh % 2 == 0:
            t2 = _dot(cqn, wq_ref[:, hh * LANES:(hh + 2) * LANES])
        t = t2[:, (hh % 2) * LANES:(hh % 2 + 1) * LANES]
        sq = t * t
        tot = jnp.sum(sq, axis=-1, keepdims=True)
        nope = jnp.sum(jnp.where(low, sq, 0.0), axis=-1, keepdims=True)
        r = jnp.where(low, lax.rsqrt(nope * (1.0 / HEAD_DIM) + EPS),
                      lax.rsqrt((tot - nope) * (1.0 / ROPE_DIM) + EPS))
        y = t * r * gq_ref[...]
        y = y * cos + _swap_rope_halves(y) * sin
        qc_ref[0, :, hh * LANES:(hh + 1) * LANES] = (y * q_scale).astype(BF16)

    ckv_kr = _dot(h, win_ref[:, o_ckv:o_kr + LANES])
    ckvn = (_rms(ckv_kr[:, :kv_lora], kv_lora) * gckv_ref[...]).astype(BF16)
    vt = _dot_t(wvt_ref[...], ckvn) + vone_ref[...]
    pair_rows = 2 * VT_ROWS
    for p in range(N_HEADS // 2):
        vt_ref[0, p] = vt[p * pair_rows:(p + 1) * pair_rows].astype(BF16)
    kr = ckv_kr[:, kv_lora:]
    kr = _rms(kr, ROPE_DIM) * gkr_ref[...]
    kr = kr * cos + _swap_rope_halves(kr) * sin
    for hh in range(N_HEADS):
        if hh % 2 == 0:
            t2 = _dot(ckvn, wk_ref[:, hh * LANES:(hh + 2) * LANES])
        t = t2[:, (hh % 2) * LANES:(hh % 2 + 1) * LANES]
        kc_ref[0, :, hh * LANES:(hh + 1) * LANES] = (_rms(t, HEAD_DIM) * gk_ref[...] + kr).astype(BF16)


def _const_spec(shape):
    return pl.BlockSpec(shape, lambda b, i: (0,) * len(shape), pipeline_mode=pl.Buffered(1))


def _inproj(x, mod_l, cos_t, sin_t, p, tm=512):
    B, S, D = x.shape
    W = HALF_WIDTH
    consts = [p["g_norm_mix"], p["w_in"], p["g_q_dil"], p["g_k_dil"], p["g_cq"], p["w_q_up"],
              p["g_ckv"], p["w_k_up"], p["w_vt_up"], p["vt_ones"], p["g_q"], p["g_k"], p["g_kr"]]
    tok = lambda width: pl.BlockSpec((1, tm, width), lambda b, i: (b, i, 0))
    tok_widths = (W, W, W, N_HEADS * LANES, N_HEADS * LANES)
    vt_shape = (B, N_HEADS // 2, 2 * VT_ROWS, S)
    q_scale = (HEAD_DIM + ROPE_DIM) ** -0.5 * LOG2E
    return pl.pallas_call(
        functools.partial(_inproj_kernel, q_scale=q_scale),
        grid=(B, S // tm),
        in_specs=[tok(D), pl.BlockSpec((1, 6, D), lambda b, i: (b, 0, 0))]
        + [_const_spec(a.shape) for a in consts] + [tok(LANES), tok(LANES)],
        out_specs=[tok(w) for w in tok_widths]
        + [pl.BlockSpec((1,) + vt_shape[1:3] + (tm,), lambda b, i: (b, 0, 0, i))],
        out_shape=[jax.ShapeDtypeStruct((B, S, w), BF16) for w in tok_widths]
        + [jax.ShapeDtypeStruct(vt_shape, BF16)],
        compiler_params=_params(2),
        name="inproj",
    )(x, mod_l, *consts, cos_t, sin_t)


def _alibi_slope(h):
    return 2.0 ** (-8.0 * (h + 1) / N_HEADS)


def _halo_rows(prev_ref, main_ref, next_ref, lo, hi):
    rows = main_ref.shape[1]
    parts = []
    if lo < 0:
        parts.append(prev_ref[0])
    parts.append(main_ref[0, max(lo, 0):min(hi, rows), :])
    if hi > rows:
        parts.append(next_ref[0])
    return parts[0] if len(parts) == 1 else jnp.concatenate(parts, axis=0)


def _dilated_kernel(q_ref, kp_ref, km_ref, kn_ref, vp_ref, vm_ref, vn_ref, pq_ref, pk_ref, *rest,
                    qb, half, length, merge):
    step_rows = q_ref.shape[1]
    nkeys = qb + 2 * half
    n = pl.program_id(2)
    row = lax.broadcasted_iota(jnp.int32, (qb, nkeys), 0)
    col = lax.broadcasted_iota(jnp.int32, (qb, nkeys), 1)
    in_band = jnp.abs(col - half - row) <= half
    low = _lane_iota((qb, LANES)) < HEAD_DIM

    for j in range(step_rows // qb):
        rows = slice(j * qb, (j + 1) * qb)
        q = q_ref[0, rows, :]
        k_all = _halo_rows(kp_ref, km_ref, kn_ref, j * qb - half, (j + 1) * qb + half)
        v_all = _halo_rows(vp_ref, vm_ref, vn_ref, j * qb - half, (j + 1) * qb + half)
        dist = jnp.abs(pq_ref[0, 0, rows, :] - pk_ref[0, 0, j]).astype(F32)
        key_u = n * step_rows + j * qb - half + col
        mask = in_band & (key_u >= 0) & (key_u < length)

        for p in range(HALF_WIDTH // LANES):
            sl = slice(p * LANES, (p + 1) * LANES)
            qp, kp, vp = q[:, sl], k_all[:, sl], v_all[:, sl]
            outs, lses = [], []
            for hh in range(2):
                qm = jnp.where(low if hh == 0 else ~low, qp, jnp.zeros_like(qp))
                s = _dot_t(qm, kp) - _alibi_slope(2 * p + hh) * dist
                s = jnp.where(mask, s, NEG_INF)
                m = jnp.max(s, axis=-1, keepdims=True)
                e = jnp.exp(s - m)
                l = jnp.sum(e, axis=-1, keepdims=True)
                outs.append(_dot(e.astype(BF16), vp) / l)
                lses.append(m + jnp.log(l))
            o = jnp.where(low, outs[0], outs[1])
            lse = jnp.where(low, lses[0], lses[1])
            if merge:
                others = [(rest[2 * c][0, rows, sl], rest[2 * c + 1][0, rows, sl]) for c in range((len(rest) - 1) // 2)]
                top = lse
                for _, lc in others:
                    top = jnp.maximum(top, lc)
                w = jnp.exp(lse - top)
                num, den = o * w, w
                for oc, lc in others:
                    w = jnp.exp(lc - top)
                    num, den = num + oc * w, den + w
                rest[-1][0, rows, sl] = (num / den).astype(rest[-1].dtype)
            else:
                rest[0][0, rows, sl] = o
                rest[1][0, rows, sl] = lse


def _strided_positions(positions, dil, qb, half):
    B, S = positions.shape
    L = S // dil
    pos_s = positions.reshape(B, L, dil).transpose(0, 2, 1)
    padded = jnp.pad(pos_s, ((0, 0), (0, 0), (half, qb)))
    nb = L // qb
    parts = [padded[:, :, off:off + L].reshape(B, dil, nb, qb)[..., :w]
             for off, w in ((0, qb), (qb, 2 * half))]
    return pos_s[..., None], jnp.concatenate(parts, axis=-1)[:, :, :, None, :]


def _dilated_config(qd, kd, vd, pos_q, pos_k, window, dil, prev=None, qb=128, n_sub=4):
    B, S, W = qd.shape
    half = window // (2 * dil)
    L = S // dil
    step_rows = n_sub * qb
    assert S % dil == 0 and L % step_rows == 0 and step_rows % half == 0 and half % BF16_ROWS == 0
    halo_per_step = step_rows // half
    last_halo = L // half - 1
    view = lambda a: a.reshape(B, L, dil * W)
    main = pl.BlockSpec((1, step_rows, W), lambda b, r, n: (b, n, r))
    before = pl.BlockSpec((1, half, W), lambda b, r, n: (b, jnp.maximum(n * halo_per_step - 1, 0), r))
    after = pl.BlockSpec((1, half, W), lambda b, r, n: (b, jnp.minimum((n + 1) * halo_per_step, last_halo), r))

    in_specs = [main, before, main, after, before, main, after,
                pl.BlockSpec((1, 1, step_rows, 1), lambda b, r, n: (b, r, n, 0)),
                pl.BlockSpec((1, 1, n_sub, 1, qb + 2 * half), lambda b, r, n: (b, r, n, 0, 0))]
    args = [view(qd)] + [view(kd)] * 3 + [view(vd)] * 3 + [pos_q, pos_k]
    merge = prev is not None
    if merge:
        assert dil == 1
        in_specs += [main] * len(prev)
        args += list(prev)
        out_specs = main
        out_shape = jax.ShapeDtypeStruct((B, L, dil * W), BF16)
    else:
        out_specs = [main, main]
        out_shape = [jax.ShapeDtypeStruct((B, L, dil * W), F32)] * 2
    out = pl.pallas_call(
        functools.partial(_dilated_kernel, qb=qb, half=half, length=L, merge=merge),
        grid=(B, dil, L // step_rows),
        in_specs=in_specs, out_specs=out_specs, out_shape=out_shape,
        compiler_params=_params(3),
        name=f"dilated_d{dil}",
    )(*args)
    if merge:
        return out.reshape(B, S, W)
    return [o.reshape(B, S, W) for o in out]


def _mla_kernel(q_ref, k_ref, vt_ref, o_ref, *, tk, online):
    tq = q_ref.shape[1]
    qs = [q_ref[0, :, hh * LANES:(hh + 1) * LANES] for hh in range(2)]

    def body(c, carry):
        start = pl.multiple_of(c * tk, tk)
        sts = [_dot_t(k_ref[0, pl.ds(start, tk), hh * LANES:(hh + 1) * LANES], qs[hh]) for hh in range(2)]
        new = []
        for hh in range(2):
            vt = vt_ref[0, 0, hh * VT_ROWS:(hh + 1) * VT_ROWS, pl.ds(start, tk)]
            st = sts[hh]
            if online:
                m, acc = carry[hh]
                m_new = jnp.maximum(m, jnp.max(st, axis=0, keepdims=True))
                acc = jnp.exp2(m - m_new) * acc + _dot(vt, jnp.exp2(st - m_new).astype(BF16))
                new.append((m_new, acc))
            else:
                new.append(carry[hh] + _dot(vt, jnp.exp2(st).astype(BF16)))
        return tuple(new)

    zero = jnp.zeros((VT_ROWS, tq), F32)
    init = (jnp.full((1, tq), -jnp.inf, F32), zero) if online else zero
    res = lax.fori_loop(0, k_ref.shape[1] // tk, body, (init, init))
    accs = [r[1] if online else r for r in res]
    ot = jnp.concatenate([a[:HEAD_DIM] / a[HEAD_DIM:HEAD_DIM + 1] for a in accs], axis=0)
    o_ref[0] = ot.T.astype(o_ref.dtype)


def _latent_attention(qc, kc, vt, score_bound, tq=256, tk=4096):
    B, S, _ = qc.shape
    pair = 2 * LANES

    def call(online, name):
        return pl.pallas_call(
            functools.partial(_mla_kernel, tk=tk, online=online),
            grid=(B, N_HEADS // 2, S // tq),
            in_specs=[pl.BlockSpec((1, tq, pair), lambda b, p, i: (b, i, p)),
                      pl.BlockSpec((1, S, pair), lambda b, p, i: (b, 0, p)),
                      pl.BlockSpec((1, 1, 2 * VT_ROWS, S), lambda b, p, i: (b, p, 0, 0))],
            out_specs=pl.BlockSpec((1, tq, LANES), lambda b, p, i: (b, i, p)),
            out_shape=jax.ShapeDtypeStruct((B, S, HALF_WIDTH), BF16),
            compiler_params=_params(3),
            name=name,
        )

    return lax.cond(score_bound <= MAX_RAW_SCORE,
                    call(False, "latent_attention"), call(True, "latent_attention_online"),
                    qc, kc, vt)


def _mlp_kernel(x_ref, od_ref, om_ref, mod_ref, wout_ref, g_ref, w1_ref, w2_ref, o_ref, *, tf):
    D = x_ref.shape[-1]
    W = od_ref.shape[-1]
    mix = _dot(od_ref[0], wout_ref[0:W, :]) + _dot(om_ref[0], wout_ref[W:2 * W, :])
    x1 = x_ref[0] + mod_ref[0, 2:3, :] * mix
    h = _rms(x1, D) * g_ref[...]
    h = (h * (1.0 + mod_ref[0, 4:5, :]) + mod_ref[0, 3:4, :]).astype(BF16)
    y = jnp.zeros_like(x1)
    for c in range(w1_ref.shape[1] // tf):
        a = jnp.maximum(_dot(h, w1_ref[:, c * tf:(c + 1) * tf]), 0.0)
        y = y + _dot((a * a).astype(BF16), w2_ref[c * tf:(c + 1) * tf, :])
    o_ref[0] = x1 + mod_ref[0, 5:6, :] * y


def _outproj_mlp(x, o_dil, o_mla, mod_l, p, tm=512, tf=512):
    B, S, D = x.shape
    tok = lambda width: pl.BlockSpec((1, tm, width), lambda b, i: (b, i, 0))
    consts = [p["w_out"], p["g_norm_mlp"], p["w_mlp_in"], p["w_mlp_out"]]
    return pl.pallas_call(
        functools.partial(_mlp_kernel, tf=tf),
        grid=(B, S // tm),
        in_specs=[tok(D), tok(HALF_WIDTH), tok(HALF_WIDTH), pl.BlockSpec((1, 6, D), lambda b, i: (b, 0, 0))]
        + [_const_spec(a.shape) for a in consts],
        out_specs=tok(D),
        out_shape=jax.ShapeDtypeStruct((B, S, D), F32),
        compiler_params=_params(2),
        name="outproj_mlp",
    )(x, o_dil, o_mla, mod_l, *consts)


def _layer_params(l, g_norm_mix, w_in, g_q_dil, g_k_dil, g_cq, w_q_up, g_ckv, w_kv_up, g_q_nope,
                  g_q_rope, g_k_nope, g_k_rope, w_out, g_norm_mlp, w_mlp_in, w_mlp_out):
    W = HALF_WIDTH
    q_lora, kv_lora = g_cq.shape[-1], g_ckv.shape[-1]
    pad = LANES - HEAD_DIM - ROPE_DIM
    o_kr = 3 * W + q_lora + kv_lora
    w_kr = jnp.pad(w_in[l][:, o_kr:o_kr + ROPE_DIM], ((0, 0), (HEAD_DIM, pad)))
    wq = w_q_up[l].reshape(q_lora, N_HEADS, HEAD_DIM + ROPE_DIM)
    wq = jnp.pad(wq, ((0, 0), (0, 0), (0, pad))).reshape(q_lora, N_HEADS * LANES)
    wkv = w_kv_up[l].reshape(kv_lora, N_HEADS, 2 * HEAD_DIM)
    wk = jnp.pad(wkv[:, :, :HEAD_DIM], ((0, 0), (0, 0), (0, LANES - HEAD_DIM))).reshape(kv_lora, N_HEADS * LANES)
    wvt = jnp.pad(wkv[:, :, HEAD_DIM:].transpose(1, 2, 0), ((0, 0), (0, VT_ROWS - HEAD_DIM), (0, 0)))
    vt_ones = jnp.zeros((N_HEADS, VT_ROWS, 1), F32).at[:, HEAD_DIM].set(1.0)
    gmax2 = lambda gn, gr: HEAD_DIM * jnp.max(gn[l] ** 2) + ROPE_DIM * jnp.max(gr[l] ** 2)
    score_bound = 1.02 * LOG2E * (HEAD_DIM + ROPE_DIM) ** -0.5 * jnp.sqrt(
        gmax2(g_q_nope, g_q_rope) * gmax2(g_k_nope, g_k_rope))
    row = lambda v: v.astype(F32)[None, :]
    return {
        "g_norm_mix": row(g_norm_mix[l]),
        "w_in": jnp.concatenate([w_in[l][:, :o_kr], w_kr], axis=1).astype(BF16),
        "g_q_dil": row(jnp.tile(g_q_dil[l], 2)),
        "g_k_dil": row(jnp.tile(g_k_dil[l], 2)),
        "g_cq": row(g_cq[l]),
        "w_q_up": wq.astype(BF16),
        "g_ckv": row(g_ckv[l]),
        "w_k_up": wk.astype(BF16),
        "w_vt_up": wvt.reshape(N_HEADS * VT_ROWS, kv_lora).astype(BF16),
        "vt_ones": vt_ones.reshape(N_HEADS * VT_ROWS, 1),
        "score_bound": score_bound,
        "g_q": row(jnp.pad(jnp.concatenate([g_q_nope[l], g_q_rope[l]]), (0, pad))),
        "g_k": row(jnp.pad(g_k_nope[l], (0, LANES - HEAD_DIM))),
        "g_kr": row(jnp.pad(g_k_rope[l], (HEAD_DIM, pad))),
        "w_out": w_out[l].astype(BF16),
        "g_norm_mlp": row(g_norm_mlp[l]),
        "w_mlp_in": w_mlp_in[l].astype(BF16),
        "w_mlp_out": w_mlp_out[l].astype(BF16),
    }


def kernel(x, c, positions, w_mod, b_mod, g_norm_mix, w_in, g_q_dil, g_k_dil, g_cq, w_q_up, g_ckv, w_kv_up, g_q_nope, g_q_rope, g_k_nope, g_k_rope, w_out, g_norm_mlp, w_mlp_in, w_mlp_out):
    B, S, D = x.shape
    depth = w_mod.shape[0]
    qb = 128
    cos_t, sin_t = _rope_tables(positions)
    mod = _modulation(c, w_mod, b_mod).reshape(depth, B, 6, D)
    pos_views = [_strided_positions(positions, dil, qb, window // (2 * dil)) for window, dil in DIL_CONFIGS]
    for l in range(depth):
        p = _layer_params(l, g_norm_mix, w_in, g_q_dil, g_k_dil, g_cq, w_q_up, g_ckv, w_kv_up, g_q_nope,
                          g_q_rope, g_k_nope, g_k_rope, w_out, g_norm_mlp, w_mlp_in, w_mlp_out)
        qd, kd, vd, qc, kc, vt = _inproj(x, mod[l], cos_t, sin_t, p)
        prev = []
        for (window, dil), (pos_q, pos_k) in list(zip(DIL_CONFIGS, pos_views))[:0:-1]:
            prev += _dilated_config(qd, kd, vd, pos_q, pos_k, window, dil, qb=qb)
        (window, dil), (pos_q, pos_k) = DIL_CONFIGS[0], pos_views[0]
        o_dil = _dilated_config(qd, kd, vd, pos_q, pos_k, window, dil, prev=prev, qb=qb)
        o_mla = _latent_attention(qc, kc, vt, p["score_bound"])
        x = _outproj_mlp(x, o_dil, o_mla, mod[l], p)
    return x
```

```python
import functools
import math

import jax
import jax.numpy as jnp
from jax import lax
from jax.experimental import pallas as pl
from jax.experimental.pallas import tpu as pltpu

HEAD_DIM = 64
N_HEADS = 8
HALF_WIDTH = N_HEADS * HEAD_DIM
ROPE_DIM = HEAD_DIM // 2
DIL_CONFIGS = ((128, 1), (512, 4), (2048, 16))
RESIDUE_STEP = 4
ROPE_THETA = 10000.0
EPS = 1e-6
NEG_INF = -1e30
LANES = 128
BF16_ROWS = 16
VT_ROWS = HEAD_DIM + BF16_ROWS
LOG2E = math.log2(math.e)
VMEM_LIMIT = 56 * 1024 * 1024
MAX_RAW_SCORE = 60.0

BF16 = jnp.bfloat16
F32 = jnp.float32


def _params(n_axes):
    return pltpu.CompilerParams(
        dimension_semantics=("arbitrary",) * n_axes, vmem_limit_bytes=VMEM_LIMIT)


def _lane_iota(shape):
    return lax.broadcasted_iota(jnp.int32, shape, len(shape) - 1)


def _dot_t(a, b):
    return lax.dot_general(a, b, (((1,), (1,)), ((), ())), preferred_element_type=F32)


def _dot(a, b):
    return jnp.dot(a, b, preferred_element_type=F32)


def _rms(x, width):
    return x * lax.rsqrt(jnp.sum(x * x, axis=-1, keepdims=True) * (1.0 / width) + EPS)


def _swap_rope_halves(y):
    up = pltpu.roll(y, LANES - ROPE_DIM // 2, axis=1)
    down = pltpu.roll(y, ROPE_DIM // 2, axis=1)
    return jnp.where(_lane_iota(y.shape) < HEAD_DIM + ROPE_DIM // 2, up, down)


def _rope_table_kernel(pos_ref, inv_ref, sign_ref, cos_ref, sin_ref):
    ang = pos_ref[0].astype(F32) * inv_ref[...]
    cos_ref[0] = jnp.cos(ang)
    sin_ref[0] = jnp.sin(ang) * sign_ref[...]


def _rope_tables(positions, ts=1024):
    B, S = positions.shape
    half = ROPE_DIM // 2
    inv = ROPE_THETA ** (-jnp.arange(half, dtype=F32) / half)
    zeros = jnp.zeros((HEAD_DIM,), F32)
    inv_row = jnp.concatenate([zeros, inv, inv, jnp.zeros((LANES - HEAD_DIM - ROPE_DIM,), F32)])[None]
    sign_row = jnp.concatenate([zeros, -jnp.ones((half,), F32), jnp.ones((half,), F32),
                                jnp.zeros((LANES - HEAD_DIM - ROPE_DIM,), F32)])[None]
    out = jax.ShapeDtypeStruct((B, S, LANES), F32)
    return pl.pallas_call(
        _rope_table_kernel,
        grid=(B, S // ts),
        in_specs=[pl.BlockSpec((1, ts, 1), lambda b, i: (b, i, 0)),
                  pl.BlockSpec((1, LANES), lambda b, i: (0, 0)),
                  pl.BlockSpec((1, LANES), lambda b, i: (0, 0))],
        out_specs=[pl.BlockSpec((1, ts, LANES), lambda b, i: (b, i, 0))] * 2,
        out_shape=[out, out],
        compiler_params=_params(2),
        name="rope_tables",
    )(positions[..., None], inv_row, sign_row)


def _mod_kernel(ct_ref, w_ref, b_ref, o_ref):
    ct = ct_ref[...]
    act = ct * (1.0 / (1.0 + jnp.exp(-ct)))
    w = w_ref[0]
    for b in range(ct.shape[1]):
        o_ref[0, b:b + 1, :] = jnp.sum(act[:, b:b + 1] * w, axis=0, keepdims=True) + b_ref[0]


def _modulation(c, w_mod, b_mod, tn=1536):
    depth, D, n_out = w_mod.shape
    B = c.shape[0]
    return pl.pallas_call(
        _mod_kernel,
        grid=(depth, n_out // tn),
        in_specs=[pl.BlockSpec((D, B), lambda l, j: (0, 0)),
                  pl.BlockSpec((1, D, tn), lambda l, j: (l, 0, j)),
                  pl.BlockSpec((1, 1, tn), lambda l, j: (l, 0, j))],
        out_specs=pl.BlockSpec((1, B, tn), lambda l, j: (l, 0, j)),
        out_shape=jax.ShapeDtypeStruct((depth, B, n_out), F32),
        compiler_params=_params(2),
        name="modulation",
    )(c.T, w_mod, b_mod[:, None, :])


def _head_pair_norm(t):
    sq = t * t
    low = _lane_iota(t.shape) < HEAD_DIM
    tot = jnp.sum(sq, axis=-1, keepdims=True)
    lo = jnp.sum(jnp.where(low, sq, 0.0), axis=-1, keepdims=True)
    ss = jnp.where(low, lo, tot - lo)
    return t * lax.rsqrt(ss * (1.0 / HEAD_DIM) + EPS)


def _store_residue_layouts(tile, p, out_refs, stage_ref, gather_ref):
    natural_ref, by4_ref, by16_ref = out_refs
    rows = tile.shape[0]
    W = HALF_WIDTH
    step = RESIDUE_STEP
    natural_ref[0, :, p * LANES:(p + 1) * LANES] = tile.astype(BF16)
    stage_ref[...] = tile
    for a in range(step):
        part = stage_ref[pl.ds(a, rows // step, stride=step), :]
        by4_ref[0, :, a * W + p * LANES:a * W + (p + 1) * LANES] = part.astype(BF16)
        gather_ref[a] = part
        for b in range(step):
            sub = gather_ref[a, pl.ds(b, rows // step ** 2, stride=step), :]
            col = (step * b + a) * W + p * LANES
            by16_ref[0, :, col:col + LANES] = sub.astype(BF16)


def _inproj_kernel(x_ref, mod_ref, gmix_ref, win_ref, gqd_ref, gkd_ref, gcq_ref, wq_ref,
                   gckv_ref, wk_ref, wvt_ref, vone_ref, gq_ref, gk_ref, gkr_ref, cos_ref, sin_ref,
                   qd1_ref, qd4_ref, qd16_ref, kd1_ref, kd4_ref, kd16_ref, vd1_ref, vd4_ref, vd16_ref,
                   qc_ref, kc_ref, vt_ref, stage_ref, gather_ref, *, q_scale):
    D = x_ref.shape[-1]
    x = x_ref[0]
    h = _rms(x, D) * gmix_ref[...]
    h = (h * (1.0 + mod_ref[0, 1:2, :]) + mod_ref[0, 0:1, :]).astype(BF16)
    cos = cos_ref[0]
    sin = sin_ref[0]
    low = _lane_iota(cos.shape) < HEAD_DIM
    W = HALF_WIDTH

    n_tiles = W // LANES
    operands = (((qd1_ref, qd4_ref, qd16_ref), gqd_ref, HEAD_DIM ** -0.5),
                ((kd1_ref, kd4_ref, kd16_ref), gkd_ref, 1.0),
                ((vd1_ref, vd4_ref, vd16_ref), None, None))
    for o, (out_refs, g_ref, scale) in enumerate(operands):
        for p in range(n_tiles):
            if p % 2 == 0:
                t2 = _dot(h, win_ref[:, o * W + p * LANES:o * W + (p + 2) * LANES])
            t = t2[:, (p % 2) * LANES:(p % 2 + 1) * LANES]
            if g_ref is not None:
                t = _head_pair_norm(t) * (g_ref[...] * scale)
            _store_residue_layouts(t, p, out_refs, stage_ref.at[o * n_tiles + p], gather_ref.at[o * n_tiles + p])

    q_lora = wq_ref.shape[0]
    kv_lora = wk_ref.shape[0]
    o_cq = 3 * W
    o_ckv = o_cq + q_lora
    o_kr = o_ckv + kv_lora
    cq = _dot(h, win_ref[:, o_cq:o_ckv])
    cqn = (_rms(cq, q_lora) * gcq_ref[...]).astype(BF16)
    for hh in range(N_HEADS):
        if hh % 2 == 0:
            t2 = _dot(cqn, wq_ref[:, hh * LANES:(hh + 2) * LANES])
        t = t2[:, (hh % 2) * LANES:(hh % 2 + 1) * LANES]
        sq = t * t
        tot = jnp.sum(sq, axis=-1, keepdims=True)
        nope = jnp.sum(jnp.where(low, sq, 0.0), axis=-1, keepdims=True)
        r = jnp.where(low, lax.rsqrt(nope * (1.0 / HEAD_DIM) + EPS),
                      lax.rsqrt((tot - nope) * (1.0 / ROPE_DIM) + EPS))
        y = t * r * gq_ref[...]
        y = y * cos + _swap_rope_halves(y) * sin
        qc_ref[0, :, hh * LANES:(hh + 1) * LANES] = (y * q_scale).astype(BF16)

    ckv_kr = _dot(h, win_ref[:, o_ckv:o_kr + LANES])
    ckvn = (_rms(ckv_kr[:, :kv_lora], kv_lora) * gckv_ref[...]).astype(BF16)
    vt = _dot_t(wvt_ref[...], ckvn) + vone_ref[...]
    pair_rows = 2 * VT_ROWS
    for p in range(N_HEADS // 2):
        vt_ref[0, p] = vt[p * pair_rows:(p + 1) * pair_rows].astype(BF16)
    kr = ckv_kr[:, kv_lora:]
    kr = _rms(kr, ROPE_DIM) * gkr_ref[...]
    kr = kr * cos + _swap_rope_halves(kr) * sin
    for hh in range(N_HEADS):
        if hh % 2 == 0:
            t2 = _dot(ckvn, wk_ref[:, hh * LANES:(hh + 2) * LANES])
        t = t2[:, (hh % 2) * LANES:(hh % 2 + 1) * LANES]
        kc_ref[0, :, hh * LANES:(hh + 1) * LANES] = (_rms(t, HEAD_DIM) * gk_ref[...] + kr).astype(BF16)


def _const_spec(shape):
    return pl.BlockSpec(shape, lambda b, i: (0,) * len(shape), pipeline_mode=pl.Buffered(1))


def _inproj(x, mod_l, cos_t, sin_t, p, tm=512):
    B, S, D = x.shape
    W = HALF_WIDTH
    consts = [p["g_norm_mix"], p["w_in"], p["g_q_dil"], p["g_k_dil"], p["g_cq"], p["w_q_up"],
              p["g_ckv"], p["w_k_up"], p["w_vt_up"], p["vt_ones"], p["g_q"], p["g_k"], p["g_kr"]]
    assert tuple(d for _, d in DIL_CONFIGS) == (1, RESIDUE_STEP, RESIDUE_STEP ** 2)
    tok = lambda width, rows=tm: pl.BlockSpec((1, rows, width), lambda b, i: (b, i, 0))
    dil_shapes = [(S // d, d * W) for _ in range(3) for _, d in DIL_CONFIGS]
    tok_shapes = dil_shapes + [(S, N_HEADS * LANES)] * 2
    vt_shape = (B, N_HEADS // 2, 2 * VT_ROWS, S)
    q_scale = (HEAD_DIM + ROPE_DIM) ** -0.5 * LOG2E
    n_tiles = 3 * W // LANES
    outs = pl.pallas_call(
        functools.partial(_inproj_kernel, q_scale=q_scale),
        grid=(B, S // tm),
        in_specs=[tok(D), pl.BlockSpec((1, 6, D), lambda b, i: (b, 0, 0))]
        + [_const_spec(a.shape) for a in consts] + [tok(LANES), tok(LANES)],
        out_specs=[tok(width, tm * rows // S) for rows, width in tok_shapes]
        + [pl.BlockSpec((1,) + vt_shape[1:3] + (tm,), lambda b, i: (b, 0, 0, i))],
        out_shape=[jax.ShapeDtypeStruct((B,) + s, BF16) for s in tok_shapes]
        + [jax.ShapeDtypeStruct(vt_shape, BF16)],
        scratch_shapes=[pltpu.VMEM((n_tiles, tm, LANES), F32),
                        pltpu.VMEM((n_tiles, RESIDUE_STEP, tm // RESIDUE_STEP, LANES), F32)],
        compiler_params=_params(2),
        name="inproj",
    )(x, mod_l, *consts, cos_t, sin_t)
    n_dil = len(DIL_CONFIGS)
    qd, kd, vd = (outs[o * n_dil:(o + 1) * n_dil] for o in range(3))
    return qd, kd, vd, outs[-3], outs[-2], outs[-1]


def _alibi_slope(h):
    return 2.0 ** (-8.0 * (h + 1) / N_HEADS)


def _halo_rows(prev_ref, main_ref, next_ref, lo, hi):
    rows = main_ref.shape[1]
    parts = []
    if lo < 0:
        parts.append(prev_ref[0])
    parts.append(main_ref[0, max(lo, 0):min(hi, rows), :])
    if hi > rows:
        parts.append(next_ref[0])
    return parts[0] if len(parts) == 1 else jnp.concatenate(parts, axis=0)


def _to_token_order(blk_ref, dil, nat_ref, tmp_ref):
    W = HALF_WIDTH
    step = RESIDUE_STEP
    rows = nat_ref.shape[1]
    for p in range(W // LANES):
        for a in range(step):
            if dil == step:
                part = blk_ref[0, :, a * W + p * LANES:a * W + (p + 1) * LANES]
            else:
                for b in range(step):
                    col = (step * b + a) * W + p * LANES
                    tmp_ref[p, a, pl.ds(b, rows // step ** 2, stride=step), :] = blk_ref[0, :, col:col + LANES]
                part = tmp_ref[p, a]
            nat_ref[p, pl.ds(a, rows // step, stride=step), :] = part


def _dilated_kernel(q_ref, kp_ref, km_ref, kn_ref, vp_ref, vm_ref, vn_ref, pq_ref, pk_ref, *rest,
                    qb, half, length, merge_dils):
    step_rows = q_ref.shape[1]
    nkeys = qb + 2 * half
    n = pl.program_id(2)
    merge = bool(merge_dils)
    if merge:
        n_other = 2 * len(merge_dils)
        out_ref, nat_ref, tmp_ref = rest[n_other:]
        for c, dil in enumerate(merge_dils):
            for t in range(2):
                _to_token_order(rest[2 * c + t], dil, nat_ref.at[2 * c + t], tmp_ref.at[2 * c + t])
    row = lax.broadcasted_iota(jnp.int32, (qb, nkeys), 0)
    col = lax.broadcasted_iota(jnp.int32, (qb, nkeys), 1)
    in_band = jnp.abs(col - half - row) <= half
    low = _lane_iota((qb, LANES)) < HEAD_DIM

    for j in range(step_rows // qb):
        rows = slice(j * qb, (j + 1) * qb)
        q = q_ref[0, rows, :]
        k_all = _halo_rows(kp_ref, km_ref, kn_ref, j * qb - half, (j + 1) * qb + half)
        v_all = _halo_rows(vp_ref, vm_ref, vn_ref, j * qb - half, (j + 1) * qb + half)
        dist = jnp.abs(pq_ref[0, 0, rows, :] - pk_ref[0, 0, j]).astype(F32)
        key_u = n * step_rows + j * qb - half + col
        mask = in_band & (key_u >= 0) & (key_u < length)

        for p in range(HALF_WIDTH // LANES):
            sl = slice(p * LANES, (p + 1) * LANES)
            qp, kp, vp = q[:, sl], k_all[:, sl], v_all[:, sl]
            outs, lses = [], []
            for hh in range(2):
                qm = jnp.where(low if hh == 0 else ~low, qp, jnp.zeros_like(qp))
                s = _dot_t(qm, kp) - _alibi_slope(2 * p + hh) * dist
                s = jnp.where(mask, s, NEG_INF)
                m = jnp.max(s, axis=-1, keepdims=True)
                e = jnp.exp(s - m)
                l = jnp.sum(e, axis=-1, keepdims=True)
                outs.append(_dot(e.astype(BF16), vp) / l)
                lses.append(m + jnp.log(l))
            o = jnp.where(low, outs[0], outs[1])
            lse = jnp.where(low, lses[0], lses[1])
            if merge:
                others = [(nat_ref[2 * c, p, rows, :], nat_ref[2 * c + 1, p, rows, :]) for c in range(len(merge_dils))]
                top = lse
                for _, lc in others:
                    top = jnp.maximum(top, lc)
                w = jnp.exp(lse - top)
                num, den = o * w, w
                for oc, lc in others:
                    w = jnp.exp(lc - top)
                    num, den = num + oc * w, den + w
                out_ref[0, rows, sl] = (num / den).astype(out_ref.dtype)
            else:
                rest[0][0, rows, sl] = o
                rest[1][0, rows, sl] = lse


def _strided_positions(positions, dil, qb, half):
    B, S = positions.shape
    L = S // dil
    pos_s = positions.reshape(B, L, dil).transpose(0, 2, 1)
    padded = jnp.pad(pos_s, ((0, 0), (0, 0), (half, qb)))
    nb = L // qb
    parts = [padded[:, :, off:off + L].reshape(B, dil, nb, qb)[..., :w]
             for off, w in ((0, qb), (qb, 2 * half))]
    return pos_s[..., None], jnp.concatenate(parts, axis=-1)[:, :, :, None, :]


def _dilated_config(qd, kd, vd, pos_q, pos_k, window, dil, prev=(), qb=128, n_sub=4):
    B, L, width = qd.shape
    W = width // dil
    half = window // (2 * dil)
    step_rows = n_sub * qb
    assert L % step_rows == 0 and step_rows % half == 0 and half % BF16_ROWS == 0
    halo_per_step = step_rows // half
    last_halo = L // half - 1
    main = pl.BlockSpec((1, step_rows, W), lambda b, r, n: (b, n, r))
    before = pl.BlockSpec((1, half, W), lambda b, r, n: (b, jnp.maximum(n * halo_per_step - 1, 0), r))
    after = pl.BlockSpec((1, half, W), lambda b, r, n: (b, jnp.minimum((n + 1) * halo_per_step, last_halo), r))

    in_specs = [main, before, main, after, before, main, after,
                pl.BlockSpec((1, 1, step_rows, 1), lambda b, r, n: (b, r, n, 0)),
                pl.BlockSpec((1, 1, n_sub, 1, qb + 2 * half), lambda b, r, n: (b, r, n, 0, 0))]
    args = [qd] + [kd] * 3 + [vd] * 3 + [pos_q, pos_k]
    scratch = []
    if prev:
        assert dil == 1
        for d, o, lse in prev:
            in_specs += [pl.BlockSpec((1, step_rows // d, d * W), lambda b, r, n: (b, n, 0))] * 2
            args += [o, lse]
        n_tiles = W // LANES
        scratch = [pltpu.VMEM((2 * len(prev), n_tiles, step_rows, LANES), F32),
                   pltpu.VMEM((2 * len(prev), n_tiles, RESIDUE_STEP, step_rows // RESIDUE_STEP, LANES), F32)]
        out_specs = main
        out_shape = jax.ShapeDtypeStruct((B, L, W), BF16)
    else:
        out_specs = [main, main]
        out_shape = [jax.ShapeDtypeStruct((B, L, dil * W), F32)] * 2
    return pl.pallas_call(
        functools.partial(_dilated_kernel, qb=qb, half=half, length=L, merge_dils=tuple(d for d, _, _ in prev)),
        grid=(B, dil, L // step_rows),
        in_specs=in_specs, out_specs=out_specs, out_shape=out_shape,
        scratch_shapes=scratch,
        compiler_params=_params(3),
        name=f"dilated_d{dil}",
    )(*args)


def _mla_kernel(q_ref, k_ref, vt_ref, o_ref, *, tk, online):
    tq = q_ref.shape[1]
    qs = [q_ref[0, :, hh * LANES:(hh + 1) * LANES] for hh in range(2)]

    def body(c, carry):
        start = pl.multiple_of(c * tk, tk)
        sts = [_dot_t(k_ref[0, pl.ds(start, tk), hh * LANES:(hh + 1) * LANES], qs[hh]) for hh in range(2)]
        new = []
        for hh in range(2):
            vt = vt_ref[0, 0, hh * VT_ROWS:(hh + 1) * VT_ROWS, pl.ds(start, tk)]
            st = sts[hh]
            if online:
                m, acc = carry[hh]
                m_new = jnp.maximum(m, jnp.max(st, axis=0, keepdims=True))
                acc = jnp.exp2(m - m_new) * acc + _dot(vt, jnp.exp2(st - m_new).astype(BF16))
                new.append((m_new, acc))
            else:
                new.append(carry[hh] + _dot(vt, jnp.exp2(st).astype(BF16)))
        return tuple(new)

    zero = jnp.zeros((VT_ROWS, tq), F32)
    init = (jnp.full((1, tq), -jnp.inf, F32), zero) if online else zero
    res = lax.fori_loop(0, k_ref.shape[1] // tk, body, (init, init))
    accs = [r[1] if online else r for r in res]
    ot = jnp.concatenate([a[:HEAD_DIM] / a[HEAD_DIM:HEAD_DIM + 1] for a in accs], axis=0)
    o_ref[0] = ot.T.astype(o_ref.dtype)


def _latent_attention(qc, kc, vt, score_bound, tq=256, tk=8192):
    B, S, _ = qc.shape
    pair = 2 * LANES

    def call(online, name):
        return pl.pallas_call(
            functools.partial(_mla_kernel, tk=tk, online=online),
            grid=(B, N_HEADS // 2, S // tq),
            in_specs=[pl.BlockSpec((1, tq, pair), lambda b, p, i: (b, i, p)),
                      pl.BlockSpec((1, S, pair), lambda b, p, i: (b, 0, p)),
                      pl.BlockSpec((1, 1, 2 * VT_ROWS, S), lambda b, p, i: (b, p, 0, 0))],
            out_specs=pl.BlockSpec((1, tq, LANES), lambda b, p, i: (b, i, p)),
            out_shape=jax.ShapeDtypeStruct((B, S, HALF_WIDTH), BF16),
            compiler_params=_params(3),
            name=name,
        )

    return lax.cond(score_bound <= MAX_RAW_SCORE,
                    call(False, "latent_attention"), call(True, "latent_attention_online"),
                    qc, kc, vt)


def _mlp_kernel(x_ref, od_ref, om_ref, mod_ref, wout_ref, g_ref, w1_ref, w2_ref, o_ref, *, tf):
    D = x_ref.shape[-1]
    W = od_ref.shape[-1]
    mix = _dot(od_ref[0], wout_ref[0:W, :]) + _dot(om_ref[0], wout_ref[W:2 * W, :])
    x1 = x_ref[0] + mod_ref[0, 2:3, :] * mix
    h = _rms(x1, D) * g_ref[...]
    h = (h * (1.0 + mod_ref[0, 4:5, :]) + mod_ref[0, 3:4, :]).astype(BF16)
    y = jnp.zeros_like(x1)
    for c in range(w1_ref.shape[1] // tf):
        a = jnp.maximum(_dot(h, w1_ref[:, c * tf:(c + 1) * tf]), 0.0)
        y = y + _dot((a * a).astype(BF16), w2_ref[c * tf:(c + 1) * tf, :])
    o_ref[0] = x1 + mod_ref[0, 5:6, :] * y


def _outproj_mlp(x, o_dil, o_mla, mod_l, p, tm=512, tf=512):
    B, S, D = x.shape
    tok = lambda width: pl.BlockSpec((1, tm, width), lambda b, i: (b, i, 0))
    consts = [p["w_out"], p["g_norm_mlp"], p["w_mlp_in"], p["w_mlp_out"]]
    return pl.pallas_call(
        functools.partial(_mlp_kernel, tf=tf),
        grid=(B, S // tm),
        in_specs=[tok(D), tok(HALF_WIDTH), tok(HALF_WIDTH), pl.BlockSpec((1, 6, D), lambda b, i: (b, 0, 0))]
        + [_const_spec(a.shape) for a in consts],
        out_specs=tok(D),
        out_shape=jax.ShapeDtypeStruct((B, S, D), F32),
        compiler_params=_params(2),
        name="outproj_mlp",
    )(x, o_dil, o_mla, mod_l, *consts)


def _layer_params(l, g_norm_mix, w_in, g_q_dil, g_k_dil, g_cq, w_q_up, g_ckv, w_kv_up, g_q_nope,
                  g_q_rope, g_k_nope, g_k_rope, w_out, g_norm_mlp, w_mlp_in, w_mlp_out):
    W = HALF_WIDTH
    q_lora, kv_lora = g_cq.shape[-1], g_ckv.shape[-1]
    pad = LANES - HEAD_DIM - ROPE_DIM
    o_kr = 3 * W + q_lora + kv_lora
    w_kr = jnp.pad(w_in[l][:, o_kr:o_kr + ROPE_DIM], ((0, 0), (HEAD_DIM, pad)))
    wq = w_q_up[l].reshape(q_lora, N_HEADS, HEAD_DIM + ROPE_DIM)
    wq = jnp.pad(wq, ((0, 0), (0, 0), (0, pad))).reshape(q_lora, N_HEADS * LANES)
    wkv = w_kv_up[l].reshape(kv_lora, N_HEADS, 2 * HEAD_DIM)
    wk = jnp.pad(wkv[:, :, :HEAD_DIM], ((0, 0), (0, 0), (0, LANES - HEAD_DIM))).reshape(kv_lora, N_HEADS * LANES)
    wvt = jnp.pad(wkv[:, :, HEAD_DIM:].transpose(1, 2, 0), ((0, 0), (0, VT_ROWS - HEAD_DIM), (0, 0)))
    vt_ones = jnp.zeros((N_HEADS, VT_ROWS, 1), F32).at[:, HEAD_DIM].set(1.0)
    gmax2 = lambda gn, gr: HEAD_DIM * jnp.max(gn[l] ** 2) + ROPE_DIM * jnp.max(gr[l] ** 2)
    score_bound = 1.02 * LOG2E * (HEAD_DIM + ROPE_DIM) ** -0.5 * jnp.sqrt(
        gmax2(g_q_nope, g_q_rope) * gmax2(g_k_nope, g_k_rope))
    row = lambda v: v.astype(F32)[None, :]
    return {
        "g_norm_mix": row(g_norm_mix[l]),
        "w_in": jnp.concatenate([w_in[l][:, :o_kr], w_kr], axis=1).astype(BF16),
        "g_q_dil": row(jnp.tile(g_q_dil[l], 2)),
        "g_k_dil": row(jnp.tile(g_k_dil[l], 2)),
        "g_cq": row(g_cq[l]),
        "w_q_up": wq.astype(BF16),
        "g_ckv": row(g_ckv[l]),
        "w_k_up": wk.astype(BF16),
        "w_vt_up": wvt.reshape(N_HEADS * VT_ROWS, kv_lora).astype(BF16),
        "vt_ones": vt_ones.reshape(N_HEADS * VT_ROWS, 1),
        "score_bound": score_bound,
        "g_q": row(jnp.pad(jnp.concatenate([g_q_nope[l], g_q_rope[l]]), (0, pad))),
        "g_k": row(jnp.pad(g_k_nope[l], (0, LANES - HEAD_DIM))),
        "g_kr": row(jnp.pad(g_k_rope[l], (HEAD_DIM, pad))),
        "w_out": w_out[l].astype(BF16),
        "g_norm_mlp": row(g_norm_mlp[l]),
        "w_mlp_in": w_mlp_in[l].astype(BF16),
        "w_mlp_out": w_mlp_out[l].astype(BF16),
    }


def kernel(x, c, positions, w_mod, b_mod, g_norm_mix, w_in, g_q_dil, g_k_dil, g_cq, w_q_up, g_ckv, w_kv_up, g_q_nope, g_q_rope, g_k_nope, g_k_rope, w_out, g_norm_mlp, w_mlp_in, w_mlp_out):
    B, S, D = x.shape
    depth = w_mod.shape[0]
    qb = 128
    cos_t, sin_t = _rope_tables(positions)
    mod = _modulation(c, w_mod, b_mod).reshape(depth, B, 6, D)
    pos_views = [_strided_positions(positions, dil, qb, window // (2 * dil)) for window, dil in DIL_CONFIGS]
    for l in range(depth):
        p = _layer_params(l, g_norm_mix, w_in, g_q_dil, g_k_dil, g_cq, w_q_up, g_ckv, w_kv_up, g_q_nope,
                          g_q_rope, g_k_nope, g_k_rope, w_out, g_norm_mlp, w_mlp_in, w_mlp_out)
        qd, kd, vd, qc, kc, vt = _inproj(x, mod[l], cos_t, sin_t, p)
        prev = []
        for c in range(len(DIL_CONFIGS) - 1, 0, -1):
            window, dil = DIL_CONFIGS[c]
            prev.append((dil, *_dilated_config(qd[c], kd[c], vd[c], *pos_views[c], window, dil, qb=qb)))
        window, dil = DIL_CONFIGS[0]
        o_dil = _dilated_config(qd[0], kd[0], vd[0], *pos_views[0], window, dil, prev=tuple(prev), qb=qb)
        o_mla = _latent_attention(qc, kc, vt, p["score_bound"])
        x = _outproj_mlp(x, o_dil, o_mla, mod[l], p)
    return x
```

```python
import functools
import math

import jax
import jax.numpy as jnp
from jax import lax
from jax.experimental import pallas as pl
from jax.experimental.pallas import tpu as pltpu

HEAD_DIM = 64
N_HEADS = 8
HALF_WIDTH = N_HEADS * HEAD_DIM
ROPE_DIM = HEAD_DIM // 2
DIL_CONFIGS = ((128, 1), (512, 4), (2048, 16))
RESIDUE_STEP = 4
ROPE_THETA = 10000.0
EPS = 1e-6
NEG_INF = -1e30
LANES = 128
BF16_ROWS = 16
VT_ROWS = HEAD_DIM + BF16_ROWS
LOG2E = math.log2(math.e)
VMEM_LIMIT = 56 * 1024 * 1024
MAX_RAW_SCORE = 60.0

BF16 = jnp.bfloat16
F32 = jnp.float32


def _params(n_axes):
    return pltpu.CompilerParams(
        dimension_semantics=("arbitrary",) * n_axes, vmem_limit_bytes=VMEM_LIMIT)


def _lane_iota(shape):
    return lax.broadcasted_iota(jnp.int32, shape, len(shape) - 1)


def _dot_t(a, b):
    return lax.dot_general(a, b, (((1,), (1,)), ((), ())), preferred_element_type=F32)


def _dot(a, b):
    return jnp.dot(a, b, preferred_element_type=F32)


def _rms(x, width):
    return x * lax.rsqrt(jnp.sum(x * x, axis=-1, keepdims=True) * (1.0 / width) + EPS)


def _swap_rope_halves(y):
    up = pltpu.roll(y, LANES - ROPE_DIM // 2, axis=1)
    down = pltpu.roll(y, ROPE_DIM // 2, axis=1)
    return jnp.where(_lane_iota(y.shape) < HEAD_DIM + ROPE_DIM // 2, up, down)


def _rope_table_kernel(pos_ref, inv_ref, sign_ref, cos_ref, sin_ref):
    ang = pos_ref[0].astype(F32) * inv_ref[...]
    cos_ref[0] = jnp.cos(ang)
    sin_ref[0] = jnp.sin(ang) * sign_ref[...]


def _rope_tables(positions, ts=1024):
    B, S = positions.shape
    half = ROPE_DIM // 2
    inv = ROPE_THETA ** (-jnp.arange(half, dtype=F32) / half)
    zeros = jnp.zeros((HEAD_DIM,), F32)
    inv_row = jnp.concatenate([zeros, inv, inv, jnp.zeros((LANES - HEAD_DIM - ROPE_DIM,), F32)])[None]
    sign_row = jnp.concatenate([zeros, -jnp.ones((half,), F32), jnp.ones((half,), F32),
                                jnp.zeros((LANES - HEAD_DIM - ROPE_DIM,), F32)])[None]
    out = jax.ShapeDtypeStruct((B, S, LANES), F32)
    return pl.pallas_call(
        _rope_table_kernel,
        grid=(B, S // ts),
        in_specs=[pl.BlockSpec((1, ts, 1), lambda b, i: (b, i, 0)),
                  pl.BlockSpec((1, LANES), lambda b, i: (0, 0)),
                  pl.BlockSpec((1, LANES), lambda b, i: (0, 0))],
        out_specs=[pl.BlockSpec((1, ts, LANES), lambda b, i: (b, i, 0))] * 2,
        out_shape=[out, out],
        compiler_params=_params(2),
        name="rope_tables",
    )(positions[..., None], inv_row, sign_row)


def _mod_kernel(ct_ref, w_ref, b_ref, o_ref):
    ct = ct_ref[...]
    act = ct * (1.0 / (1.0 + jnp.exp(-ct)))
    w = w_ref[0]
    for b in range(ct.shape[1]):
        o_ref[0, b:b + 1, :] = jnp.sum(act[:, b:b + 1] * w, axis=0, keepdims=True) + b_ref[0]


def _modulation(c, w_mod, b_mod, tn=1536):
    depth, D, n_out = w_mod.shape
    B = c.shape[0]
    return pl.pallas_call(
        _mod_kernel,
        grid=(depth, n_out // tn),
        in_specs=[pl.BlockSpec((D, B), lambda l, j: (0, 0)),
                  pl.BlockSpec((1, D, tn), lambda l, j: (l, 0, j)),
                  pl.BlockSpec((1, 1, tn), lambda l, j: (l, 0, j))],
        out_specs=pl.BlockSpec((1, B, tn), lambda l, j: (l, 0, j)),
        out_shape=jax.ShapeDtypeStruct((depth, B, n_out), F32),
        compiler_params=_params(2),
        name="modulation",
    )(c.T, w_mod, b_mod[:, None, :])


def _head_pair_norm(t):
    sq = t * t
    low = _lane_iota(t.shape) < HEAD_DIM
    tot = jnp.sum(sq, axis=-1, keepdims=True)
    lo = jnp.sum(jnp.where(low, sq, 0.0), axis=-1, keepdims=True)
    ss = jnp.where(low, lo, tot - lo)
    return t * lax.rsqrt(ss * (1.0 / HEAD_DIM) + EPS)


def _store_residue_layouts(tile, p, out_refs, stage_ref, gather_ref):
    natural_ref, by4_ref, by16_ref = out_refs
    rows = tile.shape[0]
    W = HALF_WIDTH
    step = RESIDUE_STEP
    natural_ref[0, :, p * LANES:(p + 1) * LANES] = tile.astype(BF16)
    stage_ref[...] = tile
    for a in range(step):
        part = stage_ref[pl.ds(a, rows // step, stride=step), :]
        by4_ref[0, :, a * W + p * LANES:a * W + (p + 1) * LANES] = part.astype(BF16)
        gather_ref[a] = part
        for b in range(step):
            sub = gather_ref[a, pl.ds(b, rows // step ** 2, stride=step), :]
            col = (step * b + a) * W + p * LANES
            by16_ref[0, :, col:col + LANES] = sub.astype(BF16)


def _inproj_kernel(x_ref, mod_ref, gmix_ref, win_ref, gqd_ref, gkd_ref, gcq_ref, wq_ref, wqs_ref, qavg_ref,
                   gckv_ref, wk_ref, wvt_ref, vone_ref, gq_ref, gqs_ref, gk_ref, gkr_ref, cos_ref, sin_ref,
                   qd1_ref, qd4_ref, qd16_ref, kd1_ref, kd4_ref, kd16_ref, vd1_ref, vd4_ref, vd16_ref,
                   qc_ref, kc_ref, vt_ref, stage_ref, gather_ref, *, q_scale):
    D = x_ref.shape[-1]
    x = x_ref[0]
    h = _rms(x, D) * gmix_ref[...]
    h = (h * (1.0 + mod_ref[0, 1:2, :]) + mod_ref[0, 0:1, :]).astype(BF16)
    cos = cos_ref[0]
    sin = sin_ref[0]
    W = HALF_WIDTH

    n_tiles = W // LANES
    operands = (((qd1_ref, qd4_ref, qd16_ref), gqd_ref, LOG2E * HEAD_DIM ** -0.5),
                ((kd1_ref, kd4_ref, kd16_ref), gkd_ref, 1.0),
                ((vd1_ref, vd4_ref, vd16_ref), None, None))
    for o, (out_refs, g_ref, scale) in enumerate(operands):
        for p in range(n_tiles):
            if p % 2 == 0:
                t2 = _dot(h, win_ref[:, o * W + p * LANES:o * W + (p + 2) * LANES])
            t = t2[:, (p % 2) * LANES:(p % 2 + 1) * LANES]
            if g_ref is not None:
                t = _head_pair_norm(t) * (g_ref[...] * scale)
            _store_residue_layouts(t, p, out_refs, stage_ref.at[o * n_tiles + p], gather_ref.at[o * n_tiles + p])

    q_lora = wq_ref.shape[0]
    kv_lora = wk_ref.shape[0]
    o_cq = 3 * W
    o_ckv = o_cq + q_lora
    o_kr = o_ckv + kv_lora
    cq = _dot(h, win_ref[:, o_cq:o_ckv])
    cqn = (_rms(cq, q_lora) * gcq_ref[...]).astype(BF16)
    cos2 = jnp.concatenate([cos, cos], axis=1)
    sin2 = jnp.concatenate([sin, sin], axis=1)
    for hp in range(N_HEADS // 2):
        cols = slice(2 * hp * LANES, 2 * (hp + 1) * LANES)
        t = _dot(cqn, wq_ref[:, cols])
        t_swapped = _dot(cqn, wqs_ref[:, cols])
        r = lax.rsqrt(_dot((t * t).astype(BF16), qavg_ref[...]) + EPS)
        y = (t * gq_ref[...]) * cos2 + (t_swapped * gqs_ref[...]) * sin2
        qc_ref[0, :, cols] = (y * (r * q_scale)).astype(BF16)

    ckv_kr = _dot(h, win_ref[:, o_ckv:o_kr + LANES])
    ckvn = (_rms(ckv_kr[:, :kv_lora], kv_lora) * gckv_ref[...]).astype(BF16)
    vt = _dot_t(wvt_ref[...], ckvn) + vone_ref[...]
    pair_rows = 2 * VT_ROWS
    for p in range(N_HEADS // 2):
        vt_ref[0, p] = vt[p * pair_rows:(p + 1) * pair_rows].astype(BF16)
    kr = ckv_kr[:, kv_lora:]
    kr = _rms(kr, ROPE_DIM) * gkr_ref[...]
    kr = kr * cos + _swap_rope_halves(kr) * sin
    for hh in range(N_HEADS):
        if hh % 2 == 0:
            t2 = _dot(ckvn, wk_ref[:, hh * LANES:(hh + 2) * LANES])
        t = t2[:, (hh % 2) * LANES:(hh % 2 + 1) * LANES]
        kc_ref[0, :, hh * LANES:(hh + 1) * LANES] = (_rms(t, HEAD_DIM) * gk_ref[...] + kr).astype(BF16)


def _const_spec(shape):
    return pl.BlockSpec(shape, lambda b, i: (0,) * len(shape), pipeline_mode=pl.Buffered(1))


def _inproj(x, mod_l, cos_t, sin_t, p, tm=512):
    B, S, D = x.shape
    W = HALF_WIDTH
    consts = [p["g_norm_mix"], p["w_in"], p["g_q_dil"], p["g_k_dil"], p["g_cq"], p["w_q_up"],
              p["w_q_up_swapped"], p["q_group_avg"], p["g_ckv"], p["w_k_up"], p["w_vt_up"], p["vt_ones"],
              p["g_q"], p["g_q_swapped"], p["g_k"], p["g_kr"]]
    assert tuple(d for _, d in DIL_CONFIGS) == (1, RESIDUE_STEP, RESIDUE_STEP ** 2)
    tok = lambda width, rows=tm: pl.BlockSpec((1, rows, width), lambda b, i: (b, i, 0))
    dil_shapes = [(S // d, d * W) for _ in range(3) for _, d in DIL_CONFIGS]
    tok_shapes = dil_shapes + [(S, N_HEADS * LANES)] * 2
    vt_shape = (B, N_HEADS // 2, 2 * VT_ROWS, S)
    q_scale = (HEAD_DIM + ROPE_DIM) ** -0.5 * LOG2E
    n_tiles = 3 * W // LANES
    outs = pl.pallas_call(
        functools.partial(_inproj_kernel, q_scale=q_scale),
        grid=(B, S // tm),
        in_specs=[tok(D), pl.BlockSpec((1, 6, D), lambda b, i: (b, 0, 0))]
        + [_const_spec(a.shape) for a in consts] + [tok(LANES), tok(LANES)],
        out_specs=[tok(width, tm * rows // S) for rows, width in tok_shapes]
        + [pl.BlockSpec((1,) + vt_shape[1:3] + (tm,), lambda b, i: (b, 0, 0, i))],
        out_shape=[jax.ShapeDtypeStruct((B,) + s, BF16) for s in tok_shapes]
        + [jax.ShapeDtypeStruct(vt_shape, BF16)],
        scratch_shapes=[pltpu.VMEM((n_tiles, tm, LANES), F32),
                        pltpu.VMEM((n_tiles, RESIDUE_STEP, tm // RESIDUE_STEP, LANES), F32)],
        compiler_params=_params(2),
        name="inproj",
    )(x, mod_l, *consts, cos_t, sin_t)
    n_dil = len(DIL_CONFIGS)
    qd, kd, vd = (outs[o * n_dil:(o + 1) * n_dil] for o in range(3))
    return qd, kd, vd, outs[-3], outs[-2], outs[-1]


def _alibi_slope(h):
    return 2.0 ** (-8.0 * (h + 1) / N_HEADS)


def _halo_rows(prev_ref, main_ref, next_ref, lo, hi):
    rows = main_ref.shape[1]
    parts = []
    if lo < 0:
        parts.append(prev_ref[0])
    parts.append(main_ref[0, max(lo, 0):min(hi, rows), :])
    if hi > rows:
        parts.append(next_ref[0])
    return parts[0] if len(parts) == 1 else jnp.concatenate(parts, axis=0)


def _to_token_order(blk_ref, dil, nat_ref, tmp_ref):
    W = HALF_WIDTH
    step = RESIDUE_STEP
    rows = nat_ref.shape[1]
    for p in range(W // LANES):
        for a in range(step):
            if dil == step:
                part = blk_ref[0, :, a * W + p * LANES:a * W + (p + 1) * LANES]
            else:
                for b in range(step):
                    col = (step * b + a) * W + p * LANES
                    tmp_ref[p, a, pl.ds(b, rows // step ** 2, stride=step), :] = blk_ref[0, :, col:col + LANES]
                part = tmp_ref[p, a]
            nat_ref[p, pl.ds(a, rows // step, stride=step), :] = part


def _dilated_kernel(q_ref, kp_ref, km_ref, kn_ref, vp_ref, vm_ref, vn_ref, pk_ref, *rest,
                    qb, half, length, merge_dils, online):
    step_rows = q_ref.shape[1]
    nkeys = qb + 2 * half
    n = pl.program_id(2)
    merge = bool(merge_dils)
    if merge:
        n_other = 2 * len(merge_dils)
        out_ref, nat_ref, tmp_ref = rest[n_other:]
        for c, dil in enumerate(merge_dils):
            for t in range(2):
                _to_token_order(rest[2 * c + t], dil, nat_ref.at[2 * c + t], tmp_ref.at[2 * c + t])
    row = lax.broadcasted_iota(jnp.int32, (qb, nkeys), 0)
    col = lax.broadcasted_iota(jnp.int32, (qb, nkeys), 1)
    in_band = jnp.abs(col - half - row) <= half
    low = _lane_iota((qb, LANES)) < HEAD_DIM

    for j in range(step_rows // qb):
        rows = slice(j * qb, (j + 1) * qb)
        q = q_ref[0, rows, :]
        k_all = _halo_rows(kp_ref, km_ref, kn_ref, j * qb - half, (j + 1) * qb + half)
        v_all = _halo_rows(vp_ref, vm_ref, vn_ref, j * qb - half, (j + 1) * qb + half)
        pk = pk_ref[0, 0, j]
        pq = jnp.sum(jnp.where(col == row + half, pk, 0), axis=-1, keepdims=True)
        dist = jnp.abs(pq - pk).astype(F32)
        key_u = n * step_rows + j * qb - half + col
        mask = in_band & (key_u >= 0) & (key_u < length)
        mask_bias = jnp.where(mask, 0.0, NEG_INF)

        for p in range(HALF_WIDTH // LANES):
            sl = slice(p * LANES, (p + 1) * LANES)
            qp, kp, vp = q[:, sl], k_all[:, sl], v_all[:, sl]
            outs, lses = [], []
            for hh in range(2):
                qm = jnp.where(low if hh == 0 else ~low, qp, jnp.zeros_like(qp))
                s = _dot_t(qm, kp) + (dist * (-LOG2E * _alibi_slope(2 * p + hh)) + mask_bias)
                if online:
                    m = jnp.max(s, axis=-1, keepdims=True)
                    e = jnp.exp2(s - m)
                else:
                    e = jnp.exp2(s)
                l = jnp.sum(e, axis=-1, keepdims=True)
                outs.append(_dot(e.astype(BF16), vp) / l)
                lses.append(m + jnp.log2(l) if online else jnp.log2(l))
            o = jnp.where(low, outs[0], outs[1])
            lse = jnp.where(low, lses[0], lses[1])
            if merge:
                others = [(nat_ref[2 * c, p, rows, :], nat_ref[2 * c + 1, p, rows, :]) for c in range(len(merge_dils))]
                top = lse
                for _, lc in others:
                    top = jnp.maximum(top, lc)
                w = jnp.exp2(lse - top)
                num, den = o * w, w
                for oc, lc in others:
                    w = jnp.exp2(lc - top)
                    num, den = num + oc * w, den + w
                out_ref[0, rows, sl] = (num / den).astype(out_ref.dtype)
            else:
                rest[0][0, rows, sl] = o
                rest[1][0, rows, sl] = lse


def _strided_positions(positions, dil, qb, half):
    B, S = positions.shape
    L = S // dil
    pos_s = positions.reshape(B, L, dil).transpose(0, 2, 1)
    padded = jnp.pad(pos_s, ((0, 0), (0, 0), (half, qb)))
    nb = L // qb
    parts = [padded[:, :, off:off + L].reshape(B, dil, nb, qb)[..., :w]
             for off, w in ((0, qb), (qb, 2 * half))]
    return jnp.concatenate(parts, axis=-1)[:, :, :, None, :]


def _dilated_config(qd, kd, vd, pos_k, window, dil, online, prev=(), qb=128, n_sub=4):
    B, L, width = qd.shape
    W = width // dil
    half = window // (2 * dil)
    step_rows = n_sub * qb
    assert L % step_rows == 0 and step_rows % half == 0 and half % BF16_ROWS == 0
    halo_per_step = step_rows // half
    last_halo = L // half - 1
    main = pl.BlockSpec((1, step_rows, W), lambda b, r, n: (b, n, r))
    before = pl.BlockSpec((1, half, W), lambda b, r, n: (b, jnp.maximum(n * halo_per_step - 1, 0), r))
    after = pl.BlockSpec((1, half, W), lambda b, r, n: (b, jnp.minimum((n + 1) * halo_per_step, last_halo), r))

    in_specs = [main, before, main, after, before, main, after,
                pl.BlockSpec((1, 1, n_sub, 1, qb + 2 * half), lambda b, r, n: (b, r, n, 0, 0))]
    args = [qd] + [kd] * 3 + [vd] * 3 + [pos_k]
    scratch = []
    if prev:
        assert dil == 1
        for d, o, lse in prev:
            in_specs += [pl.BlockSpec((1, step_rows // d, d * W), lambda b, r, n: (b, n, 0))] * 2
            args += [o, lse]
        n_tiles = W // LANES
        scratch = [pltpu.VMEM((2 * len(prev), n_tiles, step_rows, LANES), F32),
                   pltpu.VMEM((2 * len(prev), n_tiles, RESIDUE_STEP, step_rows // RESIDUE_STEP, LANES), F32)]
        out_specs = main
        out_shape = jax.ShapeDtypeStruct((B, L, W), BF16)
    else:
        out_specs = [main, main]
        out_shape = [jax.ShapeDtypeStruct((B, L, dil * W), F32)] * 2
    return pl.pallas_call(
        functools.partial(_dilated_kernel, qb=qb, half=half, length=L, merge_dils=tuple(d for d, _, _ in prev),
                          online=online),
        grid=(B, dil, L // step_rows),
        in_specs=in_specs, out_specs=out_specs, out_shape=out_shape,
        scratch_shapes=scratch,
        compiler_params=_params(3),
        name=f"dilated_d{dil}" + ("_online" if online else ""),
    )(*args)


def _dilated_attention(qd, kd, vd, pos_views, score_bound, qb):
    def stage(online):
        def run(qd, kd, vd, pos_views):
            prev = []
            for c in range(len(DIL_CONFIGS) - 1, 0, -1):
                window, dil = DIL_CONFIGS[c]
                prev.append((dil, *_dilated_config(qd[c], kd[c], vd[c], pos_views[c], window, dil, online, qb=qb)))
            window, dil = DIL_CONFIGS[0]
            return _dilated_config(qd[0], kd[0], vd[0], pos_views[0], window, dil, online, prev=tuple(prev), qb=qb)
        return run

    return lax.cond(score_bound <= MAX_RAW_SCORE, stage(False), stage(True), qd, kd, vd, pos_views)


def _mla_kernel(q_ref, k_ref, vt_ref, o_ref, *, tk, online):
    tq = q_ref.shape[1]
    qs = [q_ref[0, :, hh * LANES:(hh + 1) * LANES] for hh in range(2)]

    def body(c, carry):
        start = pl.multiple_of(c * tk, tk)
        sts = [_dot_t(k_ref[0, pl.ds(start, tk), hh * LANES:(hh + 1) * LANES], qs[hh]) for hh in range(2)]
        new = []
        for hh in range(2):
            vt = vt_ref[0, 0, hh * VT_ROWS:(hh + 1) * VT_ROWS, pl.ds(start, tk)]
            st = sts[hh]
            if online:
                m, acc = carry[hh]
                m_new = jnp.maximum(m, jnp.max(st, axis=0, keepdims=True))
                acc = jnp.exp2(m - m_new) * acc + _dot(vt, jnp.exp2(st - m_new).astype(BF16))
                new.append((m_new, acc))
            else:
                new.append(carry[hh] + _dot(vt, jnp.exp2(st).astype(BF16)))
        return tuple(new)

    zero = jnp.zeros((VT_ROWS, tq), F32)
    init = (jnp.full((1, tq), -jnp.inf, F32), zero) if online else zero
    res = lax.fori_loop(0, k_ref.shape[1] // tk, body, (init, init))
    accs = [r[1] if online else r for r in res]
    ot = jnp.concatenate([a[:HEAD_DIM] / a[HEAD_DIM:HEAD_DIM + 1] for a in accs], axis=0)
    o_ref[0] = ot.T.astype(o_ref.dtype)


def _latent_attention(qc, kc, vt, score_bound, tq=512, tk=8192):
    B, S, _ = qc.shape
    pair = 2 * LANES

    def call(online, name):
        return pl.pallas_call(
            functools.partial(_mla_kernel, tk=tk, online=online),
            grid=(B, N_HEADS // 2, S // tq),
            in_specs=[pl.BlockSpec((1, tq, pair), lambda b, p, i: (b, i, p)),
                      pl.BlockSpec((1, S, pair), lambda b, p, i: (b, 0, p)),
                      pl.BlockSpec((1, 1, 2 * VT_ROWS, S), lambda b, p, i: (b, p, 0, 0))],
            out_specs=pl.BlockSpec((1, tq, LANES), lambda b, p, i: (b, i, p)),
            out_shape=jax.ShapeDtypeStruct((B, S, HALF_WIDTH), BF16),
            compiler_params=_params(3),
            name=name,
        )

    return lax.cond(score_bound <= MAX_RAW_SCORE,
                    call(False, "latent_attention"), call(True, "latent_attention_online"),
                    qc, kc, vt)


def _mlp_kernel(x_ref, od_ref, om_ref, mod_ref, wout_ref, g_ref, w1_ref, w2_ref, o_ref, *, tf):
    D = x_ref.shape[-1]
    W = od_ref.shape[-1]
    mix = _dot(od_ref[0], wout_ref[0:W, :]) + _dot(om_ref[0], wout_ref[W:2 * W, :])
    x1 = x_ref[0] + mod_ref[0, 2:3, :] * mix
    h = _rms(x1, D) * g_ref[...]
    h = (h * (1.0 + mod_ref[0, 4:5, :]) + mod_ref[0, 3:4, :]).astype(BF16)
    y = jnp.zeros_like(x1)
    for c in range(w1_ref.shape[1] // tf):
        a = jnp.maximum(_dot(h, w1_ref[:, c * tf:(c + 1) * tf]), 0.0)
        y = y + _dot((a * a).astype(BF16), w2_ref[c * tf:(c + 1) * tf, :])
    o_ref[0] = x1 + mod_ref[0, 5:6, :] * y


def _outproj_mlp(x, o_dil, o_mla, mod_l, p, tm=512, tf=512):
    B, S, D = x.shape
    tok = lambda width: pl.BlockSpec((1, tm, width), lambda b, i: (b, i, 0))
    consts = [p["w_out"], p["g_norm_mlp"], p["w_mlp_in"], p["w_mlp_out"]]
    return pl.pallas_call(
        functools.partial(_mlp_kernel, tf=tf),
        grid=(B, S // tm),
        in_specs=[tok(D), tok(HALF_WIDTH), tok(HALF_WIDTH), pl.BlockSpec((1, 6, D), lambda b, i: (b, 0, 0))]
        + [_const_spec(a.shape) for a in consts],
        out_specs=tok(D),
        out_shape=jax.ShapeDtypeStruct((B, S, D), F32),
        compiler_params=_params(2),
        name="outproj_mlp",
    )(x, o_dil, o_mla, mod_l, *consts)


def _layer_params(l, g_norm_mix, w_in, g_q_dil, g_k_dil, g_cq, w_q_up, g_ckv, w_kv_up, g_q_nope,
                  g_q_rope, g_k_nope, g_k_rope, w_out, g_norm_mlp, w_mlp_in, w_mlp_out):
    W = HALF_WIDTH
    q_lora, kv_lora = g_cq.shape[-1], g_ckv.shape[-1]
    pad = LANES - HEAD_DIM - ROPE_DIM
    o_kr = 3 * W + q_lora + kv_lora
    w_kr = jnp.pad(w_in[l][:, o_kr:o_kr + ROPE_DIM], ((0, 0), (HEAD_DIM, pad)))
    wq = w_q_up[l].reshape(q_lora, N_HEADS, HEAD_DIM + ROPE_DIM)
    swap_halves = lambda a: jnp.concatenate([a[..., ROPE_DIM // 2:], a[..., :ROPE_DIM // 2]], axis=-1)
    wq_swapped = jnp.pad(swap_halves(wq[:, :, HEAD_DIM:]), ((0, 0), (0, 0), (HEAD_DIM, pad)))
    wq_swapped = wq_swapped.reshape(q_lora, N_HEADS * LANES)
    wq = jnp.pad(wq, ((0, 0), (0, 0), (0, pad))).reshape(q_lora, N_HEADS * LANES)
    lane = jnp.arange(LANES)
    nope_blk = (lane[:, None] < HEAD_DIM) & (lane[None, :] < HEAD_DIM)
    rope_rows = (lane >= HEAD_DIM) & (lane < HEAD_DIM + ROPE_DIM)
    rope_blk = rope_rows[:, None] & (lane[None, :] >= HEAD_DIM)
    tile_avg = nope_blk / HEAD_DIM + rope_blk / ROPE_DIM
    q_group_avg = jnp.kron(jnp.eye(2), tile_avg).astype(BF16)
    wkv = w_kv_up[l].reshape(kv_lora, N_HEADS, 2 * HEAD_DIM)
    wk = jnp.pad(wkv[:, :, :HEAD_DIM], ((0, 0), (0, 0), (0, LANES - HEAD_DIM))).reshape(kv_lora, N_HEADS * LANES)
    wvt = jnp.pad(wkv[:, :, HEAD_DIM:].transpose(1, 2, 0), ((0, 0), (0, VT_ROWS - HEAD_DIM), (0, 0)))
    vt_ones = jnp.zeros((N_HEADS, VT_ROWS, 1), F32).at[:, HEAD_DIM].set(1.0)
    gmax2 = lambda gn, gr: HEAD_DIM * jnp.max(gn[l] ** 2) + ROPE_DIM * jnp.max(gr[l] ** 2)
    score_bound = 1.02 * LOG2E * (HEAD_DIM + ROPE_DIM) ** -0.5 * jnp.sqrt(
        gmax2(g_q_nope, g_q_rope) * gmax2(g_k_nope, g_k_rope))
    row = lambda v: v.astype(F32)[None, :]
    return {
        "g_norm_mix": row(g_norm_mix[l]),
        "w_in": jnp.concatenate([w_in[l][:, :o_kr], w_kr], axis=1).astype(BF16),
        "g_q_dil": row(jnp.tile(g_q_dil[l], 2)),
        "g_k_dil": row(jnp.tile(g_k_dil[l], 2)),
        "g_cq": row(g_cq[l]),
        "w_q_up": wq.astype(BF16),
        "w_q_up_swapped": wq_swapped.astype(BF16),
        "q_group_avg": q_group_avg,
        "g_ckv": row(g_ckv[l]),
        "w_k_up": wk.astype(BF16),
        "w_vt_up": wvt.reshape(N_HEADS * VT_ROWS, kv_lora).astype(BF16),
        "vt_ones": vt_ones.reshape(N_HEADS * VT_ROWS, 1),
        "score_bound": score_bound,
        "dil_score_bound": 1.02 * LOG2E * HEAD_DIM ** 0.5 * jnp.max(jnp.abs(g_q_dil[l])) * jnp.max(jnp.abs(g_k_dil[l])),
        "g_q": row(jnp.tile(jnp.pad(jnp.concatenate([g_q_nope[l], g_q_rope[l]]), (0, pad)), 2)),
        "g_q_swapped": row(jnp.tile(jnp.pad(swap_halves(g_q_rope[l]), (HEAD_DIM, pad)), 2)),
        "g_k": row(jnp.pad(g_k_nope[l], (0, LANES - HEAD_DIM))),
        "g_kr": row(jnp.pad(g_k_rope[l], (HEAD_DIM, pad))),
        "w_out": w_out[l].astype(BF16),
        "g_norm_mlp": row(g_norm_mlp[l]),
        "w_mlp_in": w_mlp_in[l].astype(BF16),
        "w_mlp_out": w_mlp_out[l].astype(BF16),
    }


def kernel(x, c, positions, w_mod, b_mod, g_norm_mix, w_in, g_q_dil, g_k_dil, g_cq, w_q_up, g_ckv, w_kv_up, g_q_nope, g_q_rope, g_k_nope, g_k_rope, w_out, g_norm_mlp, w_mlp_in, w_mlp_out):
    B, S, D = x.shape
    depth = w_mod.shape[0]
    qb = 128
    cos_t, sin_t = _rope_tables(positions)
    mod = _modulation(c, w_mod, b_mod).reshape(depth, B, 6, D)
    pos_views = [_strided_positions(positions, dil, qb, window // (2 * dil)) for window, dil in DIL_CONFIGS]
    for l in range(depth):
        p = _layer_params(l, g_norm_mix, w_in, g_q_dil, g_k_dil, g_cq, w_q_up, g_ckv, w_kv_up, g_q_nope,
                          g_q_rope, g_k_nope, g_k_rope, w_out, g_norm_mlp, w_mlp_in, w_mlp_out)
        qd, kd, vd, qc, kc, vt = _inproj(x, mod[l], cos_t, sin_t, p)
        o_dil = _dilated_attention(qd, kd, vd, pos_views, p["dil_score_bound"], qb)
        o_mla = _latent_attention(qc, kc, vt, p["score_bound"])
        x = _outproj_mlp(x, o_dil, o_mla, mod[l], p)
    return x
```

```python
import functools
import math

import jax
import jax.numpy as jnp
from jax import lax
from jax.experimental import pallas as pl
from jax.experimental.pallas import tpu as pltpu

HEAD_DIM = 64
N_HEADS = 8
HALF_WIDTH = N_HEADS * HEAD_DIM
ROPE_DIM = HEAD_DIM // 2
DIL_CONFIGS = ((128, 1), (512, 4), (2048, 16))
RESIDUE_STEP = 4
ROPE_THETA = 10000.0
EPS = 1e-6
NEG_INF = -1e30
LANES = 128
BF16_ROWS = 16
VT_ROWS = HEAD_DIM + BF16_ROWS
LOG2E = math.log2(math.e)
VMEM_LIMIT = 56 * 1024 * 1024
MAX_RAW_SCORE = 60.0

BF16 = jnp.bfloat16
F32 = jnp.float32


def _params(n_axes):
    return pltpu.CompilerParams(
        dimension_semantics=("arbitrary",) * n_axes, vmem_limit_bytes=VMEM_LIMIT)


def _lane_iota(shape):
    return lax.broadcasted_iota(jnp.int32, shape, len(shape) - 1)


def _dot_t(a, b):
    return lax.dot_general(a, b, (((1,), (1,)), ((), ())), preferred_element_type=F32)


def _dot(a, b):
    return jnp.dot(a, b, preferred_element_type=F32)


def _rms(x, width):
    return x * lax.rsqrt(jnp.sum(x * x, axis=-1, keepdims=True) * (1.0 / width) + EPS)


def _swap_rope_halves(y):
    up = pltpu.roll(y, LANES - ROPE_DIM // 2, axis=1)
    down = pltpu.roll(y, ROPE_DIM // 2, axis=1)
    return jnp.where(_lane_iota(y.shape) < HEAD_DIM + ROPE_DIM // 2, up, down)


def _rope_table_kernel(pos_ref, inv_ref, sign_ref, cos_ref, sin_ref):
    ang = pos_ref[0].astype(F32) * inv_ref[...]
    cos_ref[0] = jnp.cos(ang)
    sin_ref[0] = jnp.sin(ang) * sign_ref[...]


def _rope_tables(positions, ts=1024):
    B, S = positions.shape
    half = ROPE_DIM // 2
    inv = ROPE_THETA ** (-jnp.arange(half, dtype=F32) / half)
    zeros = jnp.zeros((HEAD_DIM,), F32)
    inv_row = jnp.concatenate([zeros, inv, inv, jnp.zeros((LANES - HEAD_DIM - ROPE_DIM,), F32)])[None]
    sign_row = jnp.concatenate([zeros, -jnp.ones((half,), F32), jnp.ones((half,), F32),
                                jnp.zeros((LANES - HEAD_DIM - ROPE_DIM,), F32)])[None]
    out = jax.ShapeDtypeStruct((B, S, LANES), F32)
    return pl.pallas_call(
        _rope_table_kernel,
        grid=(B, S // ts),
        in_specs=[pl.BlockSpec((1, ts, 1), lambda b, i: (b, i, 0)),
                  pl.BlockSpec((1, LANES), lambda b, i: (0, 0)),
                  pl.BlockSpec((1, LANES), lambda b, i: (0, 0))],
        out_specs=[pl.BlockSpec((1, ts, LANES), lambda b, i: (b, i, 0))] * 2,
        out_shape=[out, out],
        compiler_params=_params(2),
        name="rope_tables",
    )(positions[..., None], inv_row, sign_row)


def _mod_kernel(ct_ref, w_ref, b_ref, o_ref):
    ct = ct_ref[...]
    act = ct * (1.0 / (1.0 + jnp.exp(-ct)))
    w = w_ref[0]
    for b in range(ct.shape[1]):
        o_ref[0, b:b + 1, :] = jnp.sum(act[:, b:b + 1] * w, axis=0, keepdims=True) + b_ref[0]


def _modulation(c, w_mod, b_mod, tn=1536):
    depth, D, n_out = w_mod.shape
    B = c.shape[0]
    return pl.pallas_call(
        _mod_kernel,
        grid=(depth, n_out // tn),
        in_specs=[pl.BlockSpec((D, B), lambda l, j: (0, 0)),
                  pl.BlockSpec((1, D, tn), lambda l, j: (l, 0, j)),
                  pl.BlockSpec((1, 1, tn), lambda l, j: (l, 0, j))],
        out_specs=pl.BlockSpec((1, B, tn), lambda l, j: (l, 0, j)),
        out_shape=jax.ShapeDtypeStruct((depth, B, n_out), F32),
        compiler_params=_params(2),
        name="modulation",
    )(c.T, w_mod, b_mod[:, None, :])


def _head_pair_norm(t):
    sq = t * t
    low = _lane_iota(t.shape) < HEAD_DIM
    tot = jnp.sum(sq, axis=-1, keepdims=True)
    lo = jnp.sum(jnp.where(low, sq, 0.0), axis=-1, keepdims=True)
    ss = jnp.where(low, lo, tot - lo)
    return t * lax.rsqrt(ss * (1.0 / HEAD_DIM) + EPS)


def _store_residue_layouts(tile, p, out_refs, stage_ref, gather_ref):
    natural_ref, by4_ref, by16_ref = out_refs
    rows = tile.shape[0]
    W = HALF_WIDTH
    step = RESIDUE_STEP
    natural_ref[0, :, p * LANES:(p + 1) * LANES] = tile.astype(BF16)
    stage_ref[...] = tile
    for a in range(step):
        part = stage_ref[pl.ds(a, rows // step, stride=step), :]
        by4_ref[0, :, a * W + p * LANES:a * W + (p + 1) * LANES] = part.astype(BF16)
        gather_ref[a] = part
        for b in range(step):
            sub = gather_ref[a, pl.ds(b, rows // step ** 2, stride=step), :]
            col = (step * b + a) * W + p * LANES
            by16_ref[0, :, col:col + LANES] = sub.astype(BF16)


def _inproj_kernel(x_ref, mod_ref, gmix_ref, win_ref, gqd_ref, gkd_ref, gcq_ref, wq_ref, wqs_ref, qavg_ref,
                   gckv_ref, wk_ref, wvt_ref, vone_ref, gq_ref, gqs_ref, gk_ref, gkr_ref, cos_ref, sin_ref,
                   qd1_ref, qd4_ref, qd16_ref, kd1_ref, kd4_ref, kd16_ref, vd1_ref, vd4_ref, vd16_ref,
                   qc_ref, kc_ref, vt_ref, stage_ref, gather_ref, *, q_scale):
    D = x_ref.shape[-1]
    x = x_ref[0]
    h = _rms(x, D) * gmix_ref[...]
    h = (h * (1.0 + mod_ref[0, 1:2, :]) + mod_ref[0, 0:1, :]).astype(BF16)
    cos = cos_ref[0]
    sin = sin_ref[0]
    W = HALF_WIDTH

    q_lora = wq_ref.shape[0]
    kv_lora = wk_ref.shape[0]
    o_cq = 3 * W
    o_ckv = o_cq + q_lora
    o_kr = o_ckv + kv_lora
    cq = _dot(h, win_ref[:, o_cq:o_ckv])
    cqn = (_rms(cq, q_lora) * gcq_ref[...]).astype(BF16)
    cos2 = jnp.concatenate([cos, cos], axis=1)
    sin2 = jnp.concatenate([sin, sin], axis=1)
    for hp in range(N_HEADS // 2):
        cols = slice(2 * hp * LANES, 2 * (hp + 1) * LANES)
        t = _dot(cqn, wq_ref[:, cols])
        t_swapped = _dot(cqn, wqs_ref[:, cols])
        r = lax.rsqrt(_dot((t * t).astype(BF16), qavg_ref[...]) + EPS)
        y = (t * gq_ref[...]) * cos2 + (t_swapped * gqs_ref[...]) * sin2
        qc_ref[0, :, cols] = (y * (r * q_scale)).astype(BF16)

    ckv_kr = _dot(h, win_ref[:, o_ckv:o_kr + LANES])
    ckvn = (_rms(ckv_kr[:, :kv_lora], kv_lora) * gckv_ref[...]).astype(BF16)
    vt = _dot_t(wvt_ref[...], ckvn) + vone_ref[...]
    pair_rows = 2 * VT_ROWS
    for p in range(N_HEADS // 2):
        vt_ref[0, p] = vt[p * pair_rows:(p + 1) * pair_rows].astype(BF16)
    kr = ckv_kr[:, kv_lora:]
    kr = _rms(kr, ROPE_DIM) * gkr_ref[...]
    kr = kr * cos + _swap_rope_halves(kr) * sin
    for hh in range(N_HEADS):
        if hh % 2 == 0:
            t2 = _dot(ckvn, wk_ref[:, hh * LANES:(hh + 2) * LANES])
        t = t2[:, (hh % 2) * LANES:(hh % 2 + 1) * LANES]
        kc_ref[0, :, hh * LANES:(hh + 1) * LANES] = (_rms(t, HEAD_DIM) * gk_ref[...] + kr).astype(BF16)

    n_tiles = W // LANES
    operands = (((qd1_ref, qd4_ref, qd16_ref), gqd_ref, LOG2E * HEAD_DIM ** -0.5),
                ((kd1_ref, kd4_ref, kd16_ref), gkd_ref, 1.0),
                ((vd1_ref, vd4_ref, vd16_ref), None, None))
    for o, (out_refs, g_ref, scale) in enumerate(operands):
        for p in range(n_tiles):
            if p % 2 == 0:
                t2 = _dot(h, win_ref[:, o * W + p * LANES:o * W + (p + 2) * LANES])
            t = t2[:, (p % 2) * LANES:(p % 2 + 1) * LANES]
            if g_ref is not None:
                t = _head_pair_norm(t) * (g_ref[...] * scale)
            _store_residue_layouts(t, p, out_refs, stage_ref.at[o * n_tiles + p], gather_ref.at[o * n_tiles + p])


def _const_spec(shape):
    return pl.BlockSpec(shape, lambda b, i: (0,) * len(shape), pipeline_mode=pl.Buffered(1))


def _inproj(x, mod_l, cos_t, sin_t, p, tm=512):
    B, S, D = x.shape
    W = HALF_WIDTH
    consts = [p["g_norm_mix"], p["w_in"], p["g_q_dil"], p["g_k_dil"], p["g_cq"], p["w_q_up"],
              p["w_q_up_swapped"], p["q_group_avg"], p["g_ckv"], p["w_k_up"], p["w_vt_up"], p["vt_ones"],
              p["g_q"], p["g_q_swapped"], p["g_k"], p["g_kr"]]
    assert tuple(d for _, d in DIL_CONFIGS) == (1, RESIDUE_STEP, RESIDUE_STEP ** 2)
    tok = lambda width, rows=tm: pl.BlockSpec((1, rows, width), lambda b, i: (b, i, 0))
    dil_shapes = [(S // d, d * W) for _ in range(3) for _, d in DIL_CONFIGS]
    tok_shapes = dil_shapes + [(S, N_HEADS * LANES)] * 2
    vt_shape = (B, N_HEADS // 2, 2 * VT_ROWS, S)
    q_scale = (HEAD_DIM + ROPE_DIM) ** -0.5 * LOG2E
    n_tiles = 3 * W // LANES
    outs = pl.pallas_call(
        functools.partial(_inproj_kernel, q_scale=q_scale),
        grid=(B, S // tm),
        in_specs=[tok(D), pl.BlockSpec((1, 6, D), lambda b, i: (b, 0, 0))]
        + [_const_spec(a.shape) for a in consts] + [tok(LANES), tok(LANES)],
        out_specs=[tok(width, tm * rows // S) for rows, width in tok_shapes]
        + [pl.BlockSpec((1,) + vt_shape[1:3] + (tm,), lambda b, i: (b, 0, 0, i))],
        out_shape=[jax.ShapeDtypeStruct((B,) + s, BF16) for s in tok_shapes]
        + [jax.ShapeDtypeStruct(vt_shape, BF16)],
        scratch_shapes=[pltpu.VMEM((n_tiles, tm, LANES), F32),
                        pltpu.VMEM((n_tiles, RESIDUE_STEP, tm // RESIDUE_STEP, LANES), F32)],
        compiler_params=_params(2),
        name="inproj",
    )(x, mod_l, *consts, cos_t, sin_t)
    n_dil = len(DIL_CONFIGS)
    qd, kd, vd = (outs[o * n_dil:(o + 1) * n_dil] for o in range(3))
    return qd, kd, vd, outs[-3], outs[-2], outs[-1]


def _alibi_slope(h):
    return 2.0 ** (-8.0 * (h + 1) / N_HEADS)


def _halo_rows(prev_ref, main_ref, next_ref, lo, hi):
    rows = main_ref.shape[1]
    parts = []
    if lo < 0:
        parts.append(prev_ref[0])
    parts.append(main_ref[0, max(lo, 0):min(hi, rows), :])
    if hi > rows:
        parts.append(next_ref[0])
    return parts[0] if len(parts) == 1 else jnp.concatenate(parts, axis=0)


def _to_token_order(blk_ref, dil, nat_ref, tmp_ref):
    W = HALF_WIDTH
    step = RESIDUE_STEP
    rows = nat_ref.shape[1]
    for p in range(W // LANES):
        for a in range(step):
            if dil == step:
                part = blk_ref[0, :, a * W + p * LANES:a * W + (p + 1) * LANES]
            else:
                for b in range(step):
                    col = (step * b + a) * W + p * LANES
                    tmp_ref[p, a, pl.ds(b, rows // step ** 2, stride=step), :] = blk_ref[0, :, col:col + LANES]
                part = tmp_ref[p, a]
            nat_ref[p, pl.ds(a, rows // step, stride=step), :] = part


def _dilated_kernel(q_ref, kp_ref, km_ref, kn_ref, vp_ref, vm_ref, vn_ref, pk_ref, *rest,
                    qb, half, length, merge_dils, online):
    step_rows = q_ref.shape[1]
    nkeys = qb + 2 * half
    n = pl.program_id(2)
    merge = bool(merge_dils)
    if merge:
        n_other = 2 * len(merge_dils)
        out_ref, nat_ref, tmp_ref = rest[n_other:]
        for c, dil in enumerate(merge_dils):
            for t in range(2):
                _to_token_order(rest[2 * c + t], dil, nat_ref.at[2 * c + t], tmp_ref.at[2 * c + t])
    row = lax.broadcasted_iota(jnp.int32, (qb, nkeys), 0)
    col = lax.broadcasted_iota(jnp.int32, (qb, nkeys), 1)
    in_band = jnp.abs(col - half - row) <= half
    low = _lane_iota((qb, LANES)) < HEAD_DIM

    for j in range(step_rows // qb):
        rows = slice(j * qb, (j + 1) * qb)
        q = q_ref[0, rows, :]
        k_all = _halo_rows(kp_ref, km_ref, kn_ref, j * qb - half, (j + 1) * qb + half)
        v_all = _halo_rows(vp_ref, vm_ref, vn_ref, j * qb - half, (j + 1) * qb + half)
        pk = pk_ref[0, 0, j]
        pq = jnp.sum(jnp.where(col == row + half, pk, 0), axis=-1, keepdims=True)
        dist = jnp.abs(pq - pk).astype(F32)
        key_u = n * step_rows + j * qb - half + col
        mask = in_band & (key_u >= 0) & (key_u < length)
        mask_bias = jnp.where(mask, 0.0, NEG_INF)

        for p in range(HALF_WIDTH // LANES):
            sl = slice(p * LANES, (p + 1) * LANES)
            qp, kp, vp = q[:, sl], k_all[:, sl], v_all[:, sl]
            outs, stats = [], []
            for hh in range(2):
                qm = jnp.where(low if hh == 0 else ~low, qp, jnp.zeros_like(qp))
                s = _dot_t(qm, kp) + (dist * (-LOG2E * _alibi_slope(2 * p + hh)) + mask_bias)
                if online:
                    m = jnp.max(s, axis=-1, keepdims=True)
                    e = jnp.exp2(s - m)
                    l = jnp.sum(e, axis=-1, keepdims=True)
                    outs.append(_dot(e.astype(BF16), vp) / l)
                    stats.append(m + jnp.log2(l))
                else:
                    e = jnp.exp2(s)
                    outs.append(_dot(e.astype(BF16), vp))
                    stats.append(jnp.sum(e, axis=-1, keepdims=True))
            o = jnp.where(low, outs[0], outs[1])
            stat = jnp.where(low, stats[0], stats[1])
            if not merge:
                rest[0][0, rows, sl] = o
                rest[1][0, rows, sl] = stat
                continue
            others = [(nat_ref[2 * c, p, rows, :], nat_ref[2 * c + 1, p, rows, :]) for c in range(len(merge_dils))]
            if online:
                top = stat
                for _, lc in others:
                    top = jnp.maximum(top, lc)
                w = jnp.exp2(stat - top)
                num, den = o * w, w
                for oc, lc in others:
                    w = jnp.exp2(lc - top)
                    num, den = num + oc * w, den + w
            else:
                num, den = o, stat
                for oc, lc in others:
                    num, den = num + oc, den + lc
            out_ref[0, rows, sl] = (num / den).astype(out_ref.dtype)


def _strided_positions(positions, dil, qb, half):
    B, S = positions.shape
    L = S // dil
    pos_s = positions.reshape(B, L, dil).transpose(0, 2, 1)
    padded = jnp.pad(pos_s, ((0, 0), (0, 0), (half, qb)))
    nb = L // qb
    parts = [padded[:, :, off:off + L].reshape(B, dil, nb, qb)[..., :w]
             for off, w in ((0, qb), (qb, 2 * half))]
    return jnp.concatenate(parts, axis=-1)[:, :, :, None, :]


def _dilated_config(qd, kd, vd, pos_k, window, dil, online, prev=(), qb=128, n_sub=4):
    B, L, width = qd.shape
    W = width // dil
    half = window // (2 * dil)
    step_rows = n_sub * qb
    assert L % step_rows == 0 and step_rows % half == 0 and half % BF16_ROWS == 0
    halo_per_step = step_rows // half
    last_halo = L // half - 1
    main = pl.BlockSpec((1, step_rows, W), lambda b, r, n: (b, n, r))
    before = pl.BlockSpec((1, half, W), lambda b, r, n: (b, jnp.maximum(n * halo_per_step - 1, 0), r))
    after = pl.BlockSpec((1, half, W), lambda b, r, n: (b, jnp.minimum((n + 1) * halo_per_step, last_halo), r))

    in_specs = [main, before, main, after, before, main, after,
                pl.BlockSpec((1, 1, n_sub, 1, qb + 2 * half), lambda b, r, n: (b, r, n, 0, 0))]
    args = [qd] + [kd] * 3 + [vd] * 3 + [pos_k]
    scratch = []
    if prev:
        assert dil == 1
        for d, o, lse in prev:
            in_specs += [pl.BlockSpec((1, step_rows // d, d * W), lambda b, r, n: (b, n, 0))] * 2
            args += [o, lse]
        n_tiles = W // LANES
        scratch = [pltpu.VMEM((2 * len(prev), n_tiles, step_rows, LANES), F32),
                   pltpu.VMEM((2 * len(prev), n_tiles, RESIDUE_STEP, step_rows // RESIDUE_STEP, LANES), F32)]
        out_specs = main
        out_shape = jax.ShapeDtypeStruct((B, L, W), BF16)
    else:
        out_specs = [main, main]
        out_shape = [jax.ShapeDtypeStruct((B, L, dil * W), F32)] * 2
    return pl.pallas_call(
        functools.partial(_dilated_kernel, qb=qb, half=half, length=L, merge_dils=tuple(d for d, _, _ in prev),
                          online=online),
        grid=(B, dil, L // step_rows),
        in_specs=in_specs, out_specs=out_specs, out_shape=out_shape,
        scratch_shapes=scratch,
        compiler_params=_params(3),
        name=f"dilated_d{dil}" + ("_online" if online else ""),
    )(*args)


def _dilated_attention(qd, kd, vd, pos_views, score_bound, qb):
    def stage(online):
        def run(qd, kd, vd, pos_views):
            prev = []
            for c in range(len(DIL_CONFIGS) - 1, 0, -1):
                window, dil = DIL_CONFIGS[c]
                prev.append((dil, *_dilated_config(qd[c], kd[c], vd[c], pos_views[c], window, dil, online, qb=qb)))
            window, dil = DIL_CONFIGS[0]
            return _dilated_config(qd[0], kd[0], vd[0], pos_views[0], window, dil, online, prev=tuple(prev), qb=qb)
        return run

    return lax.cond(score_bound <= MAX_RAW_SCORE, stage(False), stage(True), qd, kd, vd, pos_views)


def _mla_kernel(q_ref, k_ref, vt_ref, o_ref, *, tq, stabilise):
    def query_tile(i, carry):
        rows = pl.ds(pl.multiple_of(i * tq, tq), tq)
        sts = [_dot_t(k_ref[0, :, hh * LANES:(hh + 1) * LANES], q_ref[0, rows, hh * LANES:(hh + 1) * LANES])
               for hh in range(2)]
        accs = []
        for hh in range(2):
            st = sts[hh]
            if stabilise:
                st = st - jnp.max(st, axis=0, keepdims=True)
            accs.append(_dot(vt_ref[0, 0, hh * VT_ROWS:(hh + 1) * VT_ROWS, :], jnp.exp2(st).astype(BF16)))
        ot = jnp.concatenate([a[:HEAD_DIM] / a[HEAD_DIM:HEAD_DIM + 1] for a in accs], axis=0)
        o_ref[0, rows, :] = ot.T.astype(o_ref.dtype)
        return carry

    lax.fori_loop(0, q_ref.shape[1] // tq, query_tile, 0)


def _latent_attention(qc, kc, vt, score_bound, tq=512, tiles_per_step=4):
    B, S, _ = qc.shape
    pair = 2 * LANES
    rows = tq * tiles_per_step

    def call(stabilise, name):
        return pl.pallas_call(
            functools.partial(_mla_kernel, tq=tq, stabilise=stabilise),
            grid=(B, N_HEADS // 2, S // rows),
            in_specs=[pl.BlockSpec((1, rows, pair), lambda b, p, i: (b, i, p)),
                      pl.BlockSpec((1, S, pair), lambda b, p, i: (b, 0, p)),
                      pl.BlockSpec((1, 1, 2 * VT_ROWS, S), lambda b, p, i: (b, p, 0, 0))],
            out_specs=pl.BlockSpec((1, rows, LANES), lambda b, p, i: (b, i, p)),
            out_shape=jax.ShapeDtypeStruct((B, S, HALF_WIDTH), BF16),
            compiler_params=_params(3),
            name=name,
        )

    return lax.cond(score_bound <= MAX_RAW_SCORE,
                    call(False, "latent_attention"), call(True, "latent_attention_stabilised"),
                    qc, kc, vt)


def _mlp_kernel(x_ref, od_ref, om_ref, mod_ref, wout_ref, g_ref, w1_ref, w2_ref, o_ref, *, tf):
    D = x_ref.shape[-1]
    W = od_ref.shape[-1]
    mix = _dot(od_ref[0], wout_ref[0:W, :]) + _dot(om_ref[0], wout_ref[W:2 * W, :])
    x1 = x_ref[0] + mod_ref[0, 2:3, :] * mix
    h = _rms(x1, D) * g_ref[...]
    h = (h * (1.0 + mod_ref[0, 4:5, :]) + mod_ref[0, 3:4, :]).astype(BF16)
    y = jnp.zeros_like(x1)
    for c in range(w1_ref.shape[1] // tf):
        a = jnp.maximum(_dot(h, w1_ref[:, c * tf:(c + 1) * tf]), 0.0)
        y = y + _dot((a * a).astype(BF16), w2_ref[c * tf:(c + 1) * tf, :])
    o_ref[0] = x1 + mod_ref[0, 5:6, :] * y


def _outproj_mlp(x, o_dil, o_mla, mod_l, p, tm=512, tf=512):
    B, S, D = x.shape
    tok = lambda width: pl.BlockSpec((1, tm, width), lambda b, i: (b, i, 0))
    consts = [p["w_out"], p["g_norm_mlp"], p["w_mlp_in"], p["w_mlp_out"]]
    return pl.pallas_call(
        functools.partial(_mlp_kernel, tf=tf),
        grid=(B, S // tm),
        in_specs=[tok(D), tok(HALF_WIDTH), tok(HALF_WIDTH), pl.BlockSpec((1, 6, D), lambda b, i: (b, 0, 0))]
        + [_const_spec(a.shape) for a in consts],
        out_specs=tok(D),
        out_shape=jax.ShapeDtypeStruct((B, S, D), F32),
        compiler_params=_params(2),
        name="outproj_mlp",
    )(x, o_dil, o_mla, mod_l, *consts)


def _layer_params(l, g_norm_mix, w_in, g_q_dil, g_k_dil, g_cq, w_q_up, g_ckv, w_kv_up, g_q_nope,
                  g_q_rope, g_k_nope, g_k_rope, w_out, g_norm_mlp, w_mlp_in, w_mlp_out):
    W = HALF_WIDTH
    q_lora, kv_lora = g_cq.shape[-1], g_ckv.shape[-1]
    pad = LANES - HEAD_DIM - ROPE_DIM
    o_kr = 3 * W + q_lora + kv_lora
    w_kr = jnp.pad(w_in[l][:, o_kr:o_kr + ROPE_DIM], ((0, 0), (HEAD_DIM, pad)))
    wq = w_q_up[l].reshape(q_lora, N_HEADS, HEAD_DIM + ROPE_DIM)
    swap_halves = lambda a: jnp.concatenate([a[..., ROPE_DIM // 2:], a[..., :ROPE_DIM // 2]], axis=-1)
    wq_swapped = jnp.pad(swap_halves(wq[:, :, HEAD_DIM:]), ((0, 0), (0, 0), (HEAD_DIM, pad)))
    wq_swapped = wq_swapped.reshape(q_lora, N_HEADS * LANES)
    wq = jnp.pad(wq, ((0, 0), (0, 0), (0, pad))).reshape(q_lora, N_HEADS * LANES)
    lane = jnp.arange(LANES)
    nope_blk = (lane[:, None] < HEAD_DIM) & (lane[None, :] < HEAD_DIM)
    rope_rows = (lane >= HEAD_DIM) & (lane < HEAD_DIM + ROPE_DIM)
    rope_blk = rope_rows[:, None] & (lane[None, :] >= HEAD_DIM)
    tile_avg = nope_blk / HEAD_DIM + rope_blk / ROPE_DIM
    q_group_avg = jnp.kron(jnp.eye(2), tile_avg).astype(BF16)
    wkv = w_kv_up[l].reshape(kv_lora, N_HEADS, 2 * HEAD_DIM)
    wk = jnp.pad(wkv[:, :, :HEAD_DIM], ((0, 0), (0, 0), (0, LANES - HEAD_DIM))).reshape(kv_lora, N_HEADS * LANES)
    wvt = jnp.pad(wkv[:, :, HEAD_DIM:].transpose(1, 2, 0), ((0, 0), (0, VT_ROWS - HEAD_DIM), (0, 0)))
    vt_ones = jnp.zeros((N_HEADS, VT_ROWS, 1), F32).at[:, HEAD_DIM].set(1.0)
    gmax2 = lambda gn, gr: HEAD_DIM * jnp.max(gn[l] ** 2) + ROPE_DIM * jnp.max(gr[l] ** 2)
    score_bound = 1.02 * LOG2E * (HEAD_DIM + ROPE_DIM) ** -0.5 * jnp.sqrt(
        gmax2(g_q_nope, g_q_rope) * gmax2(g_k_nope, g_k_rope))
    row = lambda v: v.astype(F32)[None, :]
    return {
        "g_norm_mix": row(g_norm_mix[l]),
        "w_in": jnp.concatenate([w_in[l][:, :o_kr], w_kr], axis=1).astype(BF16),
        "g_q_dil": row(jnp.tile(g_q_dil[l], 2)),
        "g_k_dil": row(jnp.tile(g_k_dil[l], 2)),
        "g_cq": row(g_cq[l]),
        "w_q_up": wq.astype(BF16),
        "w_q_up_swapped": wq_swapped.astype(BF16),
        "q_group_avg": q_group_avg,
        "g_ckv": row(g_ckv[l]),
        "w_k_up": wk.astype(BF16),
        "w_vt_up": wvt.reshape(N_HEADS * VT_ROWS, kv_lora).astype(BF16),
        "vt_ones": vt_ones.reshape(N_HEADS * VT_ROWS, 1),
        "score_bound": score_bound,
        "dil_score_bound": 1.02 * LOG2E * HEAD_DIM ** 0.5 * jnp.max(jnp.abs(g_q_dil[l])) * jnp.max(jnp.abs(g_k_dil[l])),
        "g_q": row(jnp.tile(jnp.pad(jnp.concatenate([g_q_nope[l], g_q_rope[l]]), (0, pad)), 2)),
        "g_q_swapped": row(jnp.tile(jnp.pad(swap_halves(g_q_rope[l]), (HEAD_DIM, pad)), 2)),
        "g_k": row(jnp.pad(g_k_nope[l], (0, LANES - HEAD_DIM))),
        "g_kr": row(jnp.pad(g_k_rope[l], (HEAD_DIM, pad))),
        "w_out": w_out[l].astype(BF16),
        "g_norm_mlp": row(g_norm_mlp[l]),
        "w_mlp_in": w_mlp_in[l].astype(BF16),
        "w_mlp_out": w_mlp_out[l].astype(BF16),
    }


def kernel(x, c, positions, w_mod, b_mod, g_norm_mix, w_in, g_q_dil, g_k_dil, g_cq, w_q_up, g_ckv, w_kv_up, g_q_nope, g_q_rope, g_k_nope, g_k_rope, w_out, g_norm_mlp, w_mlp_in, w_mlp_out):
    B, S, D = x.shape
    depth = w_mod.shape[0]
    qb = 128
    cos_t, sin_t = _rope_tables(positions)
    mod = _modulation(c, w_mod, b_mod).reshape(depth, B, 6, D)
    pos_views = [_strided_positions(positions, dil, qb, window // (2 * dil)) for window, dil in DIL_CONFIGS]
    for l in range(depth):
        p = _layer_params(l, g_norm_mix, w_in, g_q_dil, g_k_dil, g_cq, w_q_up, g_ckv, w_kv_up, g_q_nope,
                          g_q_rope, g_k_nope, g_k_rope, w_out, g_norm_mlp, w_mlp_in, w_mlp_out)
        qd, kd, vd, qc, kc, vt = _inproj(x, mod[l], cos_t, sin_t, p)
        o_dil = _dilated_attention(qd, kd, vd, pos_views, p["dil_score_bound"], qb)
        o_mla = _latent_attention(qc, kc, vt, p["score_bound"])
        x = _outproj_mlp(x, o_dil, o_mla, mod[l], p)
    return x
```

```python
import functools
import math

import jax
import jax.numpy as jnp
from jax import lax
from jax.experimental import pallas as pl
from jax.experimental.pallas import tpu as pltpu

HEAD_DIM = 64
N_HEADS = 8
HALF_WIDTH = N_HEADS * HEAD_DIM
ROPE_DIM = HEAD_DIM // 2
DIL_CONFIGS = ((128, 1), (512, 4), (2048, 16))
RESIDUE_STEP = 4
ROPE_THETA = 10000.0
EPS = 1e-6
MASKED_DISTANCE = 1e33
LANES = 128
BF16_ROWS = 16
VT_ROWS = HEAD_DIM + BF16_ROWS
LOG2E = math.log2(math.e)
VMEM_LIMIT = 56 * 1024 * 1024
MAX_RAW_SCORE = 60.0

BF16 = jnp.bfloat16
F32 = jnp.float32


def _params(n_axes):
    return pltpu.CompilerParams(
        dimension_semantics=("arbitrary",) * n_axes, vmem_limit_bytes=VMEM_LIMIT)


def _lane_iota(shape):
    return lax.broadcasted_iota(jnp.int32, shape, len(shape) - 1)


def _dot_t(a, b):
    return lax.dot_general(a, b, (((1,), (1,)), ((), ())), preferred_element_type=F32)


def _dot(a, b):
    return jnp.dot(a, b, preferred_element_type=F32)


def _rms(x, width):
    return x * lax.rsqrt(jnp.sum(x * x, axis=-1, keepdims=True) * (1.0 / width) + EPS)


def _swap_rope_halves(y):
    up = pltpu.roll(y, LANES - ROPE_DIM // 2, axis=1)
    down = pltpu.roll(y, ROPE_DIM // 2, axis=1)
    return jnp.where(_lane_iota(y.shape) < HEAD_DIM + ROPE_DIM // 2, up, down)


def _rope_table_kernel(pos_ref, inv_ref, sign_ref, cos_ref, sin_ref):
    ang = pos_ref[0].astype(F32) * inv_ref[...]
    cos_ref[0] = jnp.cos(ang)
    sin_ref[0] = jnp.sin(ang) * sign_ref[...]


def _rope_tables(positions, ts=1024):
    B, S = positions.shape
    half = ROPE_DIM // 2
    inv = ROPE_THETA ** (-jnp.arange(half, dtype=F32) / half)
    zeros = jnp.zeros((HEAD_DIM,), F32)
    inv_row = jnp.concatenate([zeros, inv, inv, jnp.zeros((LANES - HEAD_DIM - ROPE_DIM,), F32)])[None]
    sign_row = jnp.concatenate([zeros, -jnp.ones((half,), F32), jnp.ones((half,), F32),
                                jnp.zeros((LANES - HEAD_DIM - ROPE_DIM,), F32)])[None]
    out = jax.ShapeDtypeStruct((B, S, LANES), F32)
    return pl.pallas_call(
        _rope_table_kernel,
        grid=(B, S // ts),
        in_specs=[pl.BlockSpec((1, ts, 1), lambda b, i: (b, i, 0)),
                  pl.BlockSpec((1, LANES), lambda b, i: (0, 0)),
                  pl.BlockSpec((1, LANES), lambda b, i: (0, 0))],
        out_specs=[pl.BlockSpec((1, ts, LANES), lambda b, i: (b, i, 0))] * 2,
        out_shape=[out, out],
        compiler_params=_params(2),
        name="rope_tables",
    )(positions[..., None], inv_row, sign_row)


def _mod_kernel(ct_ref, w_ref, b_ref, o_ref):
    ct = ct_ref[...]
    act = ct * (1.0 / (1.0 + jnp.exp(-ct)))
    w = w_ref[0]
    for b in range(ct.shape[1]):
        o_ref[0, b:b + 1, :] = jnp.sum(act[:, b:b + 1] * w, axis=0, keepdims=True) + b_ref[0]


def _modulation(c, w_mod, b_mod, tn=1536):
    depth, D, n_out = w_mod.shape
    B = c.shape[0]
    return pl.pallas_call(
        _mod_kernel,
        grid=(depth, n_out // tn),
        in_specs=[pl.BlockSpec((D, B), lambda l, j: (0, 0)),
                  pl.BlockSpec((1, D, tn), lambda l, j: (l, 0, j)),
                  pl.BlockSpec((1, 1, tn), lambda l, j: (l, 0, j))],
        out_specs=pl.BlockSpec((1, B, tn), lambda l, j: (l, 0, j)),
        out_shape=jax.ShapeDtypeStruct((depth, B, n_out), F32),
        compiler_params=_params(2),
        name="modulation",
    )(c.T, w_mod, b_mod[:, None, :])


def _head_pair_norm(t):
    sq = t * t
    low = _lane_iota(t.shape) < HEAD_DIM
    tot = jnp.sum(sq, axis=-1, keepdims=True)
    lo = jnp.sum(jnp.where(low, sq, 0.0), axis=-1, keepdims=True)
    ss = jnp.where(low, lo, tot - lo)
    return t * lax.rsqrt(ss * (1.0 / HEAD_DIM) + EPS)


def _store_residue_layouts(tile, p, out_refs, stage_ref, gather_ref):
    natural_ref, by4_ref, by16_ref = out_refs
    rows = tile.shape[0]
    W = HALF_WIDTH
    step = RESIDUE_STEP
    natural_ref[0, :, p * LANES:(p + 1) * LANES] = tile.astype(BF16)
    stage_ref[...] = tile
    for a in range(step):
        part = stage_ref[pl.ds(a, rows // step, stride=step), :]
        by4_ref[0, :, a * W + p * LANES:a * W + (p + 1) * LANES] = part.astype(BF16)
        gather_ref[a] = part
        for b in range(step):
            sub = gather_ref[a, pl.ds(b, rows // step ** 2, stride=step), :]
            col = (step * b + a) * W + p * LANES
            by16_ref[0, :, col:col + LANES] = sub.astype(BF16)


def _inproj_kernel(x_ref, mod_ref, gmix_ref, win_ref, gqd_ref, gkd_ref, gcq_ref, wq_ref, wqs_ref, qavg_ref,
                   gckv_ref, wk_ref, wvt_ref, vone_ref, gq_ref, gqs_ref, gk_ref, gkr_ref, cos_ref, sin_ref,
                   qd1_ref, qd4_ref, qd16_ref, kd1_ref, kd4_ref, kd16_ref, vd1_ref, vd4_ref, vd16_ref,
                   qc_ref, kc_ref, vt_ref, stage_ref, gather_ref, *, q_scale):
    D = x_ref.shape[-1]
    x = x_ref[0]
    h = _rms(x, D) * gmix_ref[...]
    h = (h * (1.0 + mod_ref[0, 1:2, :]) + mod_ref[0, 0:1, :]).astype(BF16)
    cos = cos_ref[0]
    sin = sin_ref[0]
    W = HALF_WIDTH

    q_lora = wq_ref.shape[0]
    kv_lora = wk_ref.shape[0]
    o_cq = 3 * W
    o_ckv = o_cq + q_lora
    o_kr = o_ckv + kv_lora
    cq = _dot(h, win_ref[:, o_cq:o_ckv])
    ckv_kr = _dot(h, win_ref[:, o_ckv:o_kr + LANES])
    cqn = (_rms(cq, q_lora) * gcq_ref[...]).astype(BF16)
    cos2 = jnp.concatenate([cos, cos], axis=1)
    sin2 = jnp.concatenate([sin, sin], axis=1)
    for hp in range(N_HEADS // 2):
        cols = slice(2 * hp * LANES, 2 * (hp + 1) * LANES)
        t = _dot(cqn, wq_ref[:, cols])
        t_swapped = _dot(cqn, wqs_ref[:, cols])
        r = lax.rsqrt(_dot((t * t).astype(BF16), qavg_ref[...]) + EPS)
        y = (t * gq_ref[...]) * cos2 + (t_swapped * gqs_ref[...]) * sin2
        qc_ref[0, :, cols] = (y * (r * q_scale)).astype(BF16)

    ckvn = (_rms(ckv_kr[:, :kv_lora], kv_lora) * gckv_ref[...]).astype(BF16)
    vt = _dot_t(wvt_ref[...], ckvn) + vone_ref[...]
    pair_rows = 2 * VT_ROWS
    for p in range(N_HEADS // 2):
        vt_ref[0, p] = vt[p * pair_rows:(p + 1) * pair_rows].astype(BF16)
    kr = ckv_kr[:, kv_lora:]
    kr = _rms(kr, ROPE_DIM) * gkr_ref[...]
    kr = kr * cos + _swap_rope_halves(kr) * sin
    for hh in range(N_HEADS):
        if hh % 2 == 0:
            t2 = _dot(ckvn, wk_ref[:, hh * LANES:(hh + 2) * LANES])
        t = t2[:, (hh % 2) * LANES:(hh % 2 + 1) * LANES]
        kc_ref[0, :, hh * LANES:(hh + 1) * LANES] = (_rms(t, HEAD_DIM) * gk_ref[...] + kr).astype(BF16)

    n_tiles = W // LANES
    operands = (((qd1_ref, qd4_ref, qd16_ref), gqd_ref, LOG2E * HEAD_DIM ** -0.5),
                ((kd1_ref, kd4_ref, kd16_ref), gkd_ref, 1.0),
                ((vd1_ref, vd4_ref, vd16_ref), None, None))
    for o, (out_refs, g_ref, scale) in enumerate(operands):
        for p in range(n_tiles):
            if p % 2 == 0:
                t2 = _dot(h, win_ref[:, o * W + p * LANES:o * W + (p + 2) * LANES])
            t = t2[:, (p % 2) * LANES:(p % 2 + 1) * LANES]
            if g_ref is not None:
                t = _head_pair_norm(t) * (g_ref[...] * scale)
            _store_residue_layouts(t, p, out_refs, stage_ref.at[o * n_tiles + p], gather_ref.at[o * n_tiles + p])


def _const_spec(shape):
    return pl.BlockSpec(shape, lambda b, i: (0,) * len(shape), pipeline_mode=pl.Buffered(1))


def _inproj(x, mod_l, cos_t, sin_t, p, tm=512):
    B, S, D = x.shape
    W = HALF_WIDTH
    consts = [p["g_norm_mix"], p["w_in"], p["g_q_dil"], p["g_k_dil"], p["g_cq"], p["w_q_up"],
              p["w_q_up_swapped"], p["q_group_avg"], p["g_ckv"], p["w_k_up"], p["w_vt_up"], p["vt_ones"],
              p["g_q"], p["g_q_swapped"], p["g_k"], p["g_kr"]]
    assert tuple(d for _, d in DIL_CONFIGS) == (1, RESIDUE_STEP, RESIDUE_STEP ** 2)
    tok = lambda width, rows=tm: pl.BlockSpec((1, rows, width), lambda b, i: (b, i, 0))
    dil_shapes = [(S // d, d * W) for _ in range(3) for _, d in DIL_CONFIGS]
    tok_shapes = dil_shapes + [(S, N_HEADS * LANES)] * 2
    vt_shape = (B, N_HEADS // 2, 2 * VT_ROWS, S)
    q_scale = (HEAD_DIM + ROPE_DIM) ** -0.5 * LOG2E
    n_tiles = 3 * W // LANES
    outs = pl.pallas_call(
        functools.partial(_inproj_kernel, q_scale=q_scale),
        grid=(B, S // tm),
        in_specs=[tok(D), pl.BlockSpec((1, 6, D), lambda b, i: (b, 0, 0))]
        + [_const_spec(a.shape) for a in consts] + [tok(LANES), tok(LANES)],
        out_specs=[tok(width, tm * rows // S) for rows, width in tok_shapes]
        + [pl.BlockSpec((1,) + vt_shape[1:3] + (tm,), lambda b, i: (b, 0, 0, i))],
        out_shape=[jax.ShapeDtypeStruct((B,) + s, BF16) for s in tok_shapes]
        + [jax.ShapeDtypeStruct(vt_shape, BF16)],
        scratch_shapes=[pltpu.VMEM((n_tiles, tm, LANES), F32),
                        pltpu.VMEM((n_tiles, RESIDUE_STEP, tm // RESIDUE_STEP, LANES), F32)],
        compiler_params=_params(2),
        name="inproj",
    )(x, mod_l, *consts, cos_t, sin_t)
    n_dil = len(DIL_CONFIGS)
    qd, kd, vd = (outs[o * n_dil:(o + 1) * n_dil] for o in range(3))
    return qd, kd, vd, outs[-3], outs[-2], outs[-1]


def _alibi_slope(h):
    return 2.0 ** (-8.0 * (h + 1) / N_HEADS)


def _halo_rows(prev_ref, main_ref, next_ref, lo, hi):
    rows = main_ref.shape[1]
    parts = []
    if lo < 0:
        parts.append(prev_ref[0])
    parts.append(main_ref[0, max(lo, 0):min(hi, rows), :])
    if hi > rows:
        parts.append(next_ref[0])
    return parts[0] if len(parts) == 1 else jnp.concatenate(parts, axis=0)


def _to_token_order(blk_ref, dil, nat_ref, tmp_ref):
    W = HALF_WIDTH
    step = RESIDUE_STEP
    rows = nat_ref.shape[1]
    for p in range(W // LANES):
        for a in range(step):
            if dil == step:
                part = blk_ref[0, :, a * W + p * LANES:a * W + (p + 1) * LANES]
            else:
                for b in range(step):
                    col = (step * b + a) * W + p * LANES
                    tmp_ref[p, a, pl.ds(b, rows // step ** 2, stride=step), :] = blk_ref[0, :, col:col + LANES]
                part = tmp_ref[p, a]
            nat_ref[p, pl.ds(a, rows // step, stride=step), :] = part


def _dilated_kernel(q_ref, kp_ref, km_ref, kn_ref, vp_ref, vm_ref, vn_ref, pk_ref, *rest,
                    qb, half, length, merge_dils, online):
    step_rows = q_ref.shape[1]
    nkeys = qb + 2 * half
    n = pl.program_id(2)
    merge = bool(merge_dils)
    if merge:
        n_other = 2 * len(merge_dils)
        out_ref, nat_ref, tmp_ref = rest[n_other:]
        for c, dil in enumerate(merge_dils):
            for t in range(2):
                _to_token_order(rest[2 * c + t], dil, nat_ref.at[2 * c + t], tmp_ref.at[2 * c + t])
    row = lax.broadcasted_iota(jnp.int32, (qb, nkeys), 0)
    col = lax.broadcasted_iota(jnp.int32, (qb, nkeys), 1)
    in_band = jnp.abs(col - half - row) <= half
    low = _lane_iota((qb, LANES)) < HEAD_DIM

    for j in range(step_rows // qb):
        rows = slice(j * qb, (j + 1) * qb)
        q = q_ref[0, rows, :]
        k_all = _halo_rows(kp_ref, km_ref, kn_ref, j * qb - half, (j + 1) * qb + half)
        v_all = _halo_rows(vp_ref, vm_ref, vn_ref, j * qb - half, (j + 1) * qb + half)
        pk = pk_ref[0, 0, j]
        pq = jnp.sum(jnp.where(col == row + half, pk, 0), axis=-1, keepdims=True)
        key_u = n * step_rows + j * qb - half + col
        mask = in_band & (key_u >= 0) & (key_u < length)
        dist = jnp.where(mask, jnp.abs(pq - pk).astype(F32), MASKED_DISTANCE)

        for p in range(HALF_WIDTH // LANES):
            sl = slice(p * LANES, (p + 1) * LANES)
            qp, kp, vp = q[:, sl], k_all[:, sl], v_all[:, sl]
            zero = jnp.zeros_like(qp)
            s2 = _dot_t(jnp.concatenate([jnp.where(low, qp, zero), jnp.where(low, zero, qp)], axis=0), kp)
            es, stats, scales = [], [], []
            for hh in range(2):
                s = s2[hh * qb:(hh + 1) * qb] + dist * (-LOG2E * _alibi_slope(2 * p + hh))
                if online:
                    m = jnp.max(s, axis=-1, keepdims=True)
                    e = jnp.exp2(s - m)
                    l = jnp.sum(e, axis=-1, keepdims=True)
                    stats.append(m + jnp.log2(l))
                    scales.append(1.0 / l)
                else:
                    e = jnp.exp2(s)
                    stats.append(jnp.sum(e, axis=-1, keepdims=True))
                es.append(e.astype(BF16))
            pv2 = _dot(jnp.concatenate(es, axis=0), vp)
            outs = [pv2[hh * qb:(hh + 1) * qb] * scales[hh] if online else pv2[hh * qb:(hh + 1) * qb]
                    for hh in range(2)]
            o = jnp.where(low, outs[0], outs[1])
            stat = jnp.where(low, stats[0], stats[1])
            if not merge:
                rest[0][0, rows, sl] = o
                rest[1][0, rows, sl] = stat
                continue
            others = [(nat_ref[2 * c, p, rows, :], nat_ref[2 * c + 1, p, rows, :]) for c in range(len(merge_dils))]
            if online:
                top = stat
                for _, lc in others:
                    top = jnp.maximum(top, lc)
                w = jnp.exp2(stat - top)
                num, den = o * w, w
                for oc, lc in others:
                    w = jnp.exp2(lc - top)
                    num, den = num + oc * w, den + w
            else:
                num, den = o, stat
                for oc, lc in others:
                    num, den = num + oc, den + lc
            out_ref[0, rows, sl] = (num / den).astype(out_ref.dtype)


def _strided_positions(positions, dil, qb, half):
    B, S = positions.shape
    L = S // dil
    pos_s = positions.reshape(B, L, dil).transpose(0, 2, 1)
    padded = jnp.pad(pos_s, ((0, 0), (0, 0), (half, qb)))
    nb = L // qb
    parts = [padded[:, :, off:off + L].reshape(B, dil, nb, qb)[..., :w]
             for off, w in ((0, qb), (qb, 2 * half))]
    return jnp.concatenate(parts, axis=-1)[:, :, :, None, :]


def _dilated_config(qd, kd, vd, pos_k, window, dil, online, prev=(), qb=128, n_sub=4):
    B, L, width = qd.shape
    W = width // dil
    half = window // (2 * dil)
    step_rows = n_sub * qb
    assert L % step_rows == 0 and step_rows % half == 0 and half % BF16_ROWS == 0
    halo_per_step = step_rows // half
    last_halo = L // half - 1
    main = pl.BlockSpec((1, step_rows, W), lambda b, r, n: (b, n, r))
    before = pl.BlockSpec((1, half, W), lambda b, r, n: (b, jnp.maximum(n * halo_per_step - 1, 0), r))
    after = pl.BlockSpec((1, half, W), lambda b, r, n: (b, jnp.minimum((n + 1) * halo_per_step, last_halo), r))

    in_specs = [main, before, main, after, before, main, after,
                pl.BlockSpec((1, 1, n_sub, 1, qb + 2 * half), lambda b, r, n: (b, r, n, 0, 0))]
    args = [qd] + [kd] * 3 + [vd] * 3 + [pos_k]
    scratch = []
    if prev:
        assert dil == 1
        for d, o, lse in prev:
            in_specs += [pl.BlockSpec((1, step_rows // d, d * W), lambda b, r, n: (b, n, 0))] * 2
            args += [o, lse]
        n_tiles = W // LANES
        scratch = [pltpu.VMEM((2 * len(prev), n_tiles, step_rows, LANES), F32),
                   pltpu.VMEM((2 * len(prev), n_tiles, RESIDUE_STEP, step_rows // RESIDUE_STEP, LANES), F32)]
        out_specs = main
        out_shape = jax.ShapeDtypeStruct((B, L, W), BF16)
    else:
        out_specs = [main, main]
        out_shape = [jax.ShapeDtypeStruct((B, L, dil * W), F32)] * 2
    return pl.pallas_call(
        functools.partial(_dilated_kernel, qb=qb, half=half, length=L, merge_dils=tuple(d for d, _, _ in prev),
                          online=online),
        grid=(B, dil, L // step_rows),
        in_specs=in_specs, out_specs=out_specs, out_shape=out_shape,
        scratch_shapes=scratch,
        compiler_params=_params(3),
        name=f"dilated_d{dil}" + ("_online" if online else ""),
    )(*args)


def _dilated_attention(qd, kd, vd, pos_views, score_bound, qb):
    def stage(online):
        def run(qd, kd, vd, pos_views):
            prev = []
            for c in range(len(DIL_CONFIGS) - 1, 0, -1):
                window, dil = DIL_CONFIGS[c]
                prev.append((dil, *_dilated_config(qd[c], kd[c], vd[c], pos_views[c], window, dil, online, qb=qb)))
            window, dil = DIL_CONFIGS[0]
            return _dilated_config(qd[0], kd[0], vd[0], pos_views[0], window, dil, online, prev=tuple(prev), qb=qb)
        return run

    return lax.cond(score_bound <= MAX_RAW_SCORE, stage(False), stage(True), qd, kd, vd, pos_views)


def _mla_kernel(q_ref, k_ref, vt_ref, o_ref, *, tq, stabilise):
    def query_tile(i, carry):
        rows = pl.ds(pl.multiple_of(i * tq, tq), tq)
        sts = [_dot_t(k_ref[0, :, hh * LANES:(hh + 1) * LANES], q_ref[0, rows, hh * LANES:(hh + 1) * LANES])
               for hh in range(2)]
        accs = []
        for hh in range(2):
            st = sts[hh]
            if stabilise:
                st = st - jnp.max(st, axis=0, keepdims=True)
            accs.append(_dot(vt_ref[0, 0, hh * VT_ROWS:(hh + 1) * VT_ROWS, :], jnp.exp2(st).astype(BF16)))
        ot = jnp.concatenate([a[:HEAD_DIM] / a[HEAD_DIM:HEAD_DIM + 1] for a in accs], axis=0)
        o_ref[0, rows, :] = ot.T.astype(o_ref.dtype)
        return carry

    lax.fori_loop(0, q_ref.shape[1] // tq, query_tile, 0)


def _latent_attention(qc, kc, vt, score_bound, tq=512, tiles_per_step=4):
    B, S, _ = qc.shape
    pair = 2 * LANES
    rows = tq * tiles_per_step

    def call(stabilise, name):
        return pl.pallas_call(
            functools.partial(_mla_kernel, tq=tq, stabilise=stabilise),
            grid=(B, N_HEADS // 2, S // rows),
            in_specs=[pl.BlockSpec((1, rows, pair), lambda b, p, i: (b, i, p)),
                      pl.BlockSpec((1, S, pair), lambda b, p, i: (b, 0, p)),
                      pl.BlockSpec((1, 1, 2 * VT_ROWS, S), lambda b, p, i: (b, p, 0, 0))],
            out_specs=pl.BlockSpec((1, rows, LANES), lambda b, p, i: (b, i, p)),
            out_shape=jax.ShapeDtypeStruct((B, S, HALF_WIDTH), BF16),
            compiler_params=_params(3),
            name=name,
        )

    return lax.cond(score_bound <= MAX_RAW_SCORE,
                    call(False, "latent_attention"), call(True, "latent_attention_stabilised"),
                    qc, kc, vt)


def _mlp_kernel(x_ref, od_ref, om_ref, mod_ref, wout_ref, g_ref, w1_ref, w2_ref, o_ref, *, tf):
    D = x_ref.shape[-1]
    W = od_ref.shape[-1]
    mix = _dot(od_ref[0], wout_ref[0:W, :]) + _dot(om_ref[0], wout_ref[W:2 * W, :])
    x1 = x_ref[0] + mod_ref[0, 2:3, :] * mix
    h = _rms(x1, D) * g_ref[...]
    h = (h * (1.0 + mod_ref[0, 4:5, :]) + mod_ref[0, 3:4, :]).astype(BF16)
    y = jnp.zeros_like(x1)
    for c in range(w1_ref.shape[1] // tf):
        a = jnp.maximum(_dot(h, w1_ref[:, c * tf:(c + 1) * tf]), 0.0)
        y = y + _dot((a * a).astype(BF16), w2_ref[c * tf:(c + 1) * tf, :])
    o_ref[0] = x1 + mod_ref[0, 5:6, :] * y


def _outproj_mlp(x, o_dil, o_mla, mod_l, p, tm=512, tf=512):
    B, S, D = x.shape
    tok = lambda width: pl.BlockSpec((1, tm, width), lambda b, i: (b, i, 0))
    consts = [p["w_out"], p["g_norm_mlp"], p["w_mlp_in"], p["w_mlp_out"]]
    return pl.pallas_call(
        functools.partial(_mlp_kernel, tf=tf),
        grid=(B, S // tm),
        in_specs=[tok(D), tok(HALF_WIDTH), tok(HALF_WIDTH), pl.BlockSpec((1, 6, D), lambda b, i: (b, 0, 0))]
        + [_const_spec(a.shape) for a in consts],
        out_specs=tok(D),
        out_shape=jax.ShapeDtypeStruct((B, S, D), F32),
        compiler_params=_params(2),
        name="outproj_mlp",
    )(x, o_dil, o_mla, mod_l, *consts)


def _layer_params(l, g_norm_mix, w_in, g_q_dil, g_k_dil, g_cq, w_q_up, g_ckv, w_kv_up, g_q_nope,
                  g_q_rope, g_k_nope, g_k_rope, w_out, g_norm_mlp, w_mlp_in, w_mlp_out):
    W = HALF_WIDTH
    q_lora, kv_lora = g_cq.shape[-1], g_ckv.shape[-1]
    pad = LANES - HEAD_DIM - ROPE_DIM
    o_kr = 3 * W + q_lora + kv_lora
    w_kr = jnp.pad(w_in[l][:, o_kr:o_kr + ROPE_DIM], ((0, 0), (HEAD_DIM, pad)))
    wq = w_q_up[l].reshape(q_lora, N_HEADS, HEAD_DIM + ROPE_DIM)
    swap_halves = lambda a: jnp.concatenate([a[..., ROPE_DIM // 2:], a[..., :ROPE_DIM // 2]], axis=-1)
    wq_swapped = jnp.pad(swap_halves(wq[:, :, HEAD_DIM:]), ((0, 0), (0, 0), (HEAD_DIM, pad)))
    wq_swapped = wq_swapped.reshape(q_lora, N_HEADS * LANES)
    wq = jnp.pad(wq, ((0, 0), (0, 0), (0, pad))).reshape(q_lora, N_HEADS * LANES)
    lane = jnp.arange(LANES)
    nope_blk = (lane[:, None] < HEAD_DIM) & (lane[None, :] < HEAD_DIM)
    rope_rows = (lane >= HEAD_DIM) & (lane < HEAD_DIM + ROPE_DIM)
    rope_blk = rope_rows[:, None] & (lane[None, :] >= HEAD_DIM)
    tile_avg = nope_blk / HEAD_DIM + rope_blk / ROPE_DIM
    q_group_avg = jnp.kron(jnp.eye(2), tile_avg).astype(BF16)
    wkv = w_kv_up[l].reshape(kv_lora, N_HEADS, 2 * HEAD_DIM)
    wk = jnp.pad(wkv[:, :, :HEAD_DIM], ((0, 0), (0, 0), (0, LANES - HEAD_DIM))).reshape(kv_lora, N_HEADS * LANES)
    wvt = jnp.pad(wkv[:, :, HEAD_DIM:].transpose(1, 2, 0), ((0, 0), (0, VT_ROWS - HEAD_DIM), (0, 0)))
    vt_ones = jnp.zeros((N_HEADS, VT_ROWS, 1), F32).at[:, HEAD_DIM].set(1.0)
    gmax2 = lambda gn, gr: HEAD_DIM * jnp.max(gn[l] ** 2) + ROPE_DIM * jnp.max(gr[l] ** 2)
    score_bound = 1.02 * LOG2E * (HEAD_DIM + ROPE_DIM) ** -0.5 * jnp.sqrt(
        gmax2(g_q_nope, g_q_rope) * gmax2(g_k_nope, g_k_rope))
    row = lambda v: v.astype(F32)[None, :]
    return {
        "g_norm_mix": row(g_norm_mix[l]),
        "w_in": jnp.concatenate([w_in[l][:, :o_kr], w_kr], axis=1).astype(BF16),
        "g_q_dil": row(jnp.tile(g_q_dil[l], 2)),
        "g_k_dil": row(jnp.tile(g_k_dil[l], 2)),
        "g_cq": row(g_cq[l]),
        "w_q_up": wq.astype(BF16),
        "w_q_up_swapped": wq_swapped.astype(BF16),
        "q_group_avg": q_group_avg,
        "g_ckv": row(g_ckv[l]),
        "w_k_up": wk.astype(BF16),
        "w_vt_up": wvt.reshape(N_HEADS * VT_ROWS, kv_lora).astype(BF16),
        "vt_ones": vt_ones.reshape(N_HEADS * VT_ROWS, 1),
        "score_bound": score_bound,
        "dil_score_bound": 1.02 * LOG2E * HEAD_DIM ** 0.5 * jnp.max(jnp.abs(g_q_dil[l])) * jnp.max(jnp.abs(g_k_dil[l])),
        "g_q": row(jnp.tile(jnp.pad(jnp.concatenate([g_q_nope[l], g_q_rope[l]]), (0, pad)), 2)),
        "g_q_swapped": row(jnp.tile(jnp.pad(swap_halves(g_q_rope[l]), (HEAD_DIM, pad)), 2)),
        "g_k": row(jnp.pad(g_k_nope[l], (0, LANES - HEAD_DIM))),
        "g_kr": row(jnp.pad(g_k_rope[l], (HEAD_DIM, pad))),
        "w_out": w_out[l].astype(BF16),
        "g_norm_mlp": row(g_norm_mlp[l]),
        "w_mlp_in": w_mlp_in[l].astype(BF16),
        "w_mlp_out": w_mlp_out[l].astype(BF16),
    }


def kernel(x, c, positions, w_mod, b_mod, g_norm_mix, w_in, g_q_dil, g_k_dil, g_cq, w_q_up, g_ckv, w_kv_up, g_q_nope, g_q_rope, g_k_nope, g_k_rope, w_out, g_norm_mlp, w_mlp_in, w_mlp_out):
    B, S, D = x.shape
    depth = w_mod.shape[0]
    qb = 128
    cos_t, sin_t = _rope_tables(positions)
    mod = _modulation(c, w_mod, b_mod).reshape(depth, B, 6, D)
    pos_views = [_strided_positions(positions, dil, qb, window // (2 * dil)) for window, dil in DIL_CONFIGS]
    for l in range(depth):
        p = _layer_params(l, g_norm_mix, w_in, g_q_dil, g_k_dil, g_cq, w_q_up, g_ckv, w_kv_up, g_q_nope,
                          g_q_rope, g_k_nope, g_k_rope, w_out, g_norm_mlp, w_mlp_in, w_mlp_out)
        qd, kd, vd, qc, kc, vt = _inproj(x, mod[l], cos_t, sin_t, p)
        o_dil = _dilated_attention(qd, kd, vd, pos_views, p["dil_score_bound"], qb)
        o_mla = _latent_attention(qc, kc, vt, p["score_bound"])
        x = _outproj_mlp(x, o_dil, o_mla, mod[l], p)
    return x
```

```python
import functools
import math

import jax
import jax.numpy as jnp
from jax import lax
from jax.experimental import pallas as pl
from jax.experimental.pallas import tpu as pltpu

HEAD_DIM = 64
N_HEADS = 8
HALF_WIDTH = N_HEADS * HEAD_DIM
ROPE_DIM = HEAD_DIM // 2
DIL_CONFIGS = ((128, 1), (512, 4), (2048, 16))
RESIDUE_STEP = 4
ROPE_THETA = 10000.0
EPS = 1e-6
MASKED_DISTANCE = 1e33
LANES = 128
BF16_ROWS = 16
VT_ROWS = HEAD_DIM + BF16_ROWS
LOG2E = math.log2(math.e)
VMEM_LIMIT = 56 * 1024 * 1024
MAX_RAW_SCORE = 60.0

BF16 = jnp.bfloat16
F32 = jnp.float32


def _params(n_axes):
    return pltpu.CompilerParams(
        dimension_semantics=("arbitrary",) * n_axes, vmem_limit_bytes=VMEM_LIMIT)


def _lane_iota(shape):
    return lax.broadcasted_iota(jnp.int32, shape, len(shape) - 1)


def _dot_t(a, b):
    return lax.dot_general(a, b, (((1,), (1,)), ((), ())), preferred_element_type=F32)


def _dot(a, b):
    return jnp.dot(a, b, preferred_element_type=F32)


def _rms(x, width):
    return x * lax.rsqrt(jnp.sum(x * x, axis=-1, keepdims=True) * (1.0 / width) + EPS)


def _swap_rope_halves(y):
    up = pltpu.roll(y, LANES - ROPE_DIM // 2, axis=1)
    down = pltpu.roll(y, ROPE_DIM // 2, axis=1)
    return jnp.where(_lane_iota(y.shape) < HEAD_DIM + ROPE_DIM // 2, up, down)


def _rope_table_kernel(pos_ref, inv_ref, sign_ref, cos_ref, sin_ref):
    ang = pos_ref[0].astype(F32) * inv_ref[...]
    cos_ref[0] = jnp.cos(ang)
    sin_ref[0] = jnp.sin(ang) * sign_ref[...]


def _rope_tables(positions, ts=1024):
    B, S = positions.shape
    half = ROPE_DIM // 2
    inv = ROPE_THETA ** (-jnp.arange(half, dtype=F32) / half)
    zeros = jnp.zeros((HEAD_DIM,), F32)
    inv_row = jnp.concatenate([zeros, inv, inv, jnp.zeros((LANES - HEAD_DIM - ROPE_DIM,), F32)])[None]
    sign_row = jnp.concatenate([zeros, -jnp.ones((half,), F32), jnp.ones((half,), F32),
                                jnp.zeros((LANES - HEAD_DIM - ROPE_DIM,), F32)])[None]
    out = jax.ShapeDtypeStruct((B, S, LANES), F32)
    return pl.pallas_call(
        _rope_table_kernel,
        grid=(B, S // ts),
        in_specs=[pl.BlockSpec((1, ts, 1), lambda b, i: (b, i, 0)),
                  pl.BlockSpec((1, LANES), lambda b, i: (0, 0)),
                  pl.BlockSpec((1, LANES), lambda b, i: (0, 0))],
        out_specs=[pl.BlockSpec((1, ts, LANES), lambda b, i: (b, i, 0))] * 2,
        out_shape=[out, out],
        compiler_params=_params(2),
        name="rope_tables",
    )(positions[..., None], inv_row, sign_row)


def _mod_kernel(ct_ref, w_ref, b_ref, o_ref):
    ct = ct_ref[...]
    act = ct * (1.0 / (1.0 + jnp.exp(-ct)))
    w = w_ref[0]
    for b in range(ct.shape[1]):
        o_ref[0, b:b + 1, :] = jnp.sum(act[:, b:b + 1] * w, axis=0, keepdims=True) + b_ref[0]


def _modulation(c, w_mod, b_mod, tn=1536):
    depth, D, n_out = w_mod.shape
    B = c.shape[0]
    return pl.pallas_call(
        _mod_kernel,
        grid=(depth, n_out // tn),
        in_specs=[pl.BlockSpec((D, B), lambda l, j: (0, 0)),
                  pl.BlockSpec((1, D, tn), lambda l, j: (l, 0, j)),
                  pl.BlockSpec((1, 1, tn), lambda l, j: (l, 0, j))],
        out_specs=pl.BlockSpec((1, B, tn), lambda l, j: (l, 0, j)),
        out_shape=jax.ShapeDtypeStruct((depth, B, n_out), F32),
        compiler_params=_params(2),
        name="modulation",
    )(c.T, w_mod, b_mod[:, None, :])


def _head_pair_norm(t):
    sq = t * t
    low = _lane_iota(t.shape) < HEAD_DIM
    tot = jnp.sum(sq, axis=-1, keepdims=True)
    lo = jnp.sum(jnp.where(low, sq, 0.0), axis=-1, keepdims=True)
    ss = jnp.where(low, lo, tot - lo)
    return t * lax.rsqrt(ss * (1.0 / HEAD_DIM) + EPS)


def _store_residue_layouts(tile, p, out_refs, stage_ref, gather_ref):
    natural_ref, by4_ref, by16_ref = out_refs
    rows = tile.shape[0]
    W = HALF_WIDTH
    step = RESIDUE_STEP
    natural_ref[0, :, p * LANES:(p + 1) * LANES] = tile.astype(BF16)
    stage_ref[...] = tile
    for a in range(step):
        part = stage_ref[pl.ds(a, rows // step, stride=step), :]
        by4_ref[0, :, a * W + p * LANES:a * W + (p + 1) * LANES] = part.astype(BF16)
        gather_ref[a] = part
        for b in range(step):
            sub = gather_ref[a, pl.ds(b, rows // step ** 2, stride=step), :]
            col = (step * b + a) * W + p * LANES
            by16_ref[0, :, col:col + LANES] = sub.astype(BF16)


GAIN_ROWS = ("norm_mix", "q_dil", "k_dil", "cq", "ckv", "q", "q_swapped", "k", "k_rope")


def _inproj_kernel(x_ref, mod_ref, gains_ref, win_ref, wq_ref, wqs_ref, wk_ref, wvt_ref, qavg_ref, vone_ref,
                   cos_ref, sin_ref,
                   qd1_ref, qd4_ref, qd16_ref, kd1_ref, kd4_ref, kd16_ref, vd1_ref, vd4_ref, vd16_ref,
                   qc_ref, kc_ref, vt_ref, stage_ref, gather_ref, *, q_scale):
    D = x_ref.shape[-1]
    win_ref, wq_ref, wqs_ref, wk_ref, wvt_ref = (r.at[0] for r in (win_ref, wq_ref, wqs_ref, wk_ref, wvt_ref))
    gain = lambda name, width: gains_ref[0, GAIN_ROWS.index(name):GAIN_ROWS.index(name) + 1, :width]
    x = x_ref[0]
    h = _rms(x, D) * gain("norm_mix", D)
    h = (h * (1.0 + mod_ref[0, 0, 1:2, :]) + mod_ref[0, 0, 0:1, :]).astype(BF16)
    cos = cos_ref[0]
    sin = sin_ref[0]
    W = HALF_WIDTH

    q_lora = wq_ref.shape[0]
    kv_lora = wk_ref.shape[0]
    o_cq = 3 * W
    o_ckv = o_cq + q_lora
    o_kr = o_ckv + kv_lora
    cq = _dot(h, win_ref[:, o_cq:o_ckv])
    ckv_kr = _dot(h, win_ref[:, o_ckv:o_kr + LANES])
    cqn = (_rms(cq, q_lora) * gain("cq", q_lora)).astype(BF16)
    cos2 = jnp.concatenate([cos, cos], axis=1)
    sin2 = jnp.concatenate([sin, sin], axis=1)
    for hp in range(N_HEADS // 2):
        cols = slice(2 * hp * LANES, 2 * (hp + 1) * LANES)
        t = _dot(cqn, wq_ref[:, cols])
        t_swapped = _dot(cqn, wqs_ref[:, cols])
        r = lax.rsqrt(_dot((t * t).astype(BF16), qavg_ref[...]) + EPS)
        y = (t * gain("q", 2 * LANES)) * cos2 + (t_swapped * gain("q_swapped", 2 * LANES)) * sin2
        qc_ref[0, :, cols] = (y * (r * q_scale)).astype(BF16)

    ckvn = (_rms(ckv_kr[:, :kv_lora], kv_lora) * gain("ckv", kv_lora)).astype(BF16)
    vt = _dot_t(wvt_ref[...], ckvn) + vone_ref[...]
    pair_rows = 2 * VT_ROWS
    for p in range(N_HEADS // 2):
        vt_ref[0, p] = vt[p * pair_rows:(p + 1) * pair_rows].astype(BF16)
    kr = ckv_kr[:, kv_lora:]
    kr = _rms(kr, ROPE_DIM) * gain("k_rope", LANES)
    kr = kr * cos + _swap_rope_halves(kr) * sin
    for hh in range(N_HEADS):
        if hh % 2 == 0:
            t2 = _dot(ckvn, wk_ref[:, hh * LANES:(hh + 2) * LANES])
        t = t2[:, (hh % 2) * LANES:(hh % 2 + 1) * LANES]
        kc_ref[0, :, hh * LANES:(hh + 1) * LANES] = (_rms(t, HEAD_DIM) * gain("k", LANES) + kr).astype(BF16)

    n_tiles = W // LANES
    operands = (((qd1_ref, qd4_ref, qd16_ref), "q_dil", LOG2E * HEAD_DIM ** -0.5),
                ((kd1_ref, kd4_ref, kd16_ref), "k_dil", 1.0),
                ((vd1_ref, vd4_ref, vd16_ref), None, None))
    for o, (out_refs, gain_name, scale) in enumerate(operands):
        for p in range(n_tiles):
            if p % 2 == 0:
                t2 = _dot(h, win_ref[:, o * W + p * LANES:o * W + (p + 2) * LANES])
            t = t2[:, (p % 2) * LANES:(p % 2 + 1) * LANES]
            if gain_name is not None:
                t = _head_pair_norm(t) * (gain(gain_name, LANES) * scale)
            _store_residue_layouts(t, p, out_refs, stage_ref.at[o * n_tiles + p], gather_ref.at[o * n_tiles + p])


def _const_spec(shape):
    return pl.BlockSpec(shape, lambda b, i: (0,) * len(shape), pipeline_mode=pl.Buffered(1))


def _layer_spec(stacked, l):
    zeros = (0,) * (stacked.ndim - 1)
    return pl.BlockSpec((1,) + stacked.shape[1:], lambda b, i: (l,) + zeros, pipeline_mode=pl.Buffered(1))


def _mod_spec(mod, l):
    return pl.BlockSpec((1, 1) + mod.shape[2:], lambda b, i: (l, b, 0, 0))


def _inproj(x, mod, cos_t, sin_t, p, l, tm=512):
    B, S, D = x.shape
    W = HALF_WIDTH
    layer_params = [p["gains"], p["w_in"], p["w_q_up"], p["w_q_up_swapped"], p["w_k_up"], p["w_vt_up"]]
    consts = [p["q_group_avg"], p["vt_ones"]]
    assert tuple(d for _, d in DIL_CONFIGS) == (1, RESIDUE_STEP, RESIDUE_STEP ** 2)
    tok = lambda width, rows=tm: pl.BlockSpec((1, rows, width), lambda b, i: (b, i, 0))
    dil_shapes = [(S // d, d * W) for _ in range(3) for _, d in DIL_CONFIGS]
    tok_shapes = dil_shapes + [(S, N_HEADS * LANES)] * 2
    vt_shape = (B, N_HEADS // 2, 2 * VT_ROWS, S)
    q_scale = (HEAD_DIM + ROPE_DIM) ** -0.5 * LOG2E
    n_tiles = 3 * W // LANES
    outs = pl.pallas_call(
        functools.partial(_inproj_kernel, q_scale=q_scale),
        grid=(B, S // tm),
        in_specs=[tok(D), _mod_spec(mod, l)] + [_layer_spec(a, l) for a in layer_params]
        + [_const_spec(a.shape) for a in consts] + [tok(LANES), tok(LANES)],
        out_specs=[tok(width, tm * rows // S) for rows, width in tok_shapes]
        + [pl.BlockSpec((1,) + vt_shape[1:3] + (tm,), lambda b, i: (b, 0, 0, i))],
        out_shape=[jax.ShapeDtypeStruct((B,) + s, BF16) for s in tok_shapes]
        + [jax.ShapeDtypeStruct(vt_shape, BF16)],
        scratch_shapes=[pltpu.VMEM((n_tiles, tm, LANES), F32),
                        pltpu.VMEM((n_tiles, RESIDUE_STEP, tm // RESIDUE_STEP, LANES), F32)],
        compiler_params=_params(2),
        name="inproj",
    )(x, mod, *layer_params, *consts, cos_t, sin_t)
    n_dil = len(DIL_CONFIGS)
    qd, kd, vd = (outs[o * n_dil:(o + 1) * n_dil] for o in range(3))
    return qd, kd, vd, outs[-3], outs[-2], outs[-1]


def _alibi_slope(h):
    return 2.0 ** (-8.0 * (h + 1) / N_HEADS)


def _halo_rows(prev_ref, main_ref, next_ref, lo, hi):
    rows = main_ref.shape[1]
    parts = []
    if lo < 0:
        parts.append(prev_ref[0])
    parts.append(main_ref[0, max(lo, 0):min(hi, rows), :])
    if hi > rows:
        parts.append(next_ref[0])
    return parts[0] if len(parts) == 1 else jnp.concatenate(parts, axis=0)


def _to_token_order(blk_ref, dil, nat_ref, tmp_ref):
    W = HALF_WIDTH
    step = RESIDUE_STEP
    rows = nat_ref.shape[1]
    for p in range(W // LANES):
        for a in range(step):
            if dil == step:
                part = blk_ref[0, :, a * W + p * LANES:a * W + (p + 1) * LANES]
            else:
                for b in range(step):
                    col = (step * b + a) * W + p * LANES
                    tmp_ref[p, a, pl.ds(b, rows // step ** 2, stride=step), :] = blk_ref[0, :, col:col + LANES]
                part = tmp_ref[p, a]
            nat_ref[p, pl.ds(a, rows // step, stride=step), :] = part


def _dilated_kernel(q_ref, kp_ref, km_ref, kn_ref, vp_ref, vm_ref, vn_ref, pk_ref, *rest,
                    qb, half, length, merge_dils, online):
    step_rows = q_ref.shape[1]
    nkeys = qb + 2 * half
    n = pl.program_id(2)
    merge = bool(merge_dils)
    if merge:
        n_other = 2 * len(merge_dils)
        out_ref, nat_ref, tmp_ref = rest[n_other:]
        for c, dil in enumerate(merge_dils):
            for t in range(2):
                _to_token_order(rest[2 * c + t], dil, nat_ref.at[2 * c + t], tmp_ref.at[2 * c + t])
    row = lax.broadcasted_iota(jnp.int32, (qb, nkeys), 0)
    col = lax.broadcasted_iota(jnp.int32, (qb, nkeys), 1)
    in_band = jnp.abs(col - half - row) <= half
    low = _lane_iota((qb, LANES)) < HEAD_DIM

    for j in range(step_rows // qb):
        rows = slice(j * qb, (j + 1) * qb)
        q = q_ref[0, rows, :]
        k_all = _halo_rows(kp_ref, km_ref, kn_ref, j * qb - half, (j + 1) * qb + half)
        v_all = _halo_rows(vp_ref, vm_ref, vn_ref, j * qb - half, (j + 1) * qb + half)
        pk = pk_ref[0, 0, j]
        pq = jnp.sum(jnp.where(col == row + half, pk, 0), axis=-1, keepdims=True)
        key_u = n * step_rows + j * qb - half + col
        mask = in_band & (key_u >= 0) & (key_u < length)
        dist = jnp.where(mask, jnp.abs(pq - pk).astype(F32), MASKED_DISTANCE)

        for p in range(HALF_WIDTH // LANES):
            sl = slice(p * LANES, (p + 1) * LANES)
            qp, kp, vp = q[:, sl], k_all[:, sl], v_all[:, sl]
            zero = jnp.zeros_like(qp)
            s2 = _dot_t(jnp.concatenate([jnp.where(low, qp, zero), jnp.where(low, zero, qp)], axis=0), kp)
            es, stats, scales = [], [], []
            for hh in range(2):
                s = s2[hh * qb:(hh + 1) * qb] + dist * (-LOG2E * _alibi_slope(2 * p + hh))
                if online:
                    m = jnp.max(s, axis=-1, keepdims=True)
                    e = jnp.exp2(s - m)
                    l = jnp.sum(e, axis=-1, keepdims=True)
                    stats.append(m + jnp.log2(l))
                    scales.append(1.0 / l)
                else:
                    e = jnp.exp2(s)
                    stats.append(jnp.sum(e, axis=-1, keepdims=True))
                es.append(e.astype(BF16))
            pv2 = _dot(jnp.concatenate(es, axis=0), vp)
            outs = [pv2[hh * qb:(hh + 1) * qb] * scales[hh] if online else pv2[hh * qb:(hh + 1) * qb]
                    for hh in range(2)]
            o = jnp.where(low, outs[0], outs[1])
            stat = jnp.where(low, stats[0], stats[1])
            if not merge:
                rest[0][0, rows, sl] = o
                rest[1][0, rows, sl] = stat
                continue
            others = [(nat_ref[2 * c, p, rows, :], nat_ref[2 * c + 1, p, rows, :]) for c in range(len(merge_dils))]
            if online:
                top = stat
                for _, lc in others:
                    top = jnp.maximum(top, lc)
                w = jnp.exp2(stat - top)
                num, den = o * w, w
                for oc, lc in others:
                    w = jnp.exp2(lc - top)
                    num, den = num + oc * w, den + w
            else:
                num, den = o, stat
                for oc, lc in others:
                    num, den = num + oc, den + lc
            out_ref[0, rows, sl] = (num / den).astype(out_ref.dtype)


def _strided_positions(positions, dil, qb, half):
    B, S = positions.shape
    L = S // dil
    pos_s = positions.reshape(B, L, dil).transpose(0, 2, 1)
    padded = jnp.pad(pos_s, ((0, 0), (0, 0), (half, qb)))
    nb = L // qb
    parts = [padded[:, :, off:off + L].reshape(B, dil, nb, qb)[..., :w]
             for off, w in ((0, qb), (qb, 2 * half))]
    return jnp.concatenate(parts, axis=-1)[:, :, :, None, :]


def _dilated_config(qd, kd, vd, pos_k, window, dil, online, prev=(), qb=128, n_sub=4):
    B, L, width = qd.shape
    W = width // dil
    half = window // (2 * dil)
    step_rows = n_sub * qb
    assert L % step_rows == 0 and step_rows % half == 0 and half % BF16_ROWS == 0
    halo_per_step = step_rows // half
    last_halo = L // half - 1
    main = pl.BlockSpec((1, step_rows, W), lambda b, r, n: (b, n, r))
    before = pl.BlockSpec((1, half, W), lambda b, r, n: (b, jnp.maximum(n * halo_per_step - 1, 0), r))
    after = pl.BlockSpec((1, half, W), lambda b, r, n: (b, jnp.minimum((n + 1) * halo_per_step, last_halo), r))

    in_specs = [main, before, main, after, before, main, after,
                pl.BlockSpec((1, 1, n_sub, 1, qb + 2 * half), lambda b, r, n: (b, r, n, 0, 0))]
    args = [qd] + [kd] * 3 + [vd] * 3 + [pos_k]
    scratch = []
    if prev:
        assert dil == 1
        for d, o, lse in prev:
            in_specs += [pl.BlockSpec((1, step_rows // d, d * W), lambda b, r, n: (b, n, 0))] * 2
            args += [o, lse]
        n_tiles = W // LANES
        scratch = [pltpu.VMEM((2 * len(prev), n_tiles, step_rows, LANES), F32),
                   pltpu.VMEM((2 * len(prev), n_tiles, RESIDUE_STEP, step_rows // RESIDUE_STEP, LANES), F32)]
        out_specs = main
        out_shape = jax.ShapeDtypeStruct((B, L, W), BF16)
    else:
        out_specs = [main, main]
        out_shape = [jax.ShapeDtypeStruct((B, L, dil * W), F32)] * 2
    return pl.pallas_call(
        functools.partial(_dilated_kernel, qb=qb, half=half, length=L, merge_dils=tuple(d for d, _, _ in prev),
                          online=online),
        grid=(B, dil, L // step_rows),
        in_specs=in_specs, out_specs=out_specs, out_shape=out_shape,
        scratch_shapes=scratch,
        compiler_params=_params(3),
        name=f"dilated_d{dil}" + ("_online" if online else ""),
    )(*args)


def _dilated_attention(qd, kd, vd, pos_views, score_bound, qb):
    def stage(online):
        def run(qd, kd, vd, pos_views):
            prev = []
            for c in range(len(DIL_CONFIGS) - 1, 0, -1):
                window, dil = DIL_CONFIGS[c]
                prev.append((dil, *_dilated_config(qd[c], kd[c], vd[c], pos_views[c], window, dil, online, qb=qb)))
            window, dil = DIL_CONFIGS[0]
            return _dilated_config(qd[0], kd[0], vd[0], pos_views[0], window, dil, online, prev=tuple(prev), qb=qb)
        return run

    return lax.cond(score_bound <= MAX_RAW_SCORE, stage(False), stage(True), qd, kd, vd, pos_views)


def _mla_kernel(q_ref, k_ref, vt_ref, o_ref, *, tq, stabilise):
    def query_tile(i, carry):
        rows = pl.ds(pl.multiple_of(i * tq, tq), tq)
        sts = [_dot_t(k_ref[0, :, hh * LANES:(hh + 1) * LANES], q_ref[0, rows, hh * LANES:(hh + 1) * LANES])
               for hh in range(2)]
        accs = []
        for hh in range(2):
            st = sts[hh]
            if stabilise:
                st = st - jnp.max(st, axis=0, keepdims=True)
            accs.append(_dot(vt_ref[0, 0, hh * VT_ROWS:(hh + 1) * VT_ROWS, :], jnp.exp2(st).astype(BF16)))
        ot = jnp.concatenate([a[:HEAD_DIM] / a[HEAD_DIM:HEAD_DIM + 1] for a in accs], axis=0)
        o_ref[0, rows, :] = ot.T.astype(o_ref.dtype)
        return carry

    lax.fori_loop(0, q_ref.shape[1] // tq, query_tile, 0)


def _latent_attention(qc, kc, vt, score_bound, tq=512, tiles_per_step=4):
    B, S, _ = qc.shape
    pair = 2 * LANES
    rows = tq * tiles_per_step

    def call(stabilise, name):
        return pl.pallas_call(
            functools.partial(_mla_kernel, tq=tq, stabilise=stabilise),
            grid=(B, N_HEADS // 2, S // rows),
            in_specs=[pl.BlockSpec((1, rows, pair), lambda b, p, i: (b, i, p)),
                      pl.BlockSpec((1, S, pair), lambda b, p, i: (b, 0, p)),
                      pl.BlockSpec((1, 1, 2 * VT_ROWS, S), lambda b, p, i: (b, p, 0, 0))],
            out_specs=pl.BlockSpec((1, rows, LANES), lambda b, p, i: (b, i, p)),
            out_shape=jax.ShapeDtypeStruct((B, S, HALF_WIDTH), BF16),
            compiler_params=_params(3),
            name=name,
        )

    return lax.cond(score_bound <= MAX_RAW_SCORE,
                    call(False, "latent_attention"), call(True, "latent_attention_stabilised"),
                    qc, kc, vt)


def _mlp_kernel(x_ref, od_ref, om_ref, mod_ref, wout_ref, g_ref, w1_ref, w2_ref, o_ref, *, tf):
    D = x_ref.shape[-1]
    W = od_ref.shape[-1]
    wout_ref, g_ref, w1_ref, w2_ref = (r.at[0] for r in (wout_ref, g_ref, w1_ref, w2_ref))
    mod = mod_ref.at[0, 0]
    mix = _dot(od_ref[0], wout_ref[0:W, :]) + _dot(om_ref[0], wout_ref[W:2 * W, :])
    x1 = x_ref[0] + mod[2:3, :] * mix
    h = _rms(x1, D) * g_ref[...]
    h = (h * (1.0 + mod[4:5, :]) + mod[3:4, :]).astype(BF16)
    y = jnp.zeros_like(x1)
    for c in range(w1_ref.shape[1] // tf):
        a = jnp.maximum(_dot(h, w1_ref[:, c * tf:(c + 1) * tf]), 0.0)
        y = y + _dot((a * a).astype(BF16), w2_ref[c * tf:(c + 1) * tf, :])
    o_ref[0] = x1 + mod[5:6, :] * y


def _outproj_mlp(x, o_dil, o_mla, mod, p, l, tm=512, tf=512):
    B, S, D = x.shape
    tok = lambda width: pl.BlockSpec((1, tm, width), lambda b, i: (b, i, 0))
    layer_params = [p["w_out"], p["g_norm_mlp"], p["w_mlp_in"], p["w_mlp_out"]]
    return pl.pallas_call(
        functools.partial(_mlp_kernel, tf=tf),
        grid=(B, S // tm),
        in_specs=[tok(D), tok(HALF_WIDTH), tok(HALF_WIDTH), _mod_spec(mod, l)]
        + [_layer_spec(a, l) for a in layer_params],
        out_specs=tok(D),
        out_shape=jax.ShapeDtypeStruct((B, S, D), F32),
        compiler_params=_params(2),
        name="outproj_mlp",
    )(x, o_dil, o_mla, mod, *layer_params)


def _prepare_params(g_norm_mix, w_in, g_q_dil, g_k_dil, g_cq, w_q_up, g_ckv, w_kv_up, g_q_nope,
                    g_q_rope, g_k_nope, g_k_rope, w_out, g_norm_mlp, w_mlp_in, w_mlp_out):
    depth, D = g_norm_mix.shape
    W = HALF_WIDTH
    q_lora, kv_lora = g_cq.shape[-1], g_ckv.shape[-1]
    pad = LANES - HEAD_DIM - ROPE_DIM
    o_kr = 3 * W + q_lora + kv_lora
    lead = ((0, 0),) * 2
    w_kr = jnp.pad(w_in[:, :, o_kr:o_kr + ROPE_DIM], lead + ((HEAD_DIM, pad),))
    wq = w_q_up.reshape(depth, q_lora, N_HEADS, HEAD_DIM + ROPE_DIM)
    swap_halves = lambda a: jnp.concatenate([a[..., ROPE_DIM // 2:], a[..., :ROPE_DIM // 2]], axis=-1)
    wq_swapped = jnp.pad(swap_halves(wq[..., HEAD_DIM:]), lead + ((0, 0), (HEAD_DIM, pad)))
    wq_swapped = wq_swapped.reshape(depth, q_lora, N_HEADS * LANES)
    wq = jnp.pad(wq, lead + ((0, 0), (0, pad))).reshape(depth, q_lora, N_HEADS * LANES)
    lane = jnp.arange(LANES)
    nope_blk = (lane[:, None] < HEAD_DIM) & (lane[None, :] < HEAD_DIM)
    rope_rows = (lane >= HEAD_DIM) & (lane < HEAD_DIM + ROPE_DIM)
    rope_blk = rope_rows[:, None] & (lane[None, :] >= HEAD_DIM)
    tile_avg = nope_blk / HEAD_DIM + rope_blk / ROPE_DIM
    q_group_avg = jnp.kron(jnp.eye(2), tile_avg).astype(BF16)
    wkv = w_kv_up.reshape(depth, kv_lora, N_HEADS, 2 * HEAD_DIM)
    wk = jnp.pad(wkv[..., :HEAD_DIM], lead + ((0, 0), (0, LANES - HEAD_DIM))).reshape(depth, kv_lora, N_HEADS * LANES)
    wvt = jnp.pad(wkv[..., HEAD_DIM:].transpose(0, 2, 3, 1), lead + ((0, VT_ROWS - HEAD_DIM), (0, 0)))
    vt_ones = jnp.zeros((N_HEADS, VT_ROWS, 1), F32).at[:, HEAD_DIM].set(1.0)
    gmax2 = lambda gn, gr: HEAD_DIM * jnp.max(gn ** 2, axis=-1) + ROPE_DIM * jnp.max(gr ** 2, axis=-1)
    score_bound = 1.02 * LOG2E * (HEAD_DIM + ROPE_DIM) ** -0.5 * jnp.sqrt(
        gmax2(g_q_nope, g_q_rope) * gmax2(g_k_nope, g_k_rope))
    dil_score_bound = (1.02 * LOG2E * HEAD_DIM ** 0.5
                       * jnp.max(jnp.abs(g_q_dil), axis=-1) * jnp.max(jnp.abs(g_k_dil), axis=-1))
    rows = {
        "norm_mix": g_norm_mix,
        "q_dil": jnp.tile(g_q_dil, (1, 2)),
        "k_dil": jnp.tile(g_k_dil, (1, 2)),
        "cq": g_cq,
        "ckv": g_ckv,
        "q": jnp.tile(jnp.pad(jnp.concatenate([g_q_nope, g_q_rope], axis=-1), ((0, 0), (0, pad))), (1, 2)),
        "q_swapped": jnp.tile(jnp.pad(swap_halves(g_q_rope), ((0, 0), (HEAD_DIM, pad))), (1, 2)),
        "k": jnp.pad(g_k_nope, ((0, 0), (0, LANES - HEAD_DIM))),
        "k_rope": jnp.pad(g_k_rope, ((0, 0), (HEAD_DIM, pad))),
    }
    gains = jnp.stack([jnp.pad(rows[name].astype(F32), ((0, 0), (0, D - rows[name].shape[-1])))
                       for name in GAIN_ROWS], axis=1)
    return {
        "gains": jnp.pad(gains, ((0, 0), (0, -len(GAIN_ROWS) % 8), (0, 0))),
        "w_in": jnp.concatenate([w_in[:, :, :o_kr], w_kr], axis=2).astype(BF16),
        "w_q_up": wq.astype(BF16),
        "w_q_up_swapped": wq_swapped.astype(BF16),
        "q_group_avg": q_group_avg,
        "w_k_up": wk.astype(BF16),
        "w_vt_up": wvt.reshape(depth, N_HEADS * VT_ROWS, kv_lora).astype(BF16),
        "vt_ones": vt_ones.reshape(N_HEADS * VT_ROWS, 1),
        "score_bound": score_bound,
        "dil_score_bound": dil_score_bound,
        "w_out": w_out.astype(BF16),
        "g_norm_mlp": g_norm_mlp.astype(F32)[:, None, :],
        "w_mlp_in": w_mlp_in.astype(BF16),
        "w_mlp_out": w_mlp_out.astype(BF16),
    }


def kernel(x, c, positions, w_mod, b_mod, g_norm_mix, w_in, g_q_dil, g_k_dil, g_cq, w_q_up, g_ckv, w_kv_up, g_q_nope, g_q_rope, g_k_nope, g_k_rope, w_out, g_norm_mlp, w_mlp_in, w_mlp_out):
    B, S, D = x.shape
    depth = w_mod.shape[0]
    qb = 128
    cos_t, sin_t = _rope_tables(positions)
    mod = _modulation(c, w_mod, b_mod).reshape(depth, B, 6, D)
    pos_views = [_strided_positions(positions, dil, qb, window // (2 * dil)) for window, dil in DIL_CONFIGS]
    p = _prepare_params(g_norm_mix, w_in, g_q_dil, g_k_dil, g_cq, w_q_up, g_ckv, w_kv_up, g_q_nope,
                        g_q_rope, g_k_nope, g_k_rope, w_out, g_norm_mlp, w_mlp_in, w_mlp_out)
    for l in range(depth):
        qd, kd, vd, qc, kc, vt = _inproj(x, mod, cos_t, sin_t, p, l)
        o_dil = _dilated_attention(qd, kd, vd, pos_views, p["dil_score_bound"][l], qb)
        o_mla = _latent_attention(qc, kc, vt, p["score_bound"][l])
        x = _outproj_mlp(x, o_dil, o_mla, mod, p, l)
    return x
```

```python
import functools
import math

import jax
import jax.numpy as jnp
from jax import lax
from jax.experimental import pallas as pl
from jax.experimental.pallas import tpu as pltpu

HEAD_DIM = 64
N_HEADS = 8
HALF_WIDTH = N_HEADS * HEAD_DIM
ROPE_DIM = HEAD_DIM // 2
DIL_CONFIGS = ((128, 1), (512, 4), (2048, 16))
RESIDUE_STEP = 4
ROPE_THETA = 10000.0
EPS = 1e-6
MASKED_DISTANCE = 1e33
LANES = 128
BF16_ROWS = 16
VT_ROWS = HEAD_DIM + BF16_ROWS
LOG2E = math.log2(math.e)
VMEM_LIMIT = 56 * 1024 * 1024
MAX_RAW_SCORE = 60.0

BF16 = jnp.bfloat16
F32 = jnp.float32


def _params(n_axes):
    return pltpu.CompilerParams(
        dimension_semantics=("arbitrary",) * n_axes, vmem_limit_bytes=VMEM_LIMIT)


def _lane_iota(shape):
    return lax.broadcasted_iota(jnp.int32, shape, len(shape) - 1)


def _dot_t(a, b):
    return lax.dot_general(a, b, (((1,), (1,)), ((), ())), preferred_element_type=F32)


def _dot(a, b):
    return jnp.dot(a, b, preferred_element_type=F32)


def _rms(x, width):
    return x * lax.rsqrt(jnp.sum(x * x, axis=-1, keepdims=True) * (1.0 / width) + EPS)


def _swap_rope_halves(y):
    up = pltpu.roll(y, LANES - ROPE_DIM // 2, axis=1)
    down = pltpu.roll(y, ROPE_DIM // 2, axis=1)
    return jnp.where(_lane_iota(y.shape) < HEAD_DIM + ROPE_DIM // 2, up, down)


def _rope_table_kernel(pos_ref, inv_ref, sign_ref, cos_ref, sin_ref):
    per_row = LANES // ROPE_DIM
    rows = pos_ref.shape[1]
    ang = pos_ref[0].astype(F32) * inv_ref[...]
    cos_c = jnp.cos(ang)
    sin_c = jnp.sin(ang) * sign_ref[...]
    lane = _lane_iota(ang.shape)
    rope = (lane >= HEAD_DIM) & (lane < HEAD_DIM + ROPE_DIM)
    for a in range(per_row):
        shift = (HEAD_DIM - ROPE_DIM * a) % LANES
        cos_ref[0, pl.ds(a, rows, stride=per_row), :] = jnp.where(rope, pltpu.roll(cos_c, shift, axis=1), 1.0)
        sin_ref[0, pl.ds(a, rows, stride=per_row), :] = jnp.where(rope, pltpu.roll(sin_c, shift, axis=1), 0.0)


def _rope_tables(positions, ts=1024):
    B, S = positions.shape
    half = ROPE_DIM // 2
    per_row = LANES // ROPE_DIM
    inv = ROPE_THETA ** (-jnp.arange(half, dtype=F32) / half)
    inv_row = jnp.tile(jnp.concatenate([inv, inv]), per_row)[None]
    sign_row = jnp.tile(jnp.concatenate([-jnp.ones((half,), F32), jnp.ones((half,), F32)]), per_row)[None]
    packed_pos = jnp.repeat(positions.reshape(B, S // per_row, per_row), ROPE_DIM, axis=-1)
    out = jax.ShapeDtypeStruct((B, S, LANES), F32)
    return pl.pallas_call(
        _rope_table_kernel,
        grid=(B, S // ts),
        in_specs=[pl.BlockSpec((1, ts // per_row, LANES), lambda b, i: (b, i, 0)),
                  pl.BlockSpec((1, LANES), lambda b, i: (0, 0)),
                  pl.BlockSpec((1, LANES), lambda b, i: (0, 0))],
        out_specs=[pl.BlockSpec((1, ts, LANES), lambda b, i: (b, i, 0))] * 2,
        out_shape=[out, out],
        compiler_params=_params(2),
        name="rope_tables",
    )(packed_pos, inv_row, sign_row)


def _mod_kernel(ct_ref, w_ref, b_ref, o_ref):
    ct = ct_ref[...]
    act = ct * (1.0 / (1.0 + jnp.exp(-ct)))
    w = w_ref[0]
    for b in range(ct.shape[1]):
        o_ref[0, b:b + 1, :] = jnp.sum(act[:, b:b + 1] * w, axis=0, keepdims=True) + b_ref[0]


def _modulation(c, w_mod, b_mod, tn=1536):
    depth, D, n_out = w_mod.shape
    B = c.shape[0]
    return pl.pallas_call(
        _mod_kernel,
        grid=(depth, n_out // tn),
        in_specs=[pl.BlockSpec((D, B), lambda l, j: (0, 0)),
                  pl.BlockSpec((1, D, tn), lambda l, j: (l, 0, j)),
                  pl.BlockSpec((1, 1, tn), lambda l, j: (l, 0, j))],
        out_specs=pl.BlockSpec((1, B, tn), lambda l, j: (l, 0, j)),
        out_shape=jax.ShapeDtypeStruct((depth, B, n_out), F32),
        compiler_params=_params(2),
        name="modulation",
    )(c.T, w_mod, b_mod[:, None, :])


def _head_pair_norm(t):
    sq = t * t
    low = _lane_iota(t.shape) < HEAD_DIM
    tot = jnp.sum(sq, axis=-1, keepdims=True)
    lo = jnp.sum(jnp.where(low, sq, 0.0), axis=-1, keepdims=True)
    ss = jnp.where(low, lo, tot - lo)
    return t * lax.rsqrt(ss * (1.0 / HEAD_DIM) + EPS)


def _store_residue_layouts(tile, p, out_refs, stage_ref, gather_ref):
    natural_ref, by4_ref, by16_ref = out_refs
    rows = tile.shape[0]
    W = HALF_WIDTH
    step = RESIDUE_STEP
    natural_ref[0, :, p * LANES:(p + 1) * LANES] = tile.astype(BF16)
    stage_ref[...] = tile
    for a in range(step):
        part = stage_ref[pl.ds(a, rows // step, stride=step), :]
        by4_ref[0, :, a * W + p * LANES:a * W + (p + 1) * LANES] = part.astype(BF16)
        gather_ref[a] = part
        for b in range(step):
            sub = gather_ref[a, pl.ds(b, rows // step ** 2, stride=step), :]
            col = (step * b + a) * W + p * LANES
            by16_ref[0, :, col:col + LANES] = sub.astype(BF16)


GAIN_ROWS = ("norm_mix", "q_dil", "k_dil", "cq", "ckv", "q", "q_swapped", "k", "k_rope")


def _inproj_kernel(x_ref, mod_ref, gains_ref, win_ref, wq_ref, wqs_ref, wk_ref, wvt_ref, qavg_ref, vone_ref,
                   cos_ref, sin_ref,
                   qd1_ref, qd4_ref, qd16_ref, kd1_ref, kd4_ref, kd16_ref, vd1_ref, vd4_ref, vd16_ref,
                   qc_ref, kc_ref, vt_ref, stage_ref, gather_ref, *, q_scale):
    D = x_ref.shape[-1]
    win_ref, wq_ref, wqs_ref, wk_ref, wvt_ref = (r.at[0] for r in (win_ref, wq_ref, wqs_ref, wk_ref, wvt_ref))
    gain = lambda name, width: gains_ref[0, GAIN_ROWS.index(name):GAIN_ROWS.index(name) + 1, :width]
    x = x_ref[0]
    h = _rms(x, D) * gain("norm_mix", D)
    h = (h * (1.0 + mod_ref[0, 0, 1:2, :]) + mod_ref[0, 0, 0:1, :]).astype(BF16)
    cos = cos_ref[0]
    sin = sin_ref[0]
    W = HALF_WIDTH

    q_lora = wq_ref.shape[0]
    kv_lora = wk_ref.shape[0]
    o_cq = 3 * W
    o_ckv = o_cq + q_lora
    o_kr = o_ckv + kv_lora
    cq = _dot(h, win_ref[:, o_cq:o_ckv])
    ckv_kr = _dot(h, win_ref[:, o_ckv:o_kr + LANES])
    cqn = (_rms(cq, q_lora) * gain("cq", q_lora)).astype(BF16)
    cos2 = jnp.concatenate([cos, cos], axis=1)
    sin2 = jnp.concatenate([sin, sin], axis=1)
    for hp in range(N_HEADS // 2):
        cols = slice(2 * hp * LANES, 2 * (hp + 1) * LANES)
        t = _dot(cqn, wq_ref[:, cols])
        t_swapped = _dot(cqn, wqs_ref[:, cols])
        r = lax.rsqrt(_dot((t * t).astype(BF16), qavg_ref[...]) + EPS)
        y = (t * gain("q", 2 * LANES)) * cos2 + (t_swapped * gain("q_swapped", 2 * LANES)) * sin2
        qc_ref[0, :, cols] = (y * (r * q_scale)).astype(BF16)

    ckvn = (_rms(ckv_kr[:, :kv_lora], kv_lora) * gain("ckv", kv_lora)).astype(BF16)
    vt = _dot_t(wvt_ref[...], ckvn) + vone_ref[...]
    pair_rows = 2 * VT_ROWS
    for p in range(N_HEADS // 2):
        vt_ref[0, p] = vt[p * pair_rows:(p + 1) * pair_rows].astype(BF16)
    kr = ckv_kr[:, kv_lora:]
    kr = _rms(kr, ROPE_DIM) * gain("k_rope", LANES)
    kr = kr * cos + _swap_rope_halves(kr) * sin
    for hh in range(N_HEADS):
        if hh % 2 == 0:
            t2 = _dot(ckvn, wk_ref[:, hh * LANES:(hh + 2) * LANES])
        t = t2[:, (hh % 2) * LANES:(hh % 2 + 1) * LANES]
        kc_ref[0, :, hh * LANES:(hh + 1) * LANES] = (_rms(t, HEAD_DIM) * gain("k", LANES) + kr).astype(BF16)

    n_tiles = W // LANES
    operands = (((qd1_ref, qd4_ref, qd16_ref), "q_dil", LOG2E * HEAD_DIM ** -0.5),
                ((kd1_ref, kd4_ref, kd16_ref), "k_dil", 1.0),
                ((vd1_ref, vd4_ref, vd16_ref), None, None))
    for o, (out_refs, gain_name, scale) in enumerate(operands):
        for p in range(n_tiles):
            if p % 2 == 0:
                t2 = _dot(h, win_ref[:, o * W + p * LANES:o * W + (p + 2) * LANES])
            t = t2[:, (p % 2) * LANES:(p % 2 + 1) * LANES]
            if gain_name is not None:
                t = _head_pair_norm(t) * (gain(gain_name, LANES) * scale)
            _store_residue_layouts(t, p, out_refs, stage_ref.at[o * n_tiles + p], gather_ref.at[o * n_tiles + p])


def _const_spec(shape):
    return pl.BlockSpec(shape, lambda b, i: (0,) * len(shape), pipeline_mode=pl.Buffered(1))


def _layer_spec(stacked, l):
    zeros = (0,) * (stacked.ndim - 1)
    return pl.BlockSpec((1,) + stacked.shape[1:], lambda b, i: (l,) + zeros, pipeline_mode=pl.Buffered(1))


def _mod_spec(mod, l):
    return pl.BlockSpec((1, 1) + mod.shape[2:], lambda b, i: (l, b, 0, 0))


def _inproj(x, mod, cos_t, sin_t, p, l, tm=512):
    B, S, D = x.shape
    W = HALF_WIDTH
    layer_params = [p["gains"], p["w_in"], p["w_q_up"], p["w_q_up_swapped"], p["w_k_up"], p["w_vt_up"]]
    consts = [p["q_group_avg"], p["vt_ones"]]
    assert tuple(d for _, d in DIL_CONFIGS) == (1, RESIDUE_STEP, RESIDUE_STEP ** 2)
    tok = lambda width, rows=tm: pl.BlockSpec((1, rows, width), lambda b, i: (b, i, 0))
    dil_shapes = [(S // d, d * W) for _ in range(3) for _, d in DIL_CONFIGS]
    tok_shapes = dil_shapes + [(S, N_HEADS * LANES)] * 2
    vt_shape = (B, N_HEADS // 2, 2 * VT_ROWS, S)
    q_scale = (HEAD_DIM + ROPE_DIM) ** -0.5 * LOG2E
    n_tiles = 3 * W // LANES
    outs = pl.pallas_call(
        functools.partial(_inproj_kernel, q_scale=q_scale),
        grid=(B, S // tm),
        in_specs=[tok(D), _mod_spec(mod, l)] + [_layer_spec(a, l) for a in layer_params]
        + [_const_spec(a.shape) for a in consts] + [tok(LANES), tok(LANES)],
        out_specs=[tok(width, tm * rows // S) for rows, width in tok_shapes]
        + [pl.BlockSpec((1,) + vt_shape[1:3] + (tm,), lambda b, i: (b, 0, 0, i))],
        out_shape=[jax.ShapeDtypeStruct((B,) + s, BF16) for s in tok_shapes]
        + [jax.ShapeDtypeStruct(vt_shape, BF16)],
        scratch_shapes=[pltpu.VMEM((n_tiles, tm, LANES), F32),
                        pltpu.VMEM((n_tiles, RESIDUE_STEP, tm // RESIDUE_STEP, LANES), F32)],
        compiler_params=_params(2),
        name="inproj",
    )(x, mod, *layer_params, *consts, cos_t, sin_t)
    n_dil = len(DIL_CONFIGS)
    qd, kd, vd = (outs[o * n_dil:(o + 1) * n_dil] for o in range(3))
    return qd, kd, vd, outs[-3], outs[-2], outs[-1]


def _alibi_slope(h):
    return 2.0 ** (-8.0 * (h + 1) / N_HEADS)


def _halo_rows(prev_ref, main_ref, next_ref, lo, hi):
    rows = main_ref.shape[1]
    parts = []
    if lo < 0:
        parts.append(prev_ref[0])
    parts.append(main_ref[0, max(lo, 0):min(hi, rows), :])
    if hi > rows:
        parts.append(next_ref[0])
    return parts[0] if len(parts) == 1 else jnp.concatenate(parts, axis=0)


def _to_token_order(blk_ref, dil, nat_ref, tmp_ref):
    W = HALF_WIDTH
    step = RESIDUE_STEP
    rows = nat_ref.shape[1]
    for p in range(W // LANES):
        for a in range(step):
            if dil == step:
                part = blk_ref[0, :, a * W + p * LANES:a * W + (p + 1) * LANES]
            else:
                for b in range(step):
                    col = (step * b + a) * W + p * LANES
                    tmp_ref[p, a, pl.ds(b, rows // step ** 2, stride=step), :] = blk_ref[0, :, col:col + LANES]
                part = tmp_ref[p, a]
            nat_ref[p, pl.ds(a, rows // step, stride=step), :] = part


def _dilated_kernel(q_ref, kp_ref, km_ref, kn_ref, vp_ref, vm_ref, vn_ref, pk_ref, *rest,
                    qb, half, length, merge_dils, online):
    step_rows = q_ref.shape[1]
    nkeys = qb + 2 * half
    n = pl.program_id(2)
    merge = bool(merge_dils)
    if merge:
        n_other = 2 * len(merge_dils)
        out_ref, nat_ref, tmp_ref = rest[n_other:]
        for c, dil in enumerate(merge_dils):
            for t in range(2):
                _to_token_order(rest[2 * c + t], dil, nat_ref.at[2 * c + t], tmp_ref.at[2 * c + t])
    row = lax.broadcasted_iota(jnp.int32, (qb, nkeys), 0)
    col = lax.broadcasted_iota(jnp.int32, (qb, nkeys), 1)
    in_band = jnp.abs(col - half - row) <= half
    low = _lane_iota((qb, LANES)) < HEAD_DIM

    for j in range(step_rows // qb):
        rows = slice(j * qb, (j + 1) * qb)
        q = q_ref[0, rows, :]
        k_all = _halo_rows(kp_ref, km_ref, kn_ref, j * qb - half, (j + 1) * qb + half)
        v_all = _halo_rows(vp_ref, vm_ref, vn_ref, j * qb - half, (j + 1) * qb + half)
        pk = pk_ref[0, 0, j]
        pq = jnp.sum(jnp.where(col == row + half, pk, 0), axis=-1, keepdims=True)
        key_u = n * step_rows + j * qb - half + col
        mask = in_band & (key_u >= 0) & (key_u < length)
        dist = jnp.where(mask, jnp.abs(pq - pk).astype(F32), MASKED_DISTANCE)

        for p in range(HALF_WIDTH // LANES):
            sl = slice(p * LANES, (p + 1) * LANES)
            qp, kp, vp = q[:, sl], k_all[:, sl], v_all[:, sl]
            zero = jnp.zeros_like(qp)
            s2 = _dot_t(jnp.concatenate([jnp.where(low, qp, zero), jnp.where(low, zero, qp)], axis=0), kp)
            es, stats, scales = [], [], []
            for hh in range(2):
                s = s2[hh * qb:(hh + 1) * qb] + dist * (-LOG2E * _alibi_slope(2 * p + hh))
                if online:
                    m = jnp.max(s, axis=-1, keepdims=True)
                    e = jnp.exp2(s - m)
                    l = jnp.sum(e, axis=-1, keepdims=True)
                    stats.append(m + jnp.log2(l))
                    scales.append(1.0 / l)
                else:
                    e = jnp.exp2(s)
                    stats.append(jnp.sum(e, axis=-1, keepdims=True))
                es.append(e.astype(BF16))
            pv2 = _dot(jnp.concatenate(es, axis=0), vp)
            outs = [pv2[hh * qb:(hh + 1) * qb] * scales[hh] if online else pv2[hh * qb:(hh + 1) * qb]
                    for hh in range(2)]
            o = jnp.where(low, outs[0], outs[1])
            stat = jnp.where(low, stats[0], stats[1])
            if not merge:
                rest[0][0, rows, sl] = o
                rest[1][0, rows, sl] = stat
                continue
            others = [(nat_ref[2 * c, p, rows, :], nat_ref[2 * c + 1, p, rows, :]) for c in range(len(merge_dils))]
            if online:
                top = stat
                for _, lc in others:
                    top = jnp.maximum(top, lc)
                w = jnp.exp2(stat - top)
                num, den = o * w, w
                for oc, lc in others:
                    w = jnp.exp2(lc - top)
                    num, den = num + oc * w, den + w
            else:
                num, den = o, stat
                for oc, lc in others:
                    num, den = num + oc, den + lc
            out_ref[0, rows, sl] = (num / den).astype(out_ref.dtype)


def _strided_positions(positions, dil, qb, half):
    B, S = positions.shape
    L = S // dil
    pos_s = positions.reshape(B, L, dil).transpose(0, 2, 1)
    padded = jnp.pad(pos_s, ((0, 0), (0, 0), (half, qb)))
    nb = L // qb
    parts = [padded[:, :, off:off + L].reshape(B, dil, nb, qb)[..., :w]
             for off, w in ((0, qb), (qb, 2 * half))]
    return jnp.concatenate(parts, axis=-1)[:, :, :, None, :]


def _dilated_config(qd, kd, vd, pos_k, window, dil, online, prev=(), qb=128, max_sub=8):
    B, L, width = qd.shape
    W = width // dil
    half = window // (2 * dil)
    n_sub = min(max_sub, L // qb)
    step_rows = n_sub * qb
    assert L % step_rows == 0 and step_rows % half == 0 and half % BF16_ROWS == 0
    halo_per_step = step_rows // half
    last_halo = L // half - 1
    main = pl.BlockSpec((1, step_rows, W), lambda b, r, n: (b, n, r))
    before = pl.BlockSpec((1, half, W), lambda b, r, n: (b, jnp.maximum(n * halo_per_step - 1, 0), r))
    after = pl.BlockSpec((1, half, W), lambda b, r, n: (b, jnp.minimum((n + 1) * halo_per_step, last_halo), r))

    in_specs = [main, before, main, after, before, main, after,
                pl.BlockSpec((1, 1, n_sub, 1, qb + 2 * half), lambda b, r, n: (b, r, n, 0, 0))]
    args = [qd] + [kd] * 3 + [vd] * 3 + [pos_k]
    scratch = []
    if prev:
        assert dil == 1
        for d, o, lse in prev:
            in_specs += [pl.BlockSpec((1, step_rows // d, d * W), lambda b, r, n: (b, n, 0))] * 2
            args += [o, lse]
        n_tiles = W // LANES
        scratch = [pltpu.VMEM((2 * len(prev), n_tiles, step_rows, LANES), F32),
                   pltpu.VMEM((2 * len(prev), n_tiles, RESIDUE_STEP, step_rows // RESIDUE_STEP, LANES), F32)]
        out_specs = main
        out_shape = jax.ShapeDtypeStruct((B, L, W), BF16)
    else:
        out_specs = [main, main]
        out_shape = [jax.ShapeDtypeStruct((B, L, dil * W), F32)] * 2
    return pl.pallas_call(
        functools.partial(_dilated_kernel, qb=qb, half=half, length=L, merge_dils=tuple(d for d, _, _ in prev),
                          online=online),
        grid=(B, dil, L // step_rows),
        in_specs=in_specs, out_specs=out_specs, out_shape=out_shape,
        scratch_shapes=scratch,
        compiler_params=_params(3),
        name=f"dilated_d{dil}" + ("_online" if online else ""),
    )(*args)


def _dilated_attention(qd, kd, vd, pos_views, score_bound, qb):
    def stage(online):
        def run(qd, kd, vd, pos_views):
            prev = []
            for c in range(len(DIL_CONFIGS) - 1, 0, -1):
                window, dil = DIL_CONFIGS[c]
                prev.append((dil, *_dilated_config(qd[c], kd[c], vd[c], pos_views[c], window, dil, online, qb=qb)))
            window, dil = DIL_CONFIGS[0]
            return _dilated_config(qd[0], kd[0], vd[0], pos_views[0], window, dil, online, prev=tuple(prev), qb=qb)
        return run

    return lax.cond(score_bound <= MAX_RAW_SCORE, stage(False), stage(True), qd, kd, vd, pos_views)


def _mla_kernel(q_ref, k_ref, vt_ref, o_ref, *, tq, stabilise):
    def query_tile(i, carry):
        rows = pl.ds(pl.multiple_of(i * tq, tq), tq)
        sts = [_dot_t(k_ref[0, :, hh * LANES:(hh + 1) * LANES], q_ref[0, rows, hh * LANES:(hh + 1) * LANES])
               for hh in range(2)]
        accs = []
        for hh in range(2):
            st = sts[hh]
            if stabilise:
                st = st - jnp.max(st, axis=0, keepdims=True)
            accs.append(_dot(vt_ref[0, 0, hh * VT_ROWS:(hh + 1) * VT_ROWS, :], jnp.exp2(st).astype(BF16)))
        ot = jnp.concatenate([a[:HEAD_DIM] / a[HEAD_DIM:HEAD_DIM + 1] for a in accs], axis=0)
        o_ref[0, rows, :] = ot.T.astype(o_ref.dtype)
        return carry

    lax.fori_loop(0, q_ref.shape[1] // tq, query_tile, 0)


def _latent_attention(qc, kc, vt, score_bound, tq=512, tiles_per_step=4):
    B, S, _ = qc.shape
    pair = 2 * LANES
    rows = tq * tiles_per_step

    def call(stabilise, name):
        return pl.pallas_call(
            functools.partial(_mla_kernel, tq=tq, stabilise=stabilise),
            grid=(B, N_HEADS // 2, S // rows),
            in_specs=[pl.BlockSpec((1, rows, pair), lambda b, p, i: (b, i, p)),
                      pl.BlockSpec((1, S, pair), lambda b, p, i: (b, 0, p)),
                      pl.BlockSpec((1, 1, 2 * VT_ROWS, S), lambda b, p, i: (b, p, 0, 0))],
            out_specs=pl.BlockSpec((1, rows, LANES), lambda b, p, i: (b, i, p)),
            out_shape=jax.ShapeDtypeStruct((B, S, HALF_WIDTH), BF16),
            compiler_params=_params(3),
            name=name,
        )

    return lax.cond(score_bound <= MAX_RAW_SCORE,
                    call(False, "latent_attention"), call(True, "latent_attention_stabilised"),
                    qc, kc, vt)


def _mlp_kernel(x_ref, od_ref, om_ref, mod_ref, wout_ref, g_ref, w1_ref, w2_ref, o_ref, *, tf):
    D = x_ref.shape[-1]
    W = od_ref.shape[-1]
    wout_ref, g_ref, w1_ref, w2_ref = (r.at[0] for r in (wout_ref, g_ref, w1_ref, w2_ref))
    mod = mod_ref.at[0, 0]
    mix = _dot(od_ref[0], wout_ref[0:W, :]) + _dot(om_ref[0], wout_ref[W:2 * W, :])
    x1 = x_ref[0] + mod[2:3, :] * mix
    h = _rms(x1, D) * g_ref[...]
    h = (h * (1.0 + mod[4:5, :]) + mod[3:4, :]).astype(BF16)
    y = jnp.zeros_like(x1)
    for c in range(w1_ref.shape[1] // tf):
        a = jnp.maximum(_dot(h, w1_ref[:, c * tf:(c + 1) * tf]), 0.0)
        y = y + _dot((a * a).astype(BF16), w2_ref[c * tf:(c + 1) * tf, :])
    o_ref[0] = x1 + mod[5:6, :] * y


def _outproj_mlp(x, o_dil, o_mla, mod, p, l, tm=512, tf=512):
    B, S, D = x.shape
    tok = lambda width: pl.BlockSpec((1, tm, width), lambda b, i: (b, i, 0))
    layer_params = [p["w_out"], p["g_norm_mlp"], p["w_mlp_in"], p["w_mlp_out"]]
    return pl.pallas_call(
        functools.partial(_mlp_kernel, tf=tf),
        grid=(B, S // tm),
        in_specs=[tok(D), tok(HALF_WIDTH), tok(HALF_WIDTH), _mod_spec(mod, l)]
        + [_layer_spec(a, l) for a in layer_params],
        out_specs=tok(D),
        out_shape=jax.ShapeDtypeStruct((B, S, D), F32),
        compiler_params=_params(2),
        name="outproj_mlp",
    )(x, o_dil, o_mla, mod, *layer_params)


def _prepare_params(g_norm_mix, w_in, g_q_dil, g_k_dil, g_cq, w_q_up, g_ckv, w_kv_up, g_q_nope,
                    g_q_rope, g_k_nope, g_k_rope, w_out, g_norm_mlp, w_mlp_in, w_mlp_out):
    depth, D = g_norm_mix.shape
    W = HALF_WIDTH
    q_lora, kv_lora = g_cq.shape[-1], g_ckv.shape[-1]
    pad = LANES - HEAD_DIM - ROPE_DIM
    o_kr = 3 * W + q_lora + kv_lora
    lead = ((0, 0),) * 2
    w_kr = jnp.pad(w_in[:, :, o_kr:o_kr + ROPE_DIM], lead + ((HEAD_DIM, pad),))
    wq = w_q_up.reshape(depth, q_lora, N_HEADS, HEAD_DIM + ROPE_DIM)
    swap_halves = lambda a: jnp.concatenate([a[..., ROPE_DIM // 2:], a[..., :ROPE_DIM // 2]], axis=-1)
    wq_swapped = jnp.pad(swap_halves(wq[..., HEAD_DIM:]), lead + ((0, 0), (HEAD_DIM, pad)))
    wq_swapped = wq_swapped.reshape(depth, q_lora, N_HEADS * LANES)
    wq = jnp.pad(wq, lead + ((0, 0), (0, pad))).reshape(depth, q_lora, N_HEADS * LANES)
    lane = jnp.arange(LANES)
    nope_blk = (lane[:, None] < HEAD_DIM) & (lane[None, :] < HEAD_DIM)
    rope_rows = (lane >= HEAD_DIM) & (lane < HEAD_DIM + ROPE_DIM)
    rope_blk = rope_rows[:, None] & (lane[None, :] >= HEAD_DIM)
    tile_avg = nope_blk / HEAD_DIM + rope_blk / ROPE_DIM
    q_group_avg = jnp.kron(jnp.eye(2), tile_avg).astype(BF16)
    wkv = w_kv_up.reshape(depth, kv_lora, N_HEADS, 2 * HEAD_DIM)
    wk = jnp.pad(wkv[..., :HEAD_DIM], lead + ((0, 0), (0, LANES - HEAD_DIM))).reshape(depth, kv_lora, N_HEADS * LANES)
    wvt = jnp.pad(wkv[..., HEAD_DIM:].transpose(0, 2, 3, 1), lead + ((0, VT_ROWS - HEAD_DIM), (0, 0)))
    vt_ones = jnp.zeros((N_HEADS, VT_ROWS, 1), F32).at[:, HEAD_DIM].set(1.0)
    gmax2 = lambda gn, gr: HEAD_DIM * jnp.max(gn ** 2, axis=-1) + ROPE_DIM * jnp.max(gr ** 2, axis=-1)
    score_bound = 1.02 * LOG2E * (HEAD_DIM + ROPE_DIM) ** -0.5 * jnp.sqrt(
        gmax2(g_q_nope, g_q_rope) * gmax2(g_k_nope, g_k_rope))
    dil_score_bound = (1.02 * LOG2E * HEAD_DIM ** 0.5
                       * jnp.max(jnp.abs(g_q_dil), axis=-1) * jnp.max(jnp.abs(g_k_dil), axis=-1))
    rows = {
        "norm_mix": g_norm_mix,
        "q_dil": jnp.tile(g_q_dil, (1, 2)),
        "k_dil": jnp.tile(g_k_dil, (1, 2)),
        "cq": g_cq,
        "ckv": g_ckv,
        "q": jnp.tile(jnp.pad(jnp.concatenate([g_q_nope, g_q_rope], axis=-1), ((0, 0), (0, pad))), (1, 2)),
        "q_swapped": jnp.tile(jnp.pad(swap_halves(g_q_rope), ((0, 0), (HEAD_DIM, pad))), (1, 2)),
        "k": jnp.pad(g_k_nope, ((0, 0), (0, LANES - HEAD_DIM))),
        "k_rope": jnp.pad(g_k_rope, ((0, 0), (HEAD_DIM, pad))),
    }
    gains = jnp.stack([jnp.pad(rows[name].astype(F32), ((0, 0), (0, D - rows[name].shape[-1])))
                       for name in GAIN_ROWS], axis=1)
    return {
        "gains": jnp.pad(gains, ((0, 0), (0, -len(GAIN_ROWS) % 8), (0, 0))),
        "w_in": jnp.concatenate([w_in[:, :, :o_kr], w_kr], axis=2).astype(BF16),
        "w_q_up": wq.astype(BF16),
        "w_q_up_swapped": wq_swapped.astype(BF16),
        "q_group_avg": q_group_avg,
        "w_k_up": wk.astype(BF16),
        "w_vt_up": wvt.reshape(depth, N_HEADS * VT_ROWS, kv_lora).astype(BF16),
        "vt_ones": vt_ones.reshape(N_HEADS * VT_ROWS, 1),
        "score_bound": score_bound,
        "dil_score_bound": dil_score_bound,
        "w_out": w_out.astype(BF16),
        "g_norm_mlp": g_norm_mlp.astype(F32)[:, None, :],
        "w_mlp_in": w_mlp_in.astype(BF16),
        "w_mlp_out": w_mlp_out.astype(BF16),
    }


def kernel(x, c, positions, w_mod, b_mod, g_norm_mix, w_in, g_q_dil, g_k_dil, g_cq, w_q_up, g_ckv, w_kv_up, g_q_nope, g_q_rope, g_k_nope, g_k_rope, w_out, g_norm_mlp, w_mlp_in, w_mlp_out):
    B, S, D = x.shape
    depth = w_mod.shape[0]
    qb = 128
    cos_t, sin_t = _rope_tables(positions)
    mod = _modulation(c, w_mod, b_mod).reshape(depth, B, 6, D)
    pos_views = [_strided_positions(positions, dil, qb, window // (2 * dil)) for window, dil in DIL_CONFIGS]
    p = _prepare_params(g_norm_mix, w_in, g_q_dil, g_k_dil, g_cq, w_q_up, g_ckv, w_kv_up, g_q_nope,
                        g_q_rope, g_k_nope, g_k_rope, w_out, g_norm_mlp, w_mlp_in, w_mlp_out)
    for l in range(depth):
        qd, kd, vd, qc, kc, vt = _inproj(x, mod, cos_t, sin_t, p, l)
        o_dil = _dilated_attention(qd, kd, vd, pos_views, p["dil_score_bound"][l], qb)
        o_mla = _latent_attention(qc, kc, vt, p["score_bound"][l])
        x = _outproj_mlp(x, o_dil, o_mla, mod, p, l)
    return x
```

```python
import functools
import math

import jax
import jax.numpy as jnp
from jax import lax
from jax.experimental import pallas as pl
from jax.experimental.pallas import tpu as pltpu

HEAD_DIM = 64
N_HEADS = 8
HALF_WIDTH = N_HEADS * HEAD_DIM
ROPE_DIM = HEAD_DIM // 2
DIL_CONFIGS = ((128, 1), (512, 4), (2048, 16))
RESIDUE_STEP = 4
ROPE_THETA = 10000.0
EPS = 1e-6
MASKED_DISTANCE = 1e33
LANES = 128
BF16_ROWS = 16
VT_ROWS = HEAD_DIM + BF16_ROWS
LOG2E = math.log2(math.e)
VMEM_LIMIT = 56 * 1024 * 1024
MAX_RAW_SCORE = 60.0

BF16 = jnp.bfloat16
F32 = jnp.float32


def _params(n_axes):
    return pltpu.CompilerParams(
        dimension_semantics=("arbitrary",) * n_axes, vmem_limit_bytes=VMEM_LIMIT)


def _lane_iota(shape):
    return lax.broadcasted_iota(jnp.int32, shape, len(shape) - 1)


def _dot_t(a, b):
    return lax.dot_general(a, b, (((1,), (1,)), ((), ())), preferred_element_type=F32)


def _dot(a, b):
    return jnp.dot(a, b, preferred_element_type=F32)


def _rms(x, width):
    return x * lax.rsqrt(jnp.sum(x * x, axis=-1, keepdims=True) * (1.0 / width) + EPS)


def _swap_rope_halves(y):
    up = pltpu.roll(y, LANES - ROPE_DIM // 2, axis=1)
    down = pltpu.roll(y, ROPE_DIM // 2, axis=1)
    return jnp.where(_lane_iota(y.shape) < HEAD_DIM + ROPE_DIM // 2, up, down)


def _rope_table_kernel(pos_ref, inv_ref, sign_ref, cos_ref, sin_ref):
    per_row = LANES // ROPE_DIM
    rows = pos_ref.shape[1]
    ang = pos_ref[0].astype(F32) * inv_ref[...]
    cos_c = jnp.cos(ang)
    sin_c = jnp.sin(ang) * sign_ref[...]
    lane = _lane_iota(ang.shape)
    rope = (lane >= HEAD_DIM) & (lane < HEAD_DIM + ROPE_DIM)
    for a in range(per_row):
        shift = (HEAD_DIM - ROPE_DIM * a) % LANES
        cos_ref[0, pl.ds(a, rows, stride=per_row), :] = jnp.where(rope, pltpu.roll(cos_c, shift, axis=1), 1.0)
        sin_ref[0, pl.ds(a, rows, stride=per_row), :] = jnp.where(rope, pltpu.roll(sin_c, shift, axis=1), 0.0)


def _rope_tables(positions, ts=1024):
    B, S = positions.shape
    half = ROPE_DIM // 2
    per_row = LANES // ROPE_DIM
    inv = ROPE_THETA ** (-jnp.arange(half, dtype=F32) / half)
    inv_row = jnp.tile(jnp.concatenate([inv, inv]), per_row)[None]
    sign_row = jnp.tile(jnp.concatenate([-jnp.ones((half,), F32), jnp.ones((half,), F32)]), per_row)[None]
    packed_pos = jnp.repeat(positions.reshape(B, S // per_row, per_row), ROPE_DIM, axis=-1)
    out = jax.ShapeDtypeStruct((B, S, LANES), F32)
    return pl.pallas_call(
        _rope_table_kernel,
        grid=(B, S // ts),
        in_specs=[pl.BlockSpec((1, ts // per_row, LANES), lambda b, i: (b, i, 0)),
                  pl.BlockSpec((1, LANES), lambda b, i: (0, 0)),
                  pl.BlockSpec((1, LANES), lambda b, i: (0, 0))],
        out_specs=[pl.BlockSpec((1, ts, LANES), lambda b, i: (b, i, 0))] * 2,
        out_shape=[out, out],
        compiler_params=_params(2),
        name="rope_tables",
    )(packed_pos, inv_row, sign_row)


def _mod_kernel(ct_ref, w_ref, b_ref, o_ref):
    ct = ct_ref[...]
    act = ct * (1.0 / (1.0 + jnp.exp(-ct)))
    w = w_ref[0]
    for b in range(ct.shape[1]):
        o_ref[0, b:b + 1, :] = jnp.sum(act[:, b:b + 1] * w, axis=0, keepdims=True) + b_ref[0]


def _modulation(c, w_mod, b_mod, tn=1536):
    depth, D, n_out = w_mod.shape
    B = c.shape[0]
    return pl.pallas_call(
        _mod_kernel,
        grid=(depth, n_out // tn),
        in_specs=[pl.BlockSpec((D, B), lambda l, j: (0, 0)),
                  pl.BlockSpec((1, D, tn), lambda l, j: (l, 0, j)),
                  pl.BlockSpec((1, 1, tn), lambda l, j: (l, 0, j))],
        out_specs=pl.BlockSpec((1, B, tn), lambda l, j: (l, 0, j)),
        out_shape=jax.ShapeDtypeStruct((depth, B, n_out), F32),
        compiler_params=_params(2),
        name="modulation",
    )(c.T, w_mod, b_mod[:, None, :])


def _head_pair_norm(t):
    sq = t * t
    low = _lane_iota(t.shape) < HEAD_DIM
    tot = jnp.sum(sq, axis=-1, keepdims=True)
    lo = jnp.sum(jnp.where(low, sq, 0.0), axis=-1, keepdims=True)
    ss = jnp.where(low, lo, tot - lo)
    return t * lax.rsqrt(ss * (1.0 / HEAD_DIM) + EPS)


def _store_residue_layouts(tile, p, out_refs, stage_ref, gather_ref):
    natural_ref, by4_ref, by16_ref = out_refs
    rows = tile.shape[0]
    W = HALF_WIDTH
    step = RESIDUE_STEP
    natural_ref[0, :, p * LANES:(p + 1) * LANES] = tile.astype(BF16)
    stage_ref[...] = tile
    for a in range(step):
        part = stage_ref[pl.ds(a, rows // step, stride=step), :]
        by4_ref[0, :, a * W + p * LANES:a * W + (p + 1) * LANES] = part.astype(BF16)
        gather_ref[a] = part
        for b in range(step):
            sub = gather_ref[a, pl.ds(b, rows // step ** 2, stride=step), :]
            col = (step * b + a) * W + p * LANES
            by16_ref[0, :, col:col + LANES] = sub.astype(BF16)


GAIN_ROWS = ("norm_mix", "q_dil", "k_dil", "cq", "ckv", "q", "q_swapped", "k", "k_rope")


def _inproj_kernel(x_ref, mod_ref, gains_ref, win_ref, wq_ref, wqs_ref, wk_ref, wvt_ref, qavg_ref, vone_ref,
                   cos_ref, sin_ref,
                   qd1_ref, qd4_ref, qd16_ref, kd1_ref, kd4_ref, kd16_ref, vd1_ref, vd4_ref, vd16_ref,
                   qc_ref, kc_ref, vt_ref, stage_ref, gather_ref, *, q_scale):
    D = x_ref.shape[-1]
    win_ref, wq_ref, wqs_ref, wk_ref, wvt_ref = (r.at[0] for r in (win_ref, wq_ref, wqs_ref, wk_ref, wvt_ref))
    gain = lambda name, width: gains_ref[0, GAIN_ROWS.index(name):GAIN_ROWS.index(name) + 1, :width]
    x = x_ref[0]
    h = _rms(x, D) * gain("norm_mix", D)
    h = (h * (1.0 + mod_ref[0, 0, 1:2, :]) + mod_ref[0, 0, 0:1, :]).astype(BF16)
    cos = cos_ref[0]
    sin = sin_ref[0]
    W = HALF_WIDTH

    q_lora = wq_ref.shape[0]
    kv_lora = wk_ref.shape[0]
    o_cq = 3 * W
    o_ckv = o_cq + q_lora
    o_kr = o_ckv + kv_lora
    cq = _dot(h, win_ref[:, o_cq:o_ckv])
    ckv_kr = _dot(h, win_ref[:, o_ckv:o_kr + LANES])
    cqn = (_rms(cq, q_lora) * gain("cq", q_lora)).astype(BF16)
    cos2 = jnp.concatenate([cos, cos], axis=1)
    sin2 = jnp.concatenate([sin, sin], axis=1)
    for hp in range(N_HEADS // 2):
        cols = slice(2 * hp * LANES, 2 * (hp + 1) * LANES)
        t = _dot(cqn, wq_ref[:, cols])
        t_swapped = _dot(cqn, wqs_ref[:, cols])
        r = lax.rsqrt(_dot((t * t).astype(BF16), qavg_ref[...]) + EPS)
        y = (t * gain("q", 2 * LANES)) * cos2 + (t_swapped * gain("q_swapped", 2 * LANES)) * sin2
        qc_ref[0, :, cols] = (y * (r * q_scale)).astype(BF16)

    ckvn = (_rms(ckv_kr[:, :kv_lora], kv_lora) * gain("ckv", kv_lora)).astype(BF16)
    vt = _dot_t(wvt_ref[...], ckvn) + vone_ref[...]
    pair_rows = 2 * VT_ROWS
    for p in range(N_HEADS // 2):
        vt_ref[0, p] = vt[p * pair_rows:(p + 1) * pair_rows].astype(BF16)
    kr = ckv_kr[:, kv_lora:]
    kr = _rms(kr, ROPE_DIM) * gain("k_rope", LANES)
    kr = kr * cos + _swap_rope_halves(kr) * sin
    for hh in range(N_HEADS):
        if hh % 2 == 0:
            t2 = _dot(ckvn, wk_ref[:, hh * LANES:(hh + 2) * LANES])
        t = t2[:, (hh % 2) * LANES:(hh % 2 + 1) * LANES]
        kc_ref[0, :, hh * LANES:(hh + 1) * LANES] = (_rms(t, HEAD_DIM) * gain("k", LANES) + kr).astype(BF16)

    n_tiles = W // LANES
    operands = (((qd1_ref, qd4_ref, qd16_ref), "q_dil", LOG2E * HEAD_DIM ** -0.5),
                ((kd1_ref, kd4_ref, kd16_ref), "k_dil", 1.0),
                ((vd1_ref, vd4_ref, vd16_ref), None, None))
    for o, (out_refs, gain_name, scale) in enumerate(operands):
        for p in range(n_tiles):
            if p % 2 == 0:
                t2 = _dot(h, win_ref[:, o * W + p * LANES:o * W + (p + 2) * LANES])
            t = t2[:, (p % 2) * LANES:(p % 2 + 1) * LANES]
            if gain_name is not None:
                t = _head_pair_norm(t) * (gain(gain_name, LANES) * scale)
            _store_residue_layouts(t, p, out_refs, stage_ref.at[o * n_tiles + p], gather_ref.at[o * n_tiles + p])


def _const_spec(shape):
    return pl.BlockSpec(shape, lambda b, i: (0,) * len(shape), pipeline_mode=pl.Buffered(1))


def _layer_spec(stacked, l):
    zeros = (0,) * (stacked.ndim - 1)
    return pl.BlockSpec((1,) + stacked.shape[1:], lambda b, i: (l,) + zeros, pipeline_mode=pl.Buffered(1))


def _mod_spec(mod, l):
    return pl.BlockSpec((1, 1) + mod.shape[2:], lambda b, i: (l, b, 0, 0))


def _inproj(x, mod, cos_t, sin_t, p, l, tm=512):
    B, S, D = x.shape
    W = HALF_WIDTH
    layer_params = [p["gains"], p["w_in"], p["w_q_up"], p["w_q_up_swapped"], p["w_k_up"], p["w_vt_up"]]
    consts = [p["q_group_avg"], p["vt_ones"]]
    assert tuple(d for _, d in DIL_CONFIGS) == (1, RESIDUE_STEP, RESIDUE_STEP ** 2)
    tok = lambda width, rows=tm: pl.BlockSpec((1, rows, width), lambda b, i: (b, i, 0))
    dil_shapes = [(S // d, d * W) for _ in range(3) for _, d in DIL_CONFIGS]
    tok_shapes = dil_shapes + [(S, N_HEADS * LANES)] * 2
    vt_shape = (B, N_HEADS // 2, 2 * VT_ROWS, S)
    q_scale = (HEAD_DIM + ROPE_DIM) ** -0.5 * LOG2E
    n_tiles = 3 * W // LANES
    outs = pl.pallas_call(
        functools.partial(_inproj_kernel, q_scale=q_scale),
        grid=(B, S // tm),
        in_specs=[tok(D), _mod_spec(mod, l)] + [_layer_spec(a, l) for a in layer_params]
        + [_const_spec(a.shape) for a in consts] + [tok(LANES), tok(LANES)],
        out_specs=[tok(width, tm * rows // S) for rows, width in tok_shapes]
        + [pl.BlockSpec((1,) + vt_shape[1:3] + (tm,), lambda b, i: (b, 0, 0, i))],
        out_shape=[jax.ShapeDtypeStruct((B,) + s, BF16) for s in tok_shapes]
        + [jax.ShapeDtypeStruct(vt_shape, BF16)],
        scratch_shapes=[pltpu.VMEM((n_tiles, tm, LANES), F32),
                        pltpu.VMEM((n_tiles, RESIDUE_STEP, tm // RESIDUE_STEP, LANES), F32)],
        compiler_params=_params(2),
        name="inproj",
    )(x, mod, *layer_params, *consts, cos_t, sin_t)
    n_dil = len(DIL_CONFIGS)
    qd, kd, vd = (outs[o * n_dil:(o + 1) * n_dil] for o in range(3))
    return qd, kd, vd, outs[-3], outs[-2], outs[-1]


def _alibi_slope(h):
    return 2.0 ** (-8.0 * (h + 1) / N_HEADS)


def _halo_rows(prev_ref, main_ref, next_ref, lo, hi):
    rows = main_ref.shape[1]
    parts = []
    if lo < 0:
        parts.append(prev_ref[0])
    parts.append(main_ref[0, max(lo, 0):min(hi, rows), :])
    if hi > rows:
        parts.append(next_ref[0])
    return parts[0] if len(parts) == 1 else jnp.concatenate(parts, axis=0)


def _to_token_order(blk_ref, dil, nat_ref, tmp_ref):
    W = HALF_WIDTH
    step = RESIDUE_STEP
    rows = nat_ref.shape[1]
    for p in range(W // LANES):
        for a in range(step):
            if dil == step:
                part = blk_ref[0, :, a * W + p * LANES:a * W + (p + 1) * LANES]
            else:
                for b in range(step):
                    col = (step * b + a) * W + p * LANES
                    tmp_ref[p, a, pl.ds(b, rows // step ** 2, stride=step), :] = blk_ref[0, :, col:col + LANES]
                part = tmp_ref[p, a]
            nat_ref[p, pl.ds(a, rows // step, stride=step), :] = part


def _dilated_kernel(q_ref, kp_ref, km_ref, kn_ref, vp_ref, vm_ref, vn_ref, pk_ref, *rest,
                    qb, half, length, merge_dils, online):
    step_rows = q_ref.shape[1]
    nkeys = qb + 2 * half
    n = pl.program_id(2)
    merge = bool(merge_dils)
    if merge:
        n_other = 2 * len(merge_dils)
        out_ref, nat_ref, tmp_ref = rest[n_other:]
        for c, dil in enumerate(merge_dils):
            for t in range(2):
                _to_token_order(rest[2 * c + t], dil, nat_ref.at[2 * c + t], tmp_ref.at[2 * c + t])
    row = lax.broadcasted_iota(jnp.int32, (qb, nkeys), 0)
    col = lax.broadcasted_iota(jnp.int32, (qb, nkeys), 1)
    in_band = jnp.abs(col - half - row) <= half
    low = _lane_iota((qb, LANES)) < HEAD_DIM

    n_res = q_ref.shape[2] // HALF_WIDTH
    for res, j in [(res, j) for res in range(n_res) for j in range(step_rows // qb)]:
        rows = slice(j * qb, (j + 1) * qb)
        q = q_ref[0, rows, :]
        k_all = _halo_rows(kp_ref, km_ref, kn_ref, j * qb - half, (j + 1) * qb + half)
        v_all = _halo_rows(vp_ref, vm_ref, vn_ref, j * qb - half, (j + 1) * qb + half)
        pk = pk_ref[0, res, j]
        pq = jnp.sum(jnp.where(col == row + half, pk, 0), axis=-1, keepdims=True)
        key_u = n * step_rows + j * qb - half + col
        mask = in_band & (key_u >= 0) & (key_u < length)
        dist = jnp.where(mask, jnp.abs(pq - pk).astype(F32), MASKED_DISTANCE)

        for p in range(HALF_WIDTH // LANES):
            sl = slice(res * HALF_WIDTH + p * LANES, res * HALF_WIDTH + (p + 1) * LANES)
            qp, kp, vp = q[:, sl], k_all[:, sl], v_all[:, sl]
            zero = jnp.zeros_like(qp)
            s2 = _dot_t(jnp.concatenate([jnp.where(low, qp, zero), jnp.where(low, zero, qp)], axis=0), kp)
            es, stats, scales = [], [], []
            for hh in range(2):
                s = s2[hh * qb:(hh + 1) * qb] + dist * (-LOG2E * _alibi_slope(2 * p + hh))
                if online:
                    m = jnp.max(s, axis=-1, keepdims=True)
                    e = jnp.exp2(s - m)
                    l = jnp.sum(e, axis=-1, keepdims=True)
                    stats.append(m + jnp.log2(l))
                    scales.append(1.0 / l)
                else:
                    e = jnp.exp2(s)
                    stats.append(jnp.sum(e, axis=-1, keepdims=True))
                es.append(e.astype(BF16))
            pv2 = _dot(jnp.concatenate(es, axis=0), vp)
            outs = [pv2[hh * qb:(hh + 1) * qb] * scales[hh] if online else pv2[hh * qb:(hh + 1) * qb]
                    for hh in range(2)]
            o = jnp.where(low, outs[0], outs[1])
            stat = jnp.where(low, stats[0], stats[1])
            if not merge:
                rest[0][0, rows, sl] = o
                rest[1][0, rows, sl] = stat
                continue
            others = [(nat_ref[2 * c, p, rows, :], nat_ref[2 * c + 1, p, rows, :]) for c in range(len(merge_dils))]
            if online:
                top = stat
                for _, lc in others:
                    top = jnp.maximum(top, lc)
                w = jnp.exp2(stat - top)
                num, den = o * w, w
                for oc, lc in others:
                    w = jnp.exp2(lc - top)
                    num, den = num + oc * w, den + w
            else:
                num, den = o, stat
                for oc, lc in others:
                    num, den = num + oc, den + lc
            out_ref[0, rows, sl] = (num / den).astype(out_ref.dtype)


def _strided_positions(positions, dil, qb, half):
    B, S = positions.shape
    L = S // dil
    pos_s = positions.reshape(B, L, dil).transpose(0, 2, 1)
    padded = jnp.pad(pos_s, ((0, 0), (0, 0), (half, qb)))
    nb = L // qb
    parts = [padded[:, :, off:off + L].reshape(B, dil, nb, qb)[..., :w]
             for off, w in ((0, qb), (qb, 2 * half))]
    return jnp.concatenate(parts, axis=-1)[:, :, :, None, :]


def _dilated_config(qd, kd, vd, pos_k, window, dil, online, prev=(), qb=128, units_per_step=8):
    B, L, width = qd.shape
    W = width // dil
    half = window // (2 * dil)
    n_sub = min(units_per_step, L // qb)
    n_res = min(dil, units_per_step // n_sub)
    step_rows = n_sub * qb
    assert L % step_rows == 0 and step_rows % half == 0 and half % BF16_ROWS == 0 and dil % n_res == 0
    halo_per_step = step_rows // half
    last_halo = L // half - 1
    cols = n_res * W
    main = pl.BlockSpec((1, step_rows, cols), lambda b, r, n: (b, n, r))
    before = pl.BlockSpec((1, half, cols), lambda b, r, n: (b, jnp.maximum(n * halo_per_step - 1, 0), r))
    after = pl.BlockSpec((1, half, cols), lambda b, r, n: (b, jnp.minimum((n + 1) * halo_per_step, last_halo), r))

    in_specs = [main, before, main, after, before, main, after,
                pl.BlockSpec((1, n_res, n_sub, 1, qb + 2 * half), lambda b, r, n: (b, r, n, 0, 0))]
    args = [qd] + [kd] * 3 + [vd] * 3 + [pos_k]
    scratch = []
    if prev:
        assert dil == 1
        for d, o, lse in prev:
            in_specs += [pl.BlockSpec((1, step_rows // d, d * W), lambda b, r, n: (b, n, 0))] * 2
            args += [o, lse]
        n_tiles = W // LANES
        scratch = [pltpu.VMEM((2 * len(prev), n_tiles, step_rows, LANES), F32),
                   pltpu.VMEM((2 * len(prev), n_tiles, RESIDUE_STEP, step_rows // RESIDUE_STEP, LANES), F32)]
        out_specs = main
        out_shape = jax.ShapeDtypeStruct((B, L, W), BF16)
    else:
        out_specs = [main, main]
        out_shape = [jax.ShapeDtypeStruct((B, L, dil * W), F32)] * 2
    return pl.pallas_call(
        functools.partial(_dilated_kernel, qb=qb, half=half, length=L, merge_dils=tuple(d for d, _, _ in prev),
                          online=online),
        grid=(B, dil // n_res, L // step_rows),
        in_specs=in_specs, out_specs=out_specs, out_shape=out_shape,
        scratch_shapes=scratch,
        compiler_params=_params(3),
        name=f"dilated_d{dil}" + ("_online" if online else ""),
    )(*args)


def _dilated_attention(qd, kd, vd, pos_views, score_bound, qb):
    def stage(online):
        def run(qd, kd, vd, pos_views):
            prev = []
            for c in range(len(DIL_CONFIGS) - 1, 0, -1):
                window, dil = DIL_CONFIGS[c]
                prev.append((dil, *_dilated_config(qd[c], kd[c], vd[c], pos_views[c], window, dil, online, qb=qb,
                                                   units_per_step=16)))
            window, dil = DIL_CONFIGS[0]
            return _dilated_config(qd[0], kd[0], vd[0], pos_views[0], window, dil, online, prev=tuple(prev), qb=qb)
        return run

    return lax.cond(score_bound <= MAX_RAW_SCORE, stage(False), stage(True), qd, kd, vd, pos_views)


def _mla_kernel(q_ref, k_ref, vt_ref, o_ref, *, tq, stabilise):
    def query_tile(i, carry):
        rows = pl.ds(pl.multiple_of(i * tq, tq), tq)
        sts = [_dot_t(k_ref[0, :, hh * LANES:(hh + 1) * LANES], q_ref[0, rows, hh * LANES:(hh + 1) * LANES])
               for hh in range(2)]
        accs = []
        for hh in range(2):
            st = sts[hh]
            if stabilise:
                st = st - jnp.max(st, axis=0, keepdims=True)
            accs.append(_dot(vt_ref[0, 0, hh * VT_ROWS:(hh + 1) * VT_ROWS, :], jnp.exp2(st).astype(BF16)))
        ot = jnp.concatenate([a[:HEAD_DIM] / a[HEAD_DIM:HEAD_DIM + 1] for a in accs], axis=0)
        o_ref[0, rows, :] = ot.T.astype(o_ref.dtype)
        return carry

    lax.fori_loop(0, q_ref.shape[1] // tq, query_tile, 0)


def _latent_attention(qc, kc, vt, score_bound, tq=512, tiles_per_step=4):
    B, S, _ = qc.shape
    pair = 2 * LANES
    rows = tq * tiles_per_step

    def call(stabilise, name):
        return pl.pallas_call(
            functools.partial(_mla_kernel, tq=tq, stabilise=stabilise),
            grid=(B, N_HEADS // 2, S // rows),
            in_specs=[pl.BlockSpec((1, rows, pair), lambda b, p, i: (b, i, p)),
                      pl.BlockSpec((1, S, pair), lambda b, p, i: (b, 0, p)),
                      pl.BlockSpec((1, 1, 2 * VT_ROWS, S), lambda b, p, i: (b, p, 0, 0))],
            out_specs=pl.BlockSpec((1, rows, LANES), lambda b, p, i: (b, i, p)),
            out_shape=jax.ShapeDtypeStruct((B, S, HALF_WIDTH), BF16),
            compiler_params=_params(3),
            name=name,
        )

    return lax.cond(score_bound <= MAX_RAW_SCORE,
                    call(False, "latent_attention"), call(True, "latent_attention_stabilised"),
                    qc, kc, vt)


def _mlp_kernel(x_ref, od_ref, om_ref, mod_ref, wout_ref, g_ref, w1_ref, w2_ref, o_ref, *, tf):
    D = x_ref.shape[-1]
    W = od_ref.shape[-1]
    wout_ref, g_ref, w1_ref, w2_ref = (r.at[0] for r in (wout_ref, g_ref, w1_ref, w2_ref))
    mod = mod_ref.at[0, 0]
    mix = _dot(od_ref[0], wout_ref[0:W, :]) + _dot(om_ref[0], wout_ref[W:2 * W, :])
    x1 = x_ref[0] + mod[2:3, :] * mix
    h = _rms(x1, D) * g_ref[...]
    h = (h * (1.0 + mod[4:5, :]) + mod[3:4, :]).astype(BF16)
    y = jnp.zeros_like(x1)
    for c in range(w1_ref.shape[1] // tf):
        a = jnp.maximum(_dot(h, w1_ref[:, c * tf:(c + 1) * tf]), 0.0)
        y = y + _dot((a * a).astype(BF16), w2_ref[c * tf:(c + 1) * tf, :])
    o_ref[0] = x1 + mod[5:6, :] * y


def _outproj_mlp(x, o_dil, o_mla, mod, p, l, tm=512, tf=512):
    B, S, D = x.shape
    tok = lambda width: pl.BlockSpec((1, tm, width), lambda b, i: (b, i, 0))
    layer_params = [p["w_out"], p["g_norm_mlp"], p["w_mlp_in"], p["w_mlp_out"]]
    return pl.pallas_call(
        functools.partial(_mlp_kernel, tf=tf),
        grid=(B, S // tm),
        in_specs=[tok(D), tok(HALF_WIDTH), tok(HALF_WIDTH), _mod_spec(mod, l)]
        + [_layer_spec(a, l) for a in layer_params],
        out_specs=tok(D),
        out_shape=jax.ShapeDtypeStruct((B, S, D), F32),
        compiler_params=_params(2),
        name="outproj_mlp",
    )(x, o_dil, o_mla, mod, *layer_params)


def _prepare_params(g_norm_mix, w_in, g_q_dil, g_k_dil, g_cq, w_q_up, g_ckv, w_kv_up, g_q_nope,
                    g_q_rope, g_k_nope, g_k_rope, w_out, g_norm_mlp, w_mlp_in, w_mlp_out):
    depth, D = g_norm_mix.shape
    W = HALF_WIDTH
    q_lora, kv_lora = g_cq.shape[-1], g_ckv.shape[-1]
    pad = LANES - HEAD_DIM - ROPE_DIM
    o_kr = 3 * W + q_lora + kv_lora
    lead = ((0, 0),) * 2
    w_kr = jnp.pad(w_in[:, :, o_kr:o_kr + ROPE_DIM], lead + ((HEAD_DIM, pad),))
    wq = w_q_up.reshape(depth, q_lora, N_HEADS, HEAD_DIM + ROPE_DIM)
    swap_halves = lambda a: jnp.concatenate([a[..., ROPE_DIM // 2:], a[..., :ROPE_DIM // 2]], axis=-1)
    wq_swapped = jnp.pad(swap_halves(wq[..., HEAD_DIM:]), lead + ((0, 0), (HEAD_DIM, pad)))
    wq_swapped = wq_swapped.reshape(depth, q_lora, N_HEADS * LANES)
    wq = jnp.pad(wq, lead + ((0, 0), (0, pad))).reshape(depth, q_lora, N_HEADS * LANES)
    lane = jnp.arange(LANES)
    nope_blk = (lane[:, None] < HEAD_DIM) & (lane[None, :] < HEAD_DIM)
    rope_rows = (lane >= HEAD_DIM) & (lane < HEAD_DIM + ROPE_DIM)
    rope_blk = rope_rows[:, None] & (lane[None, :] >= HEAD_DIM)
    tile_avg = nope_blk / HEAD_DIM + rope_blk / ROPE_DIM
    q_group_avg = jnp.kron(jnp.eye(2), tile_avg).astype(BF16)
    wkv = w_kv_up.reshape(depth, kv_lora, N_HEADS, 2 * HEAD_DIM)
    wk = jnp.pad(wkv[..., :HEAD_DIM], lead + ((0, 0), (0, LANES - HEAD_DIM))).reshape(depth, kv_lora, N_HEADS * LANES)
    wvt = jnp.pad(wkv[..., HEAD_DIM:].transpose(0, 2, 3, 1), lead + ((0, VT_ROWS - HEAD_DIM), (0, 0)))
    vt_ones = jnp.zeros((N_HEADS, VT_ROWS, 1), F32).at[:, HEAD_DIM].set(1.0)
    gmax2 = lambda gn, gr: HEAD_DIM * jnp.max(gn ** 2, axis=-1) + ROPE_DIM * jnp.max(gr ** 2, axis=-1)
    score_bound = 1.02 * LOG2E * (HEAD_DIM + ROPE_DIM) ** -0.5 * jnp.sqrt(
        gmax2(g_q_nope, g_q_rope) * gmax2(g_k_nope, g_k_rope))
    dil_score_bound = (1.02 * LOG2E * HEAD_DIM ** 0.5
                       * jnp.max(jnp.abs(g_q_dil), axis=-1) * jnp.max(jnp.abs(g_k_dil), axis=-1))
    rows = {
        "norm_mix": g_norm_mix,
        "q_dil": jnp.tile(g_q_dil, (1, 2)),
        "k_dil": jnp.tile(g_k_dil, (1, 2)),
        "cq": g_cq,
        "ckv": g_ckv,
        "q": jnp.tile(jnp.pad(jnp.concatenate([g_q_nope, g_q_rope], axis=-1), ((0, 0), (0, pad))), (1, 2)),
        "q_swapped": jnp.tile(jnp.pad(swap_halves(g_q_rope), ((0, 0), (HEAD_DIM, pad))), (1, 2)),
        "k": jnp.pad(g_k_nope, ((0, 0), (0, LANES - HEAD_DIM))),
        "k_rope": jnp.pad(g_k_rope, ((0, 0), (HEAD_DIM, pad))),
    }
    gains = jnp.stack([jnp.pad(rows[name].astype(F32), ((0, 0), (0, D - rows[name].shape[-1])))
                       for name in GAIN_ROWS], axis=1)
    return {
        "gains": jnp.pad(gains, ((0, 0), (0, -len(GAIN_ROWS) % 8), (0, 0))),
        "w_in": jnp.concatenate([w_in[:, :, :o_kr], w_kr], axis=2).astype(BF16),
        "w_q_up": wq.astype(BF16),
        "w_q_up_swapped": wq_swapped.astype(BF16),
        "q_group_avg": q_group_avg,
        "w_k_up": wk.astype(BF16),
        "w_vt_up": wvt.reshape(depth, N_HEADS * VT_ROWS, kv_lora).astype(BF16),
        "vt_ones": vt_ones.reshape(N_HEADS * VT_ROWS, 1),
        "score_bound": score_bound,
        "dil_score_bound": dil_score_bound,
        "w_out": w_out.astype(BF16),
        "g_norm_mlp": g_norm_mlp.astype(F32)[:, None, :],
        "w_mlp_in": w_mlp_in.astype(BF16),
        "w_mlp_out": w_mlp_out.astype(BF16),
    }


def kernel(x, c, positions, w_mod, b_mod, g_norm_mix, w_in, g_q_dil, g_k_dil, g_cq, w_q_up, g_ckv, w_kv_up, g_q_nope, g_q_rope, g_k_nope, g_k_rope, w_out, g_norm_mlp, w_mlp_in, w_mlp_out):
    B, S, D = x.shape
    depth = w_mod.shape[0]
    qb = 128
    cos_t, sin_t = _rope_tables(positions)
    mod = _modulation(c, w_mod, b_mod).reshape(depth, B, 6, D)
    pos_views = [_strided_positions(positions, dil, qb, window // (2 * dil)) for window, dil in DIL_CONFIGS]
    p = _prepare_params(g_norm_mix, w_in, g_q_dil, g_k_dil, g_cq, w_q_up, g_ckv, w_kv_up, g_q_nope,
                        g_q_rope, g_k_nope, g_k_rope, w_out, g_norm_mlp, w_mlp_in, w_mlp_out)
    for l in range(depth):
        qd, kd, vd, qc, kc, vt = _inproj(x, mod, cos_t, sin_t, p, l)
        o_dil = _dilated_attention(qd, kd, vd, pos_views, p["dil_score_bound"][l], qb)
        o_mla = _latent_attention(qc, kc, vt, p["score_bound"][l])
        x = _outproj_mlp(x, o_dil, o_mla, mod, p, l)
    return x
```

```python
import functools
import math

import jax
import jax.numpy as jnp
from jax import lax
from jax.experimental import pallas as pl
from jax.experimental.pallas import tpu as pltpu

HEAD_DIM = 64
N_HEADS = 8
HALF_WIDTH = N_HEADS * HEAD_DIM
ROPE_DIM = HEAD_DIM // 2
DIL_CONFIGS = ((128, 1), (512, 4), (2048, 16))
RESIDUE_STEP = 4
ROPE_THETA = 10000.0
EPS = 1e-6
MASKED_DISTANCE = 1e33
LANES = 128
BF16_ROWS = 16
VT_ROWS = HEAD_DIM + BF16_ROWS
LOG2E = math.log2(math.e)
VMEM_LIMIT = 56 * 1024 * 1024
MAX_RAW_SCORE = 40.0

BF16 = jnp.bfloat16
F32 = jnp.float32


def _params(n_axes):
    return pltpu.CompilerParams(
        dimension_semantics=("arbitrary",) * n_axes, vmem_limit_bytes=VMEM_LIMIT)


def _lane_iota(shape):
    return lax.broadcasted_iota(jnp.int32, shape, len(shape) - 1)


def _dot_t(a, b):
    return lax.dot_general(a, b, (((1,), (1,)), ((), ())), preferred_element_type=F32)


def _dot(a, b):
    return jnp.dot(a, b, preferred_element_type=F32)


def _rms(x, width):
    return x * lax.rsqrt(jnp.sum(x * x, axis=-1, keepdims=True) * (1.0 / width) + EPS)


def _swap_rope_halves(y):
    up = pltpu.roll(y, LANES - ROPE_DIM // 2, axis=1)
    down = pltpu.roll(y, ROPE_DIM // 2, axis=1)
    return jnp.where(_lane_iota(y.shape) < HEAD_DIM + ROPE_DIM // 2, up, down)


def _rope_table_kernel(pos_ref, inv_ref, sign_ref, cos_ref, sin_ref):
    per_row = LANES // ROPE_DIM
    rows = pos_ref.shape[1]
    ang = pos_ref[0].astype(F32) * inv_ref[...]
    cos_c = jnp.cos(ang)
    sin_c = jnp.sin(ang) * sign_ref[...]
    lane = _lane_iota(ang.shape)
    rope = (lane >= HEAD_DIM) & (lane < HEAD_DIM + ROPE_DIM)
    for a in range(per_row):
        shift = (HEAD_DIM - ROPE_DIM * a) % LANES
        cos_ref[0, pl.ds(a, rows, stride=per_row), :] = jnp.where(rope, pltpu.roll(cos_c, shift, axis=1), 1.0)
        sin_ref[0, pl.ds(a, rows, stride=per_row), :] = jnp.where(rope, pltpu.roll(sin_c, shift, axis=1), 0.0)


def _rope_tables(positions, ts=1024):
    B, S = positions.shape
    half = ROPE_DIM // 2
    per_row = LANES // ROPE_DIM
    inv = ROPE_THETA ** (-jnp.arange(half, dtype=F32) / half)
    inv_row = jnp.tile(jnp.concatenate([inv, inv]), per_row)[None]
    sign_row = jnp.tile(jnp.concatenate([-jnp.ones((half,), F32), jnp.ones((half,), F32)]), per_row)[None]
    packed_pos = jnp.repeat(positions.reshape(B, S // per_row, per_row), ROPE_DIM, axis=-1)
    out = jax.ShapeDtypeStruct((B, S, LANES), F32)
    return pl.pallas_call(
        _rope_table_kernel,
        grid=(B, S // ts),
        in_specs=[pl.BlockSpec((1, ts // per_row, LANES), lambda b, i: (b, i, 0)),
                  pl.BlockSpec((1, LANES), lambda b, i: (0, 0)),
                  pl.BlockSpec((1, LANES), lambda b, i: (0, 0))],
        out_specs=[pl.BlockSpec((1, ts, LANES), lambda b, i: (b, i, 0))] * 2,
        out_shape=[out, out],
        compiler_params=_params(2),
        name="rope_tables",
    )(packed_pos, inv_row, sign_row)


def _mod_kernel(ct_ref, w_ref, b_ref, o_ref):
    ct = ct_ref[...]
    act = ct * (1.0 / (1.0 + jnp.exp(-ct)))
    w = w_ref[0]
    for b in range(ct.shape[1]):
        o_ref[0, b:b + 1, :] = jnp.sum(act[:, b:b + 1] * w, axis=0, keepdims=True) + b_ref[0]


def _modulation(c, w_mod, b_mod, tn=1536):
    depth, D, n_out = w_mod.shape
    B = c.shape[0]
    return pl.pallas_call(
        _mod_kernel,
        grid=(depth, n_out // tn),
        in_specs=[pl.BlockSpec((D, B), lambda l, j: (0, 0)),
                  pl.BlockSpec((1, D, tn), lambda l, j: (l, 0, j)),
                  pl.BlockSpec((1, 1, tn), lambda l, j: (l, 0, j))],
        out_specs=pl.BlockSpec((1, B, tn), lambda l, j: (l, 0, j)),
        out_shape=jax.ShapeDtypeStruct((depth, B, n_out), F32),
        compiler_params=_params(2),
        name="modulation",
    )(c.T, w_mod, b_mod[:, None, :])


def _head_pair_norm(t):
    sq = t * t
    low = _lane_iota(t.shape) < HEAD_DIM
    tot = jnp.sum(sq, axis=-1, keepdims=True)
    lo = jnp.sum(jnp.where(low, sq, 0.0), axis=-1, keepdims=True)
    ss = jnp.where(low, lo, tot - lo)
    return t * lax.rsqrt(ss * (1.0 / HEAD_DIM) + EPS)


def _store_residue_layouts(tile, p, out_refs, stage_ref, gather_ref):
    natural_ref, by4_ref, by16_ref = out_refs
    rows = tile.shape[0]
    W = HALF_WIDTH
    step = RESIDUE_STEP
    natural_ref[0, :, p * LANES:(p + 1) * LANES] = tile.astype(BF16)
    stage_ref[...] = tile
    for a in range(step):
        part = stage_ref[pl.ds(a, rows // step, stride=step), :]
        by4_ref[0, :, a * W + p * LANES:a * W + (p + 1) * LANES] = part.astype(BF16)
        gather_ref[a] = part
        for b in range(step):
            sub = gather_ref[a, pl.ds(b, rows // step ** 2, stride=step), :]
            col = (step * b + a) * W + p * LANES
            by16_ref[0, :, col:col + LANES] = sub.astype(BF16)


GAIN_ROWS = ("norm_mix", "q_dil", "k_dil", "cq", "ckv", "q", "q_swapped", "k", "k_rope")


def _inproj_kernel(x_ref, mod_ref, gains_ref, win_ref, wq_ref, wqs_ref, wk_ref, wvt_ref, qavg_ref, vone_ref,
                   cos_ref, sin_ref,
                   qd1_ref, qd4_ref, qd16_ref, kd1_ref, kd4_ref, kd16_ref, vd1_ref, vd4_ref, vd16_ref,
                   qc_ref, kc_ref, vt_ref, stage_ref, gather_ref, *, q_scale):
    D = x_ref.shape[-1]
    win_ref, wq_ref, wqs_ref, wk_ref, wvt_ref = (r.at[0] for r in (win_ref, wq_ref, wqs_ref, wk_ref, wvt_ref))
    gain = lambda name, width: gains_ref[0, GAIN_ROWS.index(name):GAIN_ROWS.index(name) + 1, :width]
    x = x_ref[0]
    h = _rms(x, D) * gain("norm_mix", D)
    h = (h * (1.0 + mod_ref[0, 0, 1:2, :]) + mod_ref[0, 0, 0:1, :]).astype(BF16)
    cos = cos_ref[0]
    sin = sin_ref[0]
    W = HALF_WIDTH

    q_lora = wq_ref.shape[0]
    kv_lora = wk_ref.shape[0]
    o_cq = 3 * W
    o_ckv = o_cq + q_lora
    o_kr = o_ckv + kv_lora
    cq = _dot(h, win_ref[:, o_cq:o_ckv])
    ckv_kr = _dot(h, win_ref[:, o_ckv:o_kr + LANES])
    cqn = (_rms(cq, q_lora) * gain("cq", q_lora)).astype(BF16)
    cos2 = jnp.concatenate([cos, cos], axis=1)
    sin2 = jnp.concatenate([sin, sin], axis=1)
    for hp in range(N_HEADS // 2):
        cols = slice(2 * hp * LANES, 2 * (hp + 1) * LANES)
        t = _dot(cqn, wq_ref[:, cols])
        t_swapped = _dot(cqn, wqs_ref[:, cols])
        r = lax.rsqrt(_dot((t * t).astype(BF16), qavg_ref[...]) + EPS)
        y = (t * gain("q", 2 * LANES)) * cos2 + (t_swapped * gain("q_swapped", 2 * LANES)) * sin2
        qc_ref[0, :, cols] = (y * (r * q_scale)).astype(BF16)

    ckvn = (_rms(ckv_kr[:, :kv_lora], kv_lora) * gain("ckv", kv_lora)).astype(BF16)
    vt = _dot_t(wvt_ref[...], ckvn) + vone_ref[...]
    pair_rows = 2 * VT_ROWS
    for p in range(N_HEADS // 2):
        vt_ref[0, p] = vt[p * pair_rows:(p + 1) * pair_rows].astype(BF16)
    kr = ckv_kr[:, kv_lora:]
    kr = _rms(kr, ROPE_DIM) * gain("k_rope", LANES)
    kr = kr * cos + _swap_rope_halves(kr) * sin
    for hh in range(N_HEADS):
        if hh % 2 == 0:
            t2 = _dot(ckvn, wk_ref[:, hh * LANES:(hh + 2) * LANES])
        t = t2[:, (hh % 2) * LANES:(hh % 2 + 1) * LANES]
        kc_ref[0, :, hh * LANES:(hh + 1) * LANES] = (_rms(t, HEAD_DIM) * gain("k", LANES) + kr).astype(BF16)

    n_tiles = W // LANES
    operands = (((qd1_ref, qd4_ref, qd16_ref), "q_dil", LOG2E * HEAD_DIM ** -0.5),
                ((kd1_ref, kd4_ref, kd16_ref), "k_dil", 1.0),
                ((vd1_ref, vd4_ref, vd16_ref), None, None))
    for o, (out_refs, gain_name, scale) in enumerate(operands):
        for p in range(n_tiles):
            if p % 2 == 0:
                t2 = _dot(h, win_ref[:, o * W + p * LANES:o * W + (p + 2) * LANES])
            t = t2[:, (p % 2) * LANES:(p % 2 + 1) * LANES]
            if gain_name is not None:
                t = _head_pair_norm(t) * (gain(gain_name, LANES) * scale)
            _store_residue_layouts(t, p, out_refs, stage_ref.at[o * n_tiles + p], gather_ref.at[o * n_tiles + p])


def _const_spec(shape):
    return pl.BlockSpec(shape, lambda b, i: (0,) * len(shape), pipeline_mode=pl.Buffered(1))


def _layer_spec(stacked, l):
    zeros = (0,) * (stacked.ndim - 1)
    return pl.BlockSpec((1,) + stacked.shape[1:], lambda b, i: (l,) + zeros, pipeline_mode=pl.Buffered(1))


def _mod_spec(mod, l):
    return pl.BlockSpec((1, 1) + mod.shape[2:], lambda b, i: (l, b, 0, 0))


def _inproj(x, mod, cos_t, sin_t, p, l, tm=512):
    B, S, D = x.shape
    W = HALF_WIDTH
    layer_params = [p["gains"], p["w_in"], p["w_q_up"], p["w_q_up_swapped"], p["w_k_up"], p["w_vt_up"]]
    consts = [p["q_group_avg"], p["vt_ones"]]
    assert tuple(d for _, d in DIL_CONFIGS) == (1, RESIDUE_STEP, RESIDUE_STEP ** 2)
    tok = lambda width, rows=tm: pl.BlockSpec((1, rows, width), lambda b, i: (b, i, 0))
    dil_shapes = [(S // d, d * W) for _ in range(3) for _, d in DIL_CONFIGS]
    tok_shapes = dil_shapes + [(S, N_HEADS * LANES)] * 2
    vt_shape = (B, N_HEADS // 2, 2 * VT_ROWS, S)
    q_scale = (HEAD_DIM + ROPE_DIM) ** -0.5 * LOG2E
    n_tiles = 3 * W // LANES
    outs = pl.pallas_call(
        functools.partial(_inproj_kernel, q_scale=q_scale),
        grid=(B, S // tm),
        in_specs=[tok(D), _mod_spec(mod, l)] + [_layer_spec(a, l) for a in layer_params]
        + [_const_spec(a.shape) for a in consts] + [tok(LANES), tok(LANES)],
        out_specs=[tok(width, tm * rows // S) for rows, width in tok_shapes]
        + [pl.BlockSpec((1,) + vt_shape[1:3] + (tm,), lambda b, i: (b, 0, 0, i))],
        out_shape=[jax.ShapeDtypeStruct((B,) + s, BF16) for s in tok_shapes]
        + [jax.ShapeDtypeStruct(vt_shape, BF16)],
        scratch_shapes=[pltpu.VMEM((n_tiles, tm, LANES), F32),
                        pltpu.VMEM((n_tiles, RESIDUE_STEP, tm // RESIDUE_STEP, LANES), F32)],
        compiler_params=_params(2),
        name="inproj",
    )(x, mod, *layer_params, *consts, cos_t, sin_t)
    n_dil = len(DIL_CONFIGS)
    qd, kd, vd = (outs[o * n_dil:(o + 1) * n_dil] for o in range(3))
    return qd, kd, vd, outs[-3], outs[-2], outs[-1]


def _alibi_slope(h):
    return 2.0 ** (-8.0 * (h + 1) / N_HEADS)


def _halo_rows(prev_ref, main_ref, next_ref, lo, hi):
    rows = main_ref.shape[1]
    parts = []
    if lo < 0:
        parts.append(prev_ref[0])
    parts.append(main_ref[0, max(lo, 0):min(hi, rows), :])
    if hi > rows:
        parts.append(next_ref[0])
    return parts[0] if len(parts) == 1 else jnp.concatenate(parts, axis=0)


def _to_token_order(blk_ref, dil, nat_ref, tmp_ref):
    W = HALF_WIDTH
    step = RESIDUE_STEP
    rows = nat_ref.shape[1]
    for p in range(W // LANES):
        for a in range(step):
            if dil == step:
                part = blk_ref[0, :, a * W + p * LANES:a * W + (p + 1) * LANES]
            else:
                for b in range(step):
                    col = (step * b + a) * W + p * LANES
                    tmp_ref[p, a, pl.ds(b, rows // step ** 2, stride=step), :] = blk_ref[0, :, col:col + LANES]
                part = tmp_ref[p, a]
            nat_ref[p, pl.ds(a, rows // step, stride=step), :] = part


def _dilated_kernel(q_ref, kp_ref, km_ref, kn_ref, vp_ref, vm_ref, vn_ref, pk_ref, *rest,
                    qb, half, length, merge_dils, online):
    step_rows = q_ref.shape[1]
    nkeys = qb + 2 * half
    n = pl.program_id(2)
    merge = bool(merge_dils)
    if merge:
        n_other = 2 * len(merge_dils)
        out_ref, nat_ref, tmp_ref = rest[n_other:]
        for c, dil in enumerate(merge_dils):
            for t in range(2):
                _to_token_order(rest[2 * c + t], dil, nat_ref.at[2 * c + t], tmp_ref.at[2 * c + t])
    row = lax.broadcasted_iota(jnp.int32, (qb, nkeys), 0)
    col = lax.broadcasted_iota(jnp.int32, (qb, nkeys), 1)
    in_band = jnp.abs(col - half - row) <= half
    low = _lane_iota((qb, LANES)) < HEAD_DIM

    n_res = q_ref.shape[2] // HALF_WIDTH
    for res, j in [(res, j) for res in range(n_res) for j in range(step_rows // qb)]:
        rows = slice(j * qb, (j + 1) * qb)
        q = q_ref[0, rows, :]
        k_all = _halo_rows(kp_ref, km_ref, kn_ref, j * qb - half, (j + 1) * qb + half)
        v_all = _halo_rows(vp_ref, vm_ref, vn_ref, j * qb - half, (j + 1) * qb + half)
        pk = pk_ref[0, res, j]
        pq = jnp.sum(jnp.where(col == row + half, pk, 0), axis=-1, keepdims=True)
        key_u = n * step_rows + j * qb - half + col
        mask = in_band & (key_u >= 0) & (key_u < length)
        dist = jnp.where(mask, jnp.abs(pq - pk).astype(F32), MASKED_DISTANCE)

        for p in range(HALF_WIDTH // LANES):
            sl = slice(res * HALF_WIDTH + p * LANES, res * HALF_WIDTH + (p + 1) * LANES)
            qp, kp, vp = q[:, sl], k_all[:, sl], v_all[:, sl]
            zero = jnp.zeros_like(qp)
            s2 = _dot_t(jnp.concatenate([jnp.where(low, qp, zero), jnp.where(low, zero, qp)], axis=0), kp)
            es, stats, scales = [], [], []
            for hh in range(2):
                s = s2[hh * qb:(hh + 1) * qb] + dist * (-LOG2E * _alibi_slope(2 * p + hh))
                if online:
                    m = jnp.max(s, axis=-1, keepdims=True)
                    e = jnp.exp2(s - m)
                    l = jnp.sum(e, axis=-1, keepdims=True)
                    stats.append(m + jnp.log2(l))
                    scales.append(1.0 / l)
                else:
                    e = jnp.exp2(s)
                    stats.append(jnp.sum(e, axis=-1, keepdims=True))
                es.append(e.astype(BF16))
            pv2 = _dot(jnp.concatenate(es, axis=0), vp)
            outs = [pv2[hh * qb:(hh + 1) * qb] * scales[hh] if online else pv2[hh * qb:(hh + 1) * qb]
                    for hh in range(2)]
            o = jnp.where(low, outs[0], outs[1])
            stat = jnp.where(low, stats[0], stats[1])
            if not merge:
                rest[0][0, rows, sl] = o
                rest[1][0, rows, sl] = stat
                continue
            others = [(nat_ref[2 * c, p, rows, :], nat_ref[2 * c + 1, p, rows, :]) for c in range(len(merge_dils))]
            if online:
                top = stat
                for _, lc in others:
                    top = jnp.maximum(top, lc)
                w = jnp.exp2(stat - top)
                num, den = o * w, w
                for oc, lc in others:
                    w = jnp.exp2(lc - top)
                    num, den = num + oc * w, den + w
            else:
                num, den = o, stat
                for oc, lc in others:
                    num, den = num + oc, den + lc
            out_ref[0, rows, sl] = (num / den).astype(out_ref.dtype)


def _strided_positions(positions, dil, qb, half):
    B, S = positions.shape
    L = S // dil
    pos_s = positions.reshape(B, L, dil).transpose(0, 2, 1)
    padded = jnp.pad(pos_s, ((0, 0), (0, 0), (half, qb)))
    nb = L // qb
    parts = [padded[:, :, off:off + L].reshape(B, dil, nb, qb)[..., :w]
             for off, w in ((0, qb), (qb, 2 * half))]
    return jnp.concatenate(parts, axis=-1)[:, :, :, None, :]


def _dilated_config(qd, kd, vd, pos_k, window, dil, online, prev=(), qb=128, units_per_step=8):
    B, L, width = qd.shape
    W = width // dil
    half = window // (2 * dil)
    n_sub = min(units_per_step, L // qb)
    n_res = min(dil, units_per_step // n_sub)
    step_rows = n_sub * qb
    assert L % step_rows == 0 and step_rows % half == 0 and half % BF16_ROWS == 0 and dil % n_res == 0
    halo_per_step = step_rows // half
    last_halo = L // half - 1
    cols = n_res * W
    main = pl.BlockSpec((1, step_rows, cols), lambda b, r, n: (b, n, r))
    before = pl.BlockSpec((1, half, cols), lambda b, r, n: (b, jnp.maximum(n * halo_per_step - 1, 0), r))
    after = pl.BlockSpec((1, half, cols), lambda b, r, n: (b, jnp.minimum((n + 1) * halo_per_step, last_halo), r))

    in_specs = [main, before, main, after, before, main, after,
                pl.BlockSpec((1, n_res, n_sub, 1, qb + 2 * half), lambda b, r, n: (b, r, n, 0, 0))]
    args = [qd] + [kd] * 3 + [vd] * 3 + [pos_k]
    scratch = []
    if prev:
        assert dil == 1
        for d, o, lse in prev:
            in_specs += [pl.BlockSpec((1, step_rows // d, d * W), lambda b, r, n: (b, n, 0))] * 2
            args += [o, lse]
        n_tiles = W // LANES
        scratch = [pltpu.VMEM((2 * len(prev), n_tiles, step_rows, LANES), F32),
                   pltpu.VMEM((2 * len(prev), n_tiles, RESIDUE_STEP, step_rows // RESIDUE_STEP, LANES), F32)]
        out_specs = main
        out_shape = jax.ShapeDtypeStruct((B, L, W), BF16)
    else:
        out_specs = [main, main]
        out_shape = [jax.ShapeDtypeStruct((B, L, dil * W), F32)] * 2
    return pl.pallas_call(
        functools.partial(_dilated_kernel, qb=qb, half=half, length=L, merge_dils=tuple(d for d, _, _ in prev),
                          online=online),
        grid=(B, dil // n_res, L // step_rows),
        in_specs=in_specs, out_specs=out_specs, out_shape=out_shape,
        scratch_shapes=scratch,
        compiler_params=_params(3),
        name=f"dilated_d{dil}" + ("_online" if online else ""),
    )(*args)


def _dilated_attention(qd, kd, vd, pos_views, score_bound, qb):
    def stage(online):
        def run(qd, kd, vd, pos_views):
            prev = []
            for c in range(len(DIL_CONFIGS) - 1, 0, -1):
                window, dil = DIL_CONFIGS[c]
                prev.append((dil, *_dilated_config(qd[c], kd[c], vd[c], pos_views[c], window, dil, online, qb=qb,
                                                   units_per_step=16)))
            window, dil = DIL_CONFIGS[0]
            return _dilated_config(qd[0], kd[0], vd[0], pos_views[0], window, dil, online, prev=tuple(prev), qb=qb)
        return run

    return lax.cond(score_bound <= MAX_RAW_SCORE, stage(False), stage(True), qd, kd, vd, pos_views)


def _mla_kernel(q_ref, k_ref, vt_ref, o_ref, *, tq, stabilise):
    def query_tile(i, carry):
        rows = pl.ds(pl.multiple_of(i * tq, tq), tq)
        sts = [_dot_t(k_ref[0, :, hh * LANES:(hh + 1) * LANES], q_ref[0, rows, hh * LANES:(hh + 1) * LANES])
               for hh in range(2)]
        accs = []
        for hh in range(2):
            st = sts[hh]
            if stabilise:
                st = st - jnp.max(st, axis=0, keepdims=True)
            accs.append(_dot(vt_ref[0, 0, hh * VT_ROWS:(hh + 1) * VT_ROWS, :], jnp.exp2(st).astype(BF16)))
        ot = jnp.concatenate([a[:HEAD_DIM] / a[HEAD_DIM:HEAD_DIM + 1] for a in accs], axis=0)
        o_ref[0, rows, :] = ot.T.astype(o_ref.dtype)
        return carry

    lax.fori_loop(0, q_ref.shape[1] // tq, query_tile, 0)


def _latent_attention(qc, kc, vt, score_bound, tq=512, tiles_per_step=8):
    B, S, _ = qc.shape
    pair = 2 * LANES
    rows = tq * tiles_per_step

    def call(stabilise, name):
        return pl.pallas_call(
            functools.partial(_mla_kernel, tq=tq, stabilise=stabilise),
            grid=(B, N_HEADS // 2, S // rows),
            in_specs=[pl.BlockSpec((1, rows, pair), lambda b, p, i: (b, i, p)),
                      pl.BlockSpec((1, S, pair), lambda b, p, i: (b, 0, p)),
                      pl.BlockSpec((1, 1, 2 * VT_ROWS, S), lambda b, p, i: (b, p, 0, 0))],
            out_specs=pl.BlockSpec((1, rows, LANES), lambda b, p, i: (b, i, p)),
            out_shape=jax.ShapeDtypeStruct((B, S, HALF_WIDTH), BF16),
            compiler_params=_params(3),
            name=name,
        )

    return lax.cond(score_bound <= MAX_RAW_SCORE,
                    call(False, "latent_attention"), call(True, "latent_attention_stabilised"),
                    qc, kc, vt)


def _mlp_kernel(x_ref, od_ref, om_ref, mod_ref, wout_ref, g_ref, w1_ref, w2_ref, o_ref, *, tf):
    D = x_ref.shape[-1]
    W = od_ref.shape[-1]
    wout_ref, g_ref, w1_ref, w2_ref = (r.at[0] for r in (wout_ref, g_ref, w1_ref, w2_ref))
    mod = mod_ref.at[0, 0]
    mix = _dot(od_ref[0], wout_ref[0:W, :]) + _dot(om_ref[0], wout_ref[W:2 * W, :])
    x1 = x_ref[0] + mod[2:3, :] * mix
    h = _rms(x1, D) * g_ref[...]
    h = (h * (1.0 + mod[4:5, :]) + mod[3:4, :]).astype(BF16)
    y = jnp.zeros_like(x1)
    for c in range(w1_ref.shape[1] // tf):
        a = jnp.maximum(_dot(h, w1_ref[:, c * tf:(c + 1) * tf]), 0.0)
        y = y + _dot((a * a).astype(BF16), w2_ref[c * tf:(c + 1) * tf, :])
    o_ref[0] = x1 + mod[5:6, :] * y


def _outproj_mlp(x, o_dil, o_mla, mod, p, l, tm=512, tf=512):
    B, S, D = x.shape
    tok = lambda width: pl.BlockSpec((1, tm, width), lambda b, i: (b, i, 0))
    layer_params = [p["w_out"], p["g_norm_mlp"], p["w_mlp_in"], p["w_mlp_out"]]
    return pl.pallas_call(
        functools.partial(_mlp_kernel, tf=tf),
        grid=(B, S // tm),
        in_specs=[tok(D), tok(HALF_WIDTH), tok(HALF_WIDTH), _mod_spec(mod, l)]
        + [_layer_spec(a, l) for a in layer_params],
        out_specs=tok(D),
        out_shape=jax.ShapeDtypeStruct((B, S, D), F32),
        compiler_params=_params(2),
        name="outproj_mlp",
    )(x, o_dil, o_mla, mod, *layer_params)


def _prepare_params(g_norm_mix, w_in, g_q_dil, g_k_dil, g_cq, w_q_up, g_ckv, w_kv_up, g_q_nope,
                    g_q_rope, g_k_nope, g_k_rope, w_out, g_norm_mlp, w_mlp_in, w_mlp_out):
    depth, D = g_norm_mix.shape
    W = HALF_WIDTH
    q_lora, kv_lora = g_cq.shape[-1], g_ckv.shape[-1]
    pad = LANES - HEAD_DIM - ROPE_DIM
    o_kr = 3 * W + q_lora + kv_lora
    lead = ((0, 0),) * 2
    w_kr = jnp.pad(w_in[:, :, o_kr:o_kr + ROPE_DIM], lead + ((HEAD_DIM, pad),))
    wq = w_q_up.reshape(depth, q_lora, N_HEADS, HEAD_DIM + ROPE_DIM)
    swap_halves = lambda a: jnp.concatenate([a[..., ROPE_DIM // 2:], a[..., :ROPE_DIM // 2]], axis=-1)
    wq_swapped = jnp.pad(swap_halves(wq[..., HEAD_DIM:]), lead + ((0, 0), (HEAD_DIM, pad)))
    wq_swapped = wq_swapped.reshape(depth, q_lora, N_HEADS * LANES)
    wq = jnp.pad(wq, lead + ((0, 0), (0, pad))).reshape(depth, q_lora, N_HEADS * LANES)
    lane = jnp.arange(LANES)
    nope_blk = (lane[:, None] < HEAD_DIM) & (lane[None, :] < HEAD_DIM)
    rope_rows = (lane >= HEAD_DIM) & (lane < HEAD_DIM + ROPE_DIM)
    rope_blk = rope_rows[:, None] & (lane[None, :] >= HEAD_DIM)
    tile_avg = nope_blk / HEAD_DIM + rope_blk / ROPE_DIM
    q_group_avg = jnp.kron(jnp.eye(2), tile_avg).astype(BF16)
    wkv = w_kv_up.reshape(depth, kv_lora, N_HEADS, 2 * HEAD_DIM)
    wk = jnp.pad(wkv[..., :HEAD_DIM], lead + ((0, 0), (0, LANES - HEAD_DIM))).reshape(depth, kv_lora, N_HEADS * LANES)
    wvt = jnp.pad(wkv[..., HEAD_DIM:].transpose(0, 2, 3, 1), lead + ((0, VT_ROWS - HEAD_DIM), (0, 0)))
    vt_ones = jnp.zeros((N_HEADS, VT_ROWS, 1), F32).at[:, HEAD_DIM].set(1.0)
    gmax2 = lambda gn, gr: HEAD_DIM * jnp.max(gn ** 2, axis=-1) + ROPE_DIM * jnp.max(gr ** 2, axis=-1)
    score_bound = 1.02 * LOG2E * (HEAD_DIM + ROPE_DIM) ** -0.5 * jnp.sqrt(
        gmax2(g_q_nope, g_q_rope) * gmax2(g_k_nope, g_k_rope))
    dil_score_bound = (1.02 * LOG2E * HEAD_DIM ** 0.5
                       * jnp.max(jnp.abs(g_q_dil), axis=-1) * jnp.max(jnp.abs(g_k_dil), axis=-1))
    rows = {
        "norm_mix": g_norm_mix,
        "q_dil": jnp.tile(g_q_dil, (1, 2)),
        "k_dil": jnp.tile(g_k_dil, (1, 2)),
        "cq": g_cq,
        "ckv": g_ckv,
        "q": jnp.tile(jnp.pad(jnp.concatenate([g_q_nope, g_q_rope], axis=-1), ((0, 0), (0, pad))), (1, 2)),
        "q_swapped": jnp.tile(jnp.pad(swap_halves(g_q_rope), ((0, 0), (HEAD_DIM, pad))), (1, 2)),
        "k": jnp.pad(g_k_nope, ((0, 0), (0, LANES - HEAD_DIM))),
        "k_rope": jnp.pad(g_k_rope, ((0, 0), (HEAD_DIM, pad))),
    }
    gains = jnp.stack([jnp.pad(rows[name].astype(F32), ((0, 0), (0, D - rows[name].shape[-1])))
                       for name in GAIN_ROWS], axis=1)
    return {
        "gains": jnp.pad(gains, ((0, 0), (0, -len(GAIN_ROWS) % 8), (0, 0))),
        "w_in": jnp.concatenate([w_in[:, :, :o_kr], w_kr], axis=2).astype(BF16),
        "w_q_up": wq.astype(BF16),
        "w_q_up_swapped": wq_swapped.astype(BF16),
        "q_group_avg": q_group_avg,
        "w_k_up": wk.astype(BF16),
        "w_vt_up": wvt.reshape(depth, N_HEADS * VT_ROWS, kv_lora).astype(BF16),
        "vt_ones": vt_ones.reshape(N_HEADS * VT_ROWS, 1),
        "score_bound": score_bound,
        "dil_score_bound": dil_score_bound,
        "w_out": w_out.astype(BF16),
        "g_norm_mlp": g_norm_mlp.astype(F32)[:, None, :],
        "w_mlp_in": w_mlp_in.astype(BF16),
        "w_mlp_out": w_mlp_out.astype(BF16),
    }


def kernel(x, c, positions, w_mod, b_mod, g_norm_mix, w_in, g_q_dil, g_k_dil, g_cq, w_q_up, g_ckv, w_kv_up, g_q_nope, g_q_rope, g_k_nope, g_k_rope, w_out, g_norm_mlp, w_mlp_in, w_mlp_out):
    B, S, D = x.shape
    depth = w_mod.shape[0]
    qb = 128
    cos_t, sin_t = _rope_tables(positions)
    mod = _modulation(c, w_mod, b_mod).reshape(depth, B, 6, D)
    pos_views = [_strided_positions(positions, dil, qb, window // (2 * dil)) for window, dil in DIL_CONFIGS]
    p = _prepare_params(g_norm_mix, w_in, g_q_dil, g_k_dil, g_cq, w_q_up, g_ckv, w_kv_up, g_q_nope,
                        g_q_rope, g_k_nope, g_k_rope, w_out, g_norm_mlp, w_mlp_in, w_mlp_out)
    for l in range(depth):
        qd, kd, vd, qc, kc, vt = _inproj(x, mod, cos_t, sin_t, p, l)
        o_dil = _dilated_attention(qd, kd, vd, pos_views, p["dil_score_bound"][l], qb)
        o_mla = _latent_attention(qc, kc, vt, p["score_bound"][l])
        x = _outproj_mlp(x, o_dil, o_mla, mod, p, l)
    return x
```

```python
import functools
import math

import jax
import jax.numpy as jnp
from jax import lax
from jax.experimental import pallas as pl
from jax.experimental.pallas import tpu as pltpu

HEAD_DIM = 64
N_HEADS = 8
HALF_WIDTH = N_HEADS * HEAD_DIM
ROPE_DIM = HEAD_DIM // 2
DIL_CONFIGS = ((128, 1), (512, 4), (2048, 16))
RESIDUE_STEP = 4
ROPE_THETA = 10000.0
EPS = 1e-6
MASKED_DISTANCE = 1e33
LANES = 128
BF16_ROWS = 16
VT_ROWS = HEAD_DIM + BF16_ROWS
LOG2E = math.log2(math.e)
VMEM_LIMIT = 56 * 1024 * 1024
MAX_RAW_SCORE = 40.0

BF16 = jnp.bfloat16
F32 = jnp.float32


def _params(n_axes):
    return pltpu.CompilerParams(
        dimension_semantics=("arbitrary",) * n_axes, vmem_limit_bytes=VMEM_LIMIT)


def _lane_iota(shape):
    return lax.broadcasted_iota(jnp.int32, shape, len(shape) - 1)


def _dot_t(a, b):
    return lax.dot_general(a, b, (((1,), (1,)), ((), ())), preferred_element_type=F32)


def _dot(a, b):
    return jnp.dot(a, b, preferred_element_type=F32)


def _rms(x, width):
    return x * lax.rsqrt(jnp.sum(x * x, axis=-1, keepdims=True) * (1.0 / width) + EPS)


def _swap_rope_halves(y):
    up = pltpu.roll(y, LANES - ROPE_DIM // 2, axis=1)
    down = pltpu.roll(y, ROPE_DIM // 2, axis=1)
    return jnp.where(_lane_iota(y.shape) < HEAD_DIM + ROPE_DIM // 2, up, down)


def _rope_table_kernel(pos_ref, inv_ref, sign_ref, cos_ref, sin_ref):
    per_row = LANES // ROPE_DIM
    rows = pos_ref.shape[1]
    ang = pos_ref[0].astype(F32) * inv_ref[...]
    cos_c = jnp.cos(ang)
    sin_c = jnp.sin(ang) * sign_ref[...]
    lane = _lane_iota(ang.shape)
    rope = (lane >= HEAD_DIM) & (lane < HEAD_DIM + ROPE_DIM)
    for a in range(per_row):
        shift = (HEAD_DIM - ROPE_DIM * a) % LANES
        cos_ref[0, pl.ds(a, rows, stride=per_row), :] = jnp.where(rope, pltpu.roll(cos_c, shift, axis=1), 1.0)
        sin_ref[0, pl.ds(a, rows, stride=per_row), :] = jnp.where(rope, pltpu.roll(sin_c, shift, axis=1), 0.0)


def _rope_tables(positions, ts=1024):
    B, S = positions.shape
    half = ROPE_DIM // 2
    per_row = LANES // ROPE_DIM
    inv = ROPE_THETA ** (-jnp.arange(half, dtype=F32) / half)
    inv_row = jnp.tile(jnp.concatenate([inv, inv]), per_row)[None]
    sign_row = jnp.tile(jnp.concatenate([-jnp.ones((half,), F32), jnp.ones((half,), F32)]), per_row)[None]
    packed_pos = jnp.repeat(positions.reshape(B, S // per_row, per_row), ROPE_DIM, axis=-1)
    out = jax.ShapeDtypeStruct((B, S, LANES), F32)
    return pl.pallas_call(
        _rope_table_kernel,
        grid=(B, S // ts),
        in_specs=[pl.BlockSpec((1, ts // per_row, LANES), lambda b, i: (b, i, 0)),
                  pl.BlockSpec((1, LANES), lambda b, i: (0, 0)),
                  pl.BlockSpec((1, LANES), lambda b, i: (0, 0))],
        out_specs=[pl.BlockSpec((1, ts, LANES), lambda b, i: (b, i, 0))] * 2,
        out_shape=[out, out],
        compiler_params=_params(2),
        name="rope_tables",
    )(packed_pos, inv_row, sign_row)


def _mod_kernel(ct_ref, w_ref, b_ref, o_ref):
    ct = ct_ref[...]
    act = ct * (1.0 / (1.0 + jnp.exp(-ct)))
    w = w_ref[0]
    for b in range(ct.shape[1]):
        o_ref[0, b:b + 1, :] = jnp.sum(act[:, b:b + 1] * w, axis=0, keepdims=True) + b_ref[0]


def _modulation(c, w_mod, b_mod, tn=1536):
    depth, D, n_out = w_mod.shape
    B = c.shape[0]
    return pl.pallas_call(
        _mod_kernel,
        grid=(depth, n_out // tn),
        in_specs=[pl.BlockSpec((D, B), lambda l, j: (0, 0)),
                  pl.BlockSpec((1, D, tn), lambda l, j: (l, 0, j)),
                  pl.BlockSpec((1, 1, tn), lambda l, j: (l, 0, j))],
        out_specs=pl.BlockSpec((1, B, tn), lambda l, j: (l, 0, j)),
        out_shape=jax.ShapeDtypeStruct((depth, B, n_out), F32),
        compiler_params=_params(2),
        name="modulation",
    )(c.T, w_mod, b_mod[:, None, :])


def _head_pair_norm(t):
    sq = t * t
    low = _lane_iota(t.shape) < HEAD_DIM
    tot = jnp.sum(sq, axis=-1, keepdims=True)
    lo = jnp.sum(jnp.where(low, sq, 0.0), axis=-1, keepdims=True)
    ss = jnp.where(low, lo, tot - lo)
    return t * lax.rsqrt(ss * (1.0 / HEAD_DIM) + EPS)


def _store_residue_layouts(tile, p, out_refs, stage_ref, gather_ref):
    natural_ref, by4_ref, by16_ref = out_refs
    rows = tile.shape[0]
    W = HALF_WIDTH
    step = RESIDUE_STEP
    natural_ref[0, :, p * LANES:(p + 1) * LANES] = tile.astype(BF16)
    stage_ref[...] = tile
    for a in range(step):
        part = stage_ref[pl.ds(a, rows // step, stride=step), :]
        by4_ref[0, :, a * W + p * LANES:a * W + (p + 1) * LANES] = part.astype(BF16)
        gather_ref[a] = part
        for b in range(step):
            sub = gather_ref[a, pl.ds(b, rows // step ** 2, stride=step), :]
            col = (step * b + a) * W + p * LANES
            by16_ref[0, :, col:col + LANES] = sub.astype(BF16)


GAIN_ROWS = ("norm_mix", "q_dil", "k_dil", "cq", "ckv", "q", "q_swapped", "k", "k_rope")


def _inproj_kernel(x_ref, mod_ref, gains_ref, win_ref, wq_ref, wqs_ref, wk_ref, wvt_ref, qavg_ref, vone_ref,
                   cos_ref, sin_ref,
                   qd1_ref, qd4_ref, qd16_ref, kd1_ref, kd4_ref, kd16_ref, vd1_ref, vd4_ref, vd16_ref,
                   qc_ref, kc_ref, vt_ref, stage_ref, gather_ref, *, q_scale):
    D = x_ref.shape[-1]
    win_ref, wq_ref, wqs_ref, wk_ref, wvt_ref = (r.at[0] for r in (win_ref, wq_ref, wqs_ref, wk_ref, wvt_ref))
    gain = lambda name, width: gains_ref[0, GAIN_ROWS.index(name):GAIN_ROWS.index(name) + 1, :width]
    x = x_ref[0]
    h = _rms(x, D) * gain("norm_mix", D)
    h = (h * (1.0 + mod_ref[0, 0, 1:2, :]) + mod_ref[0, 0, 0:1, :]).astype(BF16)
    cos = cos_ref[0]
    sin = sin_ref[0]
    W = HALF_WIDTH

    q_lora = wq_ref.shape[0]
    kv_lora = wk_ref.shape[0]
    o_cq = 3 * W
    o_ckv = o_cq + q_lora
    o_kr = o_ckv + kv_lora
    cq = _dot(h, win_ref[:, o_cq:o_ckv])
    ckv_kr = _dot(h, win_ref[:, o_ckv:o_kr + LANES])
    cqn = (_rms(cq, q_lora) * gain("cq", q_lora)).astype(BF16)
    cos2 = jnp.concatenate([cos, cos], axis=1)
    sin2 = jnp.concatenate([sin, sin], axis=1)
    for hp in range(N_HEADS // 2):
        cols = slice(2 * hp * LANES, 2 * (hp + 1) * LANES)
        t = _dot(cqn, wq_ref[:, cols])
        t_swapped = _dot(cqn, wqs_ref[:, cols])
        r = lax.rsqrt(_dot((t * t).astype(BF16), qavg_ref[...]) + EPS)
        y = (t * gain("q", 2 * LANES)) * cos2 + (t_swapped * gain("q_swapped", 2 * LANES)) * sin2
        qc_ref[0, :, cols] = (y * (r * q_scale)).astype(BF16)

    ckvn = (_rms(ckv_kr[:, :kv_lora], kv_lora) * gain("ckv", kv_lora)).astype(BF16)
    vt = _dot_t(wvt_ref[...], ckvn) + vone_ref[...]
    pair_rows = 2 * VT_ROWS
    for p in range(N_HEADS // 2):
        vt_ref[0, p] = vt[p * pair_rows:(p + 1) * pair_rows].astype(BF16)
    kr = ckv_kr[:, kv_lora:]
    kr = _rms(kr, ROPE_DIM) * gain("k_rope", LANES)
    kr = kr * cos + _swap_rope_halves(kr) * sin
    for hh in range(N_HEADS):
        if hh % 2 == 0:
            t2 = _dot(ckvn, wk_ref[:, hh * LANES:(hh + 2) * LANES])
        t = t2[:, (hh % 2) * LANES:(hh % 2 + 1) * LANES]
        kc_ref[0, :, hh * LANES:(hh + 1) * LANES] = (_rms(t, HEAD_DIM) * gain("k", LANES) + kr).astype(BF16)

    n_tiles = W // LANES
    operands = (((qd1_ref, qd4_ref, qd16_ref), "q_dil", LOG2E * HEAD_DIM ** -0.5),
                ((kd1_ref, kd4_ref, kd16_ref), "k_dil", 1.0),
                ((vd1_ref, vd4_ref, vd16_ref), None, None))
    for o, (out_refs, gain_name, scale) in enumerate(operands):
        for p in range(n_tiles):
            if p % 2 == 0:
                t2 = _dot(h, win_ref[:, o * W + p * LANES:o * W + (p + 2) * LANES])
            t = t2[:, (p % 2) * LANES:(p % 2 + 1) * LANES]
            if gain_name is not None:
                t = _head_pair_norm(t) * (gain(gain_name, LANES) * scale)
            _store_residue_layouts(t, p, out_refs, stage_ref.at[o * n_tiles + p], gather_ref.at[o * n_tiles + p])


def _const_spec(shape):
    return pl.BlockSpec(shape, lambda b, i: (0,) * len(shape), pipeline_mode=pl.Buffered(1))


def _layer_spec(stacked, l):
    zeros = (0,) * (stacked.ndim - 1)
    return pl.BlockSpec((1,) + stacked.shape[1:], lambda b, i: (l,) + zeros, pipeline_mode=pl.Buffered(1))


def _mod_spec(mod, l):
    return pl.BlockSpec((1, 1) + mod.shape[2:], lambda b, i: (l, b, 0, 0))


def _inproj(x, mod, cos_t, sin_t, p, l, tm=512):
    B, S, D = x.shape
    W = HALF_WIDTH
    layer_params = [p["gains"], p["w_in"], p["w_q_up"], p["w_q_up_swapped"], p["w_k_up"], p["w_vt_up"]]
    consts = [p["q_group_avg"], p["vt_ones"]]
    assert tuple(d for _, d in DIL_CONFIGS) == (1, RESIDUE_STEP, RESIDUE_STEP ** 2)
    tok = lambda width, rows=tm: pl.BlockSpec((1, rows, width), lambda b, i: (b, i, 0))
    dil_shapes = [(S // d, d * W) for _ in range(3) for _, d in DIL_CONFIGS]
    tok_shapes = dil_shapes + [(S, N_HEADS * LANES)] * 2
    vt_shape = (B, N_HEADS // 2, 2 * VT_ROWS, S)
    q_scale = (HEAD_DIM + ROPE_DIM) ** -0.5 * LOG2E
    n_tiles = 3 * W // LANES
    outs = pl.pallas_call(
        functools.partial(_inproj_kernel, q_scale=q_scale),
        grid=(B, S // tm),
        in_specs=[tok(D), _mod_spec(mod, l)] + [_layer_spec(a, l) for a in layer_params]
        + [_const_spec(a.shape) for a in consts] + [tok(LANES), tok(LANES)],
        out_specs=[tok(width, tm * rows // S) for rows, width in tok_shapes]
        + [pl.BlockSpec((1,) + vt_shape[1:3] + (tm,), lambda b, i: (b, 0, 0, i))],
        out_shape=[jax.ShapeDtypeStruct((B,) + s, BF16) for s in tok_shapes]
        + [jax.ShapeDtypeStruct(vt_shape, BF16)],
        scratch_shapes=[pltpu.VMEM((n_tiles, tm, LANES), F32),
                        pltpu.VMEM((n_tiles, RESIDUE_STEP, tm // RESIDUE_STEP, LANES), F32)],
        compiler_params=_params(2),
        name="inproj",
    )(x, mod, *layer_params, *consts, cos_t, sin_t)
    n_dil = len(DIL_CONFIGS)
    qd, kd, vd = (outs[o * n_dil:(o + 1) * n_dil] for o in range(3))
    return qd, kd, vd, outs[-3], outs[-2], outs[-1]


def _alibi_slope(h):
    return 2.0 ** (-8.0 * (h + 1) / N_HEADS)


def _halo_rows(prev_ref, main_ref, next_ref, lo, hi):
    rows = main_ref.shape[1]
    parts = []
    if lo < 0:
        parts.append(prev_ref[0])
    parts.append(main_ref[0, max(lo, 0):min(hi, rows), :])
    if hi > rows:
        parts.append(next_ref[0])
    return parts[0] if len(parts) == 1 else jnp.concatenate(parts, axis=0)


def _to_token_order(blk_ref, dil, nat_ref, tmp_ref):
    W = HALF_WIDTH
    step = RESIDUE_STEP
    rows = nat_ref.shape[1]
    for p in range(W // LANES):
        for a in range(step):
            if dil == step:
                part = blk_ref[0, :, a * W + p * LANES:a * W + (p + 1) * LANES]
            else:
                for b in range(step):
                    col = (step * b + a) * W + p * LANES
                    tmp_ref[p, a, pl.ds(b, rows // step ** 2, stride=step), :] = blk_ref[0, :, col:col + LANES]
                part = tmp_ref[p, a]
            nat_ref[p, pl.ds(a, rows // step, stride=step), :] = part


def _dilated_kernel(q_ref, kp_ref, km_ref, kn_ref, vp_ref, vm_ref, vn_ref, pk_ref, *rest,
                    qb, half, length, merge_dils, online):
    step_rows = q_ref.shape[1]
    nkeys = qb + 2 * half
    n = pl.program_id(2)
    merge = bool(merge_dils)
    if merge:
        n_other = 2 * len(merge_dils)
        out_ref, nat_ref, tmp_ref = rest[n_other:]
        for c, dil in enumerate(merge_dils):
            for t in range(2):
                _to_token_order(rest[2 * c + t], dil, nat_ref.at[2 * c + t], tmp_ref.at[2 * c + t])
    row = lax.broadcasted_iota(jnp.int32, (qb, nkeys), 0)
    col = lax.broadcasted_iota(jnp.int32, (qb, nkeys), 1)
    in_band = jnp.abs(col - half - row) <= half
    low = _lane_iota((qb, LANES)) < HEAD_DIM

    n_res = q_ref.shape[2] // HALF_WIDTH
    for res, j in [(res, j) for res in range(n_res) for j in range(step_rows // qb)]:
        rows = slice(j * qb, (j + 1) * qb)
        q = q_ref[0, rows, :]
        k_all = _halo_rows(kp_ref, km_ref, kn_ref, j * qb - half, (j + 1) * qb + half)
        v_all = _halo_rows(vp_ref, vm_ref, vn_ref, j * qb - half, (j + 1) * qb + half)
        pk = pk_ref[0, res, j]
        pq = jnp.sum(jnp.where(col == row + half, pk, 0), axis=-1, keepdims=True)
        key_u = n * step_rows + j * qb - half + col
        mask = in_band & (key_u >= 0) & (key_u < length)
        dist = jnp.where(mask, jnp.abs(pq - pk).astype(F32), MASKED_DISTANCE)

        for p in range(HALF_WIDTH // LANES):
            sl = slice(res * HALF_WIDTH + p * LANES, res * HALF_WIDTH + (p + 1) * LANES)
            qp, kp, vp = q[:, sl], k_all[:, sl], v_all[:, sl]
            zero = jnp.zeros_like(qp)
            s2 = _dot_t(jnp.concatenate([jnp.where(low, qp, zero), jnp.where(low, zero, qp)], axis=0), kp)
            es, stats, scales = [], [], []
            for hh in range(2):
                s = s2[hh * qb:(hh + 1) * qb] + dist * (-LOG2E * _alibi_slope(2 * p + hh))
                if online:
                    m = jnp.max(s, axis=-1, keepdims=True)
                    e = jnp.exp2(s - m)
                    l = jnp.sum(e, axis=-1, keepdims=True)
                    stats.append(m + jnp.log2(l))
                    scales.append(1.0 / l)
                else:
                    e = jnp.exp2(s)
                    stats.append(jnp.sum(e, axis=-1, keepdims=True))
                es.append(e.astype(BF16))
            pv2 = _dot(jnp.concatenate(es, axis=0), vp)
            outs = [pv2[hh * qb:(hh + 1) * qb] * scales[hh] if online else pv2[hh * qb:(hh + 1) * qb]
                    for hh in range(2)]
            o = jnp.where(low, outs[0], outs[1])
            stat = jnp.where(low, stats[0], stats[1])
            if not merge:
                rest[0][0, rows, sl] = o
                rest[1][0, rows, sl] = stat
                continue
            others = [(nat_ref[2 * c, p, rows, :], nat_ref[2 * c + 1, p, rows, :]) for c in range(len(merge_dils))]
            if online:
                top = stat
                for _, lc in others:
                    top = jnp.maximum(top, lc)
                w = jnp.exp2(stat - top)
                num, den = o * w, w
                for oc, lc in others:
                    w = jnp.exp2(lc - top)
                    num, den = num + oc * w, den + w
            else:
                num, den = o, stat
                for oc, lc in others:
                    num, den = num + oc, den + lc
            out_ref[0, rows, sl] = (num / den).astype(out_ref.dtype)


def _strided_positions(positions, dil, qb, half):
    B, S = positions.shape
    L = S // dil
    pos_s = positions.reshape(B, L, dil).transpose(0, 2, 1)
    padded = jnp.pad(pos_s, ((0, 0), (0, 0), (half, qb)))
    nb = L // qb
    parts = [padded[:, :, off:off + L].reshape(B, dil, nb, qb)[..., :w]
             for off, w in ((0, qb), (qb, 2 * half))]
    return jnp.concatenate(parts, axis=-1)[:, :, :, None, :]


def _dilated_config(qd, kd, vd, pos_k, window, dil, online, prev=(), qb=128, units_per_step=8):
    B, L, width = qd.shape
    W = width // dil
    half = window // (2 * dil)
    n_sub = min(units_per_step, L // qb)
    n_res = min(dil, units_per_step // n_sub)
    step_rows = n_sub * qb
    assert L % step_rows == 0 and step_rows % half == 0 and half % BF16_ROWS == 0 and dil % n_res == 0
    halo_per_step = step_rows // half
    last_halo = L // half - 1
    cols = n_res * W
    main = pl.BlockSpec((1, step_rows, cols), lambda b, r, n: (b, n, r))
    before = pl.BlockSpec((1, half, cols), lambda b, r, n: (b, jnp.maximum(n * halo_per_step - 1, 0), r))
    after = pl.BlockSpec((1, half, cols), lambda b, r, n: (b, jnp.minimum((n + 1) * halo_per_step, last_halo), r))

    in_specs = [main, before, main, after, before, main, after,
                pl.BlockSpec((1, n_res, n_sub, 1, qb + 2 * half), lambda b, r, n: (b, r, n, 0, 0))]
    args = [qd] + [kd] * 3 + [vd] * 3 + [pos_k]
    scratch = []
    if prev:
        assert dil == 1
        for d, o, lse in prev:
            in_specs += [pl.BlockSpec((1, step_rows // d, d * W), lambda b, r, n: (b, n, 0))] * 2
            args += [o, lse]
        n_tiles = W // LANES
        scratch = [pltpu.VMEM((2 * len(prev), n_tiles, step_rows, LANES), F32),
                   pltpu.VMEM((2 * len(prev), n_tiles, RESIDUE_STEP, step_rows // RESIDUE_STEP, LANES), F32)]
        out_specs = main
        out_shape = jax.ShapeDtypeStruct((B, L, W), BF16)
    else:
        out_specs = [main, main]
        out_shape = [jax.ShapeDtypeStruct((B, L, dil * W), F32)] * 2
    return pl.pallas_call(
        functools.partial(_dilated_kernel, qb=qb, half=half, length=L, merge_dils=tuple(d for d, _, _ in prev),
                          online=online),
        grid=(B, dil // n_res, L // step_rows),
        in_specs=in_specs, out_specs=out_specs, out_shape=out_shape,
        scratch_shapes=scratch,
        compiler_params=_params(3),
        name=f"dilated_d{dil}" + ("_online" if online else ""),
    )(*args)


def _dilated_attention(qd, kd, vd, pos_views, score_bound, qb):
    def stage(online):
        def run(qd, kd, vd, pos_views):
            prev = []
            for c in range(len(DIL_CONFIGS) - 1, 0, -1):
                window, dil = DIL_CONFIGS[c]
                prev.append((dil, *_dilated_config(qd[c], kd[c], vd[c], pos_views[c], window, dil, online, qb=qb,
                                                   units_per_step=16)))
            window, dil = DIL_CONFIGS[0]
            return _dilated_config(qd[0], kd[0], vd[0], pos_views[0], window, dil, online, prev=tuple(prev), qb=qb)
        return run

    return lax.cond(score_bound <= MAX_RAW_SCORE, stage(False), stage(True), qd, kd, vd, pos_views)


def _mla_kernel(q_ref, k_ref, vt_ref, o_ref, *, tq, stabilise):
    def query_tile(i, carry):
        rows = pl.ds(pl.multiple_of(i * tq, tq), tq)
        sts = [_dot_t(k_ref[0, :, hh * LANES:(hh + 1) * LANES], q_ref[0, rows, hh * LANES:(hh + 1) * LANES])
               for hh in range(2)]
        accs = []
        for hh in range(2):
            st = sts[hh]
            if stabilise:
                st = st - jnp.max(st, axis=0, keepdims=True)
            accs.append(_dot(vt_ref[0, 0, hh * VT_ROWS:(hh + 1) * VT_ROWS, :], jnp.exp2(st).astype(BF16)))
        ot = jnp.concatenate([a[:HEAD_DIM] / a[HEAD_DIM:HEAD_DIM + 1] for a in accs], axis=0)
        o_ref[0, rows, :] = ot.T.astype(o_ref.dtype)
        return carry

    lax.fori_loop(0, q_ref.shape[1] // tq, query_tile, 0)


def _latent_attention(qc, kc, vt, score_bound, tq=512, tiles_per_step=8):
    B, S, _ = qc.shape
    pair = 2 * LANES
    rows = tq * tiles_per_step

    def call(stabilise, name):
        return pl.pallas_call(
            functools.partial(_mla_kernel, tq=tq, stabilise=stabilise),
            grid=(B, N_HEADS // 2, S // rows),
            in_specs=[pl.BlockSpec((1, rows, pair), lambda b, p, i: (b, i, p)),
                      pl.BlockSpec((1, S, pair), lambda b, p, i: (b, 0, p)),
                      pl.BlockSpec((1, 1, 2 * VT_ROWS, S), lambda b, p, i: (b, p, 0, 0))],
            out_specs=pl.BlockSpec((1, rows, LANES), lambda b, p, i: (b, i, p)),
            out_shape=jax.ShapeDtypeStruct((B, S, HALF_WIDTH), BF16),
            compiler_params=_params(3),
            name=name,
        )

    return lax.cond(score_bound <= MAX_RAW_SCORE,
                    call(False, "latent_attention"), call(True, "latent_attention_stabilised"),
                    qc, kc, vt)


def _mlp_kernel(x_ref, od_ref, om_ref, mod_ref, wout_ref, g_ref, w1_ref, w2_ref, o_ref, *, tf):
    D = x_ref.shape[-1]
    W = od_ref.shape[-1]
    wout_ref, g_ref, w1_ref, w2_ref = (r.at[0] for r in (wout_ref, g_ref, w1_ref, w2_ref))
    mod = mod_ref.at[0, 0]
    mix = _dot(od_ref[0], wout_ref[0:W, :]) + _dot(om_ref[0], wout_ref[W:2 * W, :])
    x1 = x_ref[0] + mod[2:3, :] * mix
    h = _rms(x1, D) * g_ref[...]
    h = (h * (1.0 + mod[4:5, :]) + mod[3:4, :]).astype(BF16)
    y = jnp.zeros_like(x1)
    for c in range(w1_ref.shape[1] // tf):
        a = jnp.maximum(_dot(h, w1_ref[:, c * tf:(c + 1) * tf]), 0.0)
        y = y + _dot((a * a).astype(BF16), w2_ref[c * tf:(c + 1) * tf, :])
    o_ref[0] = x1 + mod[5:6, :] * y


def _outproj_mlp(x, o_dil, o_mla, mod, p, l, tm=1024, tf=512):
    B, S, D = x.shape
    tok = lambda width: pl.BlockSpec((1, tm, width), lambda b, i: (b, i, 0))
    layer_params = [p["w_out"], p["g_norm_mlp"], p["w_mlp_in"], p["w_mlp_out"]]
    return pl.pallas_call(
        functools.partial(_mlp_kernel, tf=tf),
        grid=(B, S // tm),
        in_specs=[tok(D), tok(HALF_WIDTH), tok(HALF_WIDTH), _mod_spec(mod, l)]
        + [_layer_spec(a, l) for a in layer_params],
        out_specs=tok(D),
        out_shape=jax.ShapeDtypeStruct((B, S, D), F32),
        compiler_params=_params(2),
        name="outproj_mlp",
    )(x, o_dil, o_mla, mod, *layer_params)


def _prepare_params(g_norm_mix, w_in, g_q_dil, g_k_dil, g_cq, w_q_up, g_ckv, w_kv_up, g_q_nope,
                    g_q_rope, g_k_nope, g_k_rope, w_out, g_norm_mlp, w_mlp_in, w_mlp_out):
    depth, D = g_norm_mix.shape
    W = HALF_WIDTH
    q_lora, kv_lora = g_cq.shape[-1], g_ckv.shape[-1]
    pad = LANES - HEAD_DIM - ROPE_DIM
    o_kr = 3 * W + q_lora + kv_lora
    lead = ((0, 0),) * 2
    w_kr = jnp.pad(w_in[:, :, o_kr:o_kr + ROPE_DIM], lead + ((HEAD_DIM, pad),))
    wq = w_q_up.reshape(depth, q_lora, N_HEADS, HEAD_DIM + ROPE_DIM)
    swap_halves = lambda a: jnp.concatenate([a[..., ROPE_DIM // 2:], a[..., :ROPE_DIM // 2]], axis=-1)
    wq_swapped = jnp.pad(swap_halves(wq[..., HEAD_DIM:]), lead + ((0, 0), (HEAD_DIM, pad)))
    wq_swapped = wq_swapped.reshape(depth, q_lora, N_HEADS * LANES)
    wq = jnp.pad(wq, lead + ((0, 0), (0, pad))).reshape(depth, q_lora, N_HEADS * LANES)
    lane = jnp.arange(LANES)
    nope_blk = (lane[:, None] < HEAD_DIM) & (lane[None, :] < HEAD_DIM)
    rope_rows = (lane >= HEAD_DIM) & (lane < HEAD_DIM + ROPE_DIM)
    rope_blk = rope_rows[:, None] & (lane[None, :] >= HEAD_DIM)
    tile_avg = nope_blk / HEAD_DIM + rope_blk / ROPE_DIM
    q_group_avg = jnp.kron(jnp.eye(2), tile_avg).astype(BF16)
    wkv = w_kv_up.reshape(depth, kv_lora, N_HEADS, 2 * HEAD_DIM)
    wk = jnp.pad(wkv[..., :HEAD_DIM], lead + ((0, 0), (0, LANES - HEAD_DIM))).reshape(depth, kv_lora, N_HEADS * LANES)
    wvt = jnp.pad(wkv[..., HEAD_DIM:].transpose(0, 2, 3, 1), lead + ((0, VT_ROWS - HEAD_DIM), (0, 0)))
    vt_ones = jnp.zeros((N_HEADS, VT_ROWS, 1), F32).at[:, HEAD_DIM].set(1.0)
    gmax2 = lambda gn, gr: HEAD_DIM * jnp.max(gn ** 2, axis=-1) + ROPE_DIM * jnp.max(gr ** 2, axis=-1)
    score_bound = 1.02 * LOG2E * (HEAD_DIM + ROPE_DIM) ** -0.5 * jnp.sqrt(
        gmax2(g_q_nope, g_q_rope) * gmax2(g_k_nope, g_k_rope))
    dil_score_bound = (1.02 * LOG2E * HEAD_DIM ** 0.5
                       * jnp.max(jnp.abs(g_q_dil), axis=-1) * jnp.max(jnp.abs(g_k_dil), axis=-1))
    rows = {
        "norm_mix": g_norm_mix,
        "q_dil": jnp.tile(g_q_dil, (1, 2)),
        "k_dil": jnp.tile(g_k_dil, (1, 2)),
        "cq": g_cq,
        "ckv": g_ckv,
        "q": jnp.tile(jnp.pad(jnp.concatenate([g_q_nope, g_q_rope], axis=-1), ((0, 0), (0, pad))), (1, 2)),
        "q_swapped": jnp.tile(jnp.pad(swap_halves(g_q_rope), ((0, 0), (HEAD_DIM, pad))), (1, 2)),
        "k": jnp.pad(g_k_nope, ((0, 0), (0, LANES - HEAD_DIM))),
        "k_rope": jnp.pad(g_k_rope, ((0, 0), (HEAD_DIM, pad))),
    }
    gains = jnp.stack([jnp.pad(rows[name].astype(F32), ((0, 0), (0, D - rows[name].shape[-1])))
                       for name in GAIN_ROWS], axis=1)
    return {
        "gains": jnp.pad(gains, ((0, 0), (0, -len(GAIN_ROWS) % 8), (0, 0))),
        "w_in": jnp.concatenate([w_in[:, :, :o_kr], w_kr], axis=2).astype(BF16),
        "w_q_up": wq.astype(BF16),
        "w_q_up_swapped": wq_swapped.astype(BF16),
        "q_group_avg": q_group_avg,
        "w_k_up": wk.astype(BF16),
        "w_vt_up": wvt.reshape(depth, N_HEADS * VT_ROWS, kv_lora).astype(BF16),
        "vt_ones": vt_ones.reshape(N_HEADS * VT_ROWS, 1),
        "score_bound": score_bound,
        "dil_score_bound": dil_score_bound,
        "w_out": w_out.astype(BF16),
        "g_norm_mlp": g_norm_mlp.astype(F32)[:, None, :],
        "w_mlp_in": w_mlp_in.astype(BF16),
        "w_mlp_out": w_mlp_out.astype(BF16),
    }


def kernel(x, c, positions, w_mod, b_mod, g_norm_mix, w_in, g_q_dil, g_k_dil, g_cq, w_q_up, g_ckv, w_kv_up, g_q_nope, g_q_rope, g_k_nope, g_k_rope, w_out, g_norm_mlp, w_mlp_in, w_mlp_out):
    B, S, D = x.shape
    depth = w_mod.shape[0]
    qb = 128
    cos_t, sin_t = _rope_tables(positions)
    mod = _modulation(c, w_mod, b_mod).reshape(depth, B, 6, D)
    pos_views = [_strided_positions(positions, dil, qb, window // (2 * dil)) for window, dil in DIL_CONFIGS]
    p = _prepare_params(g_norm_mix, w_in, g_q_dil, g_k_dil, g_cq, w_q_up, g_ckv, w_kv_up, g_q_nope,
                        g_q_rope, g_k_nope, g_k_rope, w_out, g_norm_mlp, w_mlp_in, w_mlp_out)
    for l in range(depth):
        qd, kd, vd, qc, kc, vt = _inproj(x, mod, cos_t, sin_t, p, l)
        o_dil = _dilated_attention(qd, kd, vd, pos_views, p["dil_score_bound"][l], qb)
        o_mla = _latent_attention(qc, kc, vt, p["score_bound"][l])
        x = _outproj_mlp(x, o_dil, o_mla, mod, p, l)
    return x
```

```python
import functools
import math

import jax
import jax.numpy as jnp
from jax import lax
from jax.experimental import pallas as pl
from jax.experimental.pallas import tpu as pltpu

HEAD_DIM = 64
N_HEADS = 8
HALF_WIDTH = N_HEADS * HEAD_DIM
ROPE_DIM = HEAD_DIM // 2
DIL_CONFIGS = ((128, 1), (512, 4), (2048, 16))
RESIDUE_STEP = 4
ROPE_THETA = 10000.0
EPS = 1e-6
MASKED_DISTANCE = 1e33
LANES = 128
BF16_ROWS = 16
VT_ROWS = HEAD_DIM + BF16_ROWS
LOG2E = math.log2(math.e)
VMEM_LIMIT = 56 * 1024 * 1024
MAX_RAW_SCORE = 40.0

BF16 = jnp.bfloat16
F32 = jnp.float32


def _params(n_axes):
    return pltpu.CompilerParams(
        dimension_semantics=("arbitrary",) * n_axes, vmem_limit_bytes=VMEM_LIMIT)


def _lane_iota(shape):
    return lax.broadcasted_iota(jnp.int32, shape, len(shape) - 1)


def _dot_t(a, b):
    return lax.dot_general(a, b, (((1,), (1,)), ((), ())), preferred_element_type=F32)


def _dot(a, b):
    return jnp.dot(a, b, preferred_element_type=F32)


def _rms(x, width):
    return x * lax.rsqrt(jnp.sum(x * x, axis=-1, keepdims=True) * (1.0 / width) + EPS)


def _swap_rope_halves(y):
    up = pltpu.roll(y, LANES - ROPE_DIM // 2, axis=1)
    down = pltpu.roll(y, ROPE_DIM // 2, axis=1)
    return jnp.where(_lane_iota(y.shape) < HEAD_DIM + ROPE_DIM // 2, up, down)


def _rope_table_kernel(pos_ref, inv_ref, sign_ref, cos_ref, sin_ref):
    per_row = LANES // ROPE_DIM
    rows = pos_ref.shape[1]
    ang = pos_ref[0].astype(F32) * inv_ref[...]
    cos_c = jnp.cos(ang)
    sin_c = jnp.sin(ang) * sign_ref[...]
    lane = _lane_iota(ang.shape)
    rope = (lane >= HEAD_DIM) & (lane < HEAD_DIM + ROPE_DIM)
    for a in range(per_row):
        shift = (HEAD_DIM - ROPE_DIM * a) % LANES
        cos_ref[0, pl.ds(a, rows, stride=per_row), :] = jnp.where(rope, pltpu.roll(cos_c, shift, axis=1), 1.0)
        sin_ref[0, pl.ds(a, rows, stride=per_row), :] = jnp.where(rope, pltpu.roll(sin_c, shift, axis=1), 0.0)


def _rope_tables(positions, ts=1024):
    B, S = positions.shape
    half = ROPE_DIM // 2
    per_row = LANES // ROPE_DIM
    inv = ROPE_THETA ** (-jnp.arange(half, dtype=F32) / half)
    inv_row = jnp.tile(jnp.concatenate([inv, inv]), per_row)[None]
    sign_row = jnp.tile(jnp.concatenate([-jnp.ones((half,), F32), jnp.ones((half,), F32)]), per_row)[None]
    packed_pos = jnp.repeat(positions.reshape(B, S // per_row, per_row), ROPE_DIM, axis=-1)
    out = jax.ShapeDtypeStruct((B, S, LANES), F32)
    return pl.pallas_call(
        _rope_table_kernel,
        grid=(B, S // ts),
        in_specs=[pl.BlockSpec((1, ts // per_row, LANES), lambda b, i: (b, i, 0)),
                  pl.BlockSpec((1, LANES), lambda b, i: (0, 0)),
                  pl.BlockSpec((1, LANES), lambda b, i: (0, 0))],
        out_specs=[pl.BlockSpec((1, ts, LANES), lambda b, i: (b, i, 0))] * 2,
        out_shape=[out, out],
        compiler_params=_params(2),
        name="rope_tables",
    )(packed_pos, inv_row, sign_row)


def _mod_kernel(ct_ref, w_ref, b_ref, o_ref):
    ct = ct_ref[...]
    act = ct * (1.0 / (1.0 + jnp.exp(-ct)))
    w = w_ref[0]
    for b in range(ct.shape[1]):
        o_ref[0, b:b + 1, :] = jnp.sum(act[:, b:b + 1] * w, axis=0, keepdims=True) + b_ref[0]


def _modulation(c, w_mod, b_mod, tn=1536):
    depth, D, n_out = w_mod.shape
    B = c.shape[0]
    return pl.pallas_call(
        _mod_kernel,
        grid=(depth, n_out // tn),
        in_specs=[pl.BlockSpec((D, B), lambda l, j: (0, 0)),
                  pl.BlockSpec((1, D, tn), lambda l, j: (l, 0, j)),
                  pl.BlockSpec((1, 1, tn), lambda l, j: (l, 0, j))],
        out_specs=pl.BlockSpec((1, B, tn), lambda l, j: (l, 0, j)),
        out_shape=jax.ShapeDtypeStruct((depth, B, n_out), F32),
        compiler_params=_params(2),
        name="modulation",
    )(c.T, w_mod, b_mod[:, None, :])


def _head_pair_norm(t):
    sq = t * t
    low = _lane_iota(t.shape) < HEAD_DIM
    tot = jnp.sum(sq, axis=-1, keepdims=True)
    lo = jnp.sum(jnp.where(low, sq, 0.0), axis=-1, keepdims=True)
    ss = jnp.where(low, lo, tot - lo)
    return t * lax.rsqrt(ss * (1.0 / HEAD_DIM) + EPS)


def _store_residue_layouts(tile, p, out_refs, stage_ref, gather_ref):
    natural_ref, by4_ref, by16_ref = out_refs
    rows = tile.shape[0]
    W = HALF_WIDTH
    step = RESIDUE_STEP
    natural_ref[0, :, p * LANES:(p + 1) * LANES] = tile.astype(BF16)
    stage_ref[...] = tile
    for a in range(step):
        part = stage_ref[pl.ds(a, rows // step, stride=step), :]
        by4_ref[0, :, a * W + p * LANES:a * W + (p + 1) * LANES] = part.astype(BF16)
        gather_ref[a] = part
        for b in range(step):
            sub = gather_ref[a, pl.ds(b, rows // step ** 2, stride=step), :]
            col = (step * b + a) * W + p * LANES
            by16_ref[0, :, col:col + LANES] = sub.astype(BF16)


GAIN_ROWS = ("norm_mix", "q_dil", "k_dil", "cq", "ckv", "q", "q_swapped", "k", "k_rope")


def _inproj_kernel(x_ref, mod_ref, gains_ref, win_ref, wq_ref, wqs_ref, wk_ref, wvt_ref, qavg_ref, vone_ref,
                   cos_ref, sin_ref,
                   qd1_ref, qd4_ref, qd16_ref, kd1_ref, kd4_ref, kd16_ref, vd1_ref, vd4_ref, vd16_ref,
                   qc_ref, kc_ref, vt_ref, stage_ref, gather_ref, *, q_scale):
    D = x_ref.shape[-1]
    win_ref, wq_ref, wqs_ref, wk_ref, wvt_ref = (r.at[0] for r in (win_ref, wq_ref, wqs_ref, wk_ref, wvt_ref))
    gain = lambda name, width: gains_ref[0, GAIN_ROWS.index(name):GAIN_ROWS.index(name) + 1, :width]
    x = x_ref[0]
    h = _rms(x, D) * gain("norm_mix", D)
    h = (h * (1.0 + mod_ref[0, 0, 1:2, :]) + mod_ref[0, 0, 0:1, :]).astype(BF16)
    cos = cos_ref[0]
    sin = sin_ref[0]
    W = HALF_WIDTH

    q_lora = wq_ref.shape[0]
    kv_lora = wk_ref.shape[0]
    o_cq = 3 * W
    o_ckv = o_cq + q_lora
    o_kr = o_ckv + kv_lora
    cq = _dot(h, win_ref[:, o_cq:o_ckv])
    ckv_kr = _dot(h, win_ref[:, o_ckv:o_kr + LANES])
    cqn = (_rms(cq, q_lora) * gain("cq", q_lora)).astype(BF16)
    cos2 = jnp.concatenate([cos, cos], axis=1)
    sin2 = jnp.concatenate([sin, sin], axis=1)
    for hp in range(N_HEADS // 2):
        cols = slice(2 * hp * LANES, 2 * (hp + 1) * LANES)
        t = _dot(cqn, wq_ref[:, cols])
        t_swapped = _dot(cqn, wqs_ref[:, cols])
        r = lax.rsqrt(_dot((t * t).astype(BF16), qavg_ref[...]) + EPS)
        y = (t * gain("q", 2 * LANES)) * cos2 + (t_swapped * gain("q_swapped", 2 * LANES)) * sin2
        qc_ref[0, :, cols] = (y * (r * q_scale)).astype(BF16)

    ckvn = (_rms(ckv_kr[:, :kv_lora], kv_lora) * gain("ckv", kv_lora)).astype(BF16)
    vt = _dot_t(wvt_ref[...], ckvn) + vone_ref[...]
    pair_rows = 2 * VT_ROWS
    for p in range(N_HEADS // 2):
        vt_ref[0, p] = vt[p * pair_rows:(p + 1) * pair_rows].astype(BF16)
    kr = ckv_kr[:, kv_lora:]
    kr = _rms(kr, ROPE_DIM) * gain("k_rope", LANES)
    kr = kr * cos + _swap_rope_halves(kr) * sin
    for hh in range(N_HEADS):
        if hh % 2 == 0:
            t2 = _dot(ckvn, wk_ref[:, hh * LANES:(hh + 2) * LANES])
        t = t2[:, (hh % 2) * LANES:(hh % 2 + 1) * LANES]
        kc_ref[0, :, hh * LANES:(hh + 1) * LANES] = (_rms(t, HEAD_DIM) * gain("k", LANES) + kr).astype(BF16)

    n_tiles = W // LANES
    operands = (((qd1_ref, qd4_ref, qd16_ref), "q_dil", LOG2E * HEAD_DIM ** -0.5),
                ((kd1_ref, kd4_ref, kd16_ref), "k_dil", 1.0),
                ((vd1_ref, vd4_ref, vd16_ref), None, None))
    for o, (out_refs, gain_name, scale) in enumerate(operands):
        for p in range(n_tiles):
            if p % 2 == 0:
                t2 = _dot(h, win_ref[:, o * W + p * LANES:o * W + (p + 2) * LANES])
            t = t2[:, (p % 2) * LANES:(p % 2 + 1) * LANES]
            if gain_name is not None:
                t = _head_pair_norm(t) * (gain(gain_name, LANES) * scale)
            _store_residue_layouts(t, p, out_refs, stage_ref.at[o * n_tiles + p], gather_ref.at[o * n_tiles + p])


def _const_spec(shape):
    return pl.BlockSpec(shape, lambda b, i: (0,) * len(shape), pipeline_mode=pl.Buffered(1))


def _layer_spec(stacked, l):
    zeros = (0,) * (stacked.ndim - 1)
    return pl.BlockSpec((1,) + stacked.shape[1:], lambda b, i: (l,) + zeros, pipeline_mode=pl.Buffered(1))


def _mod_spec(mod, l):
    return pl.BlockSpec((1, 1) + mod.shape[2:], lambda b, i: (l, b, 0, 0))


def _inproj(x, mod, cos_t, sin_t, p, l, tm=512):
    B, S, D = x.shape
    W = HALF_WIDTH
    layer_params = [p["gains"], p["w_in"], p["w_q_up"], p["w_q_up_swapped"], p["w_k_up"], p["w_vt_up"]]
    consts = [p["q_group_avg"], p["vt_ones"]]
    assert tuple(d for _, d in DIL_CONFIGS) == (1, RESIDUE_STEP, RESIDUE_STEP ** 2)
    tok = lambda width, rows=tm: pl.BlockSpec((1, rows, width), lambda b, i: (b, i, 0))
    dil_shapes = [(S // d, d * W) for _ in range(3) for _, d in DIL_CONFIGS]
    tok_shapes = dil_shapes + [(S, N_HEADS * LANES)] * 2
    vt_shape = (B, N_HEADS // 2, 2 * VT_ROWS, S)
    q_scale = (HEAD_DIM + ROPE_DIM) ** -0.5 * LOG2E
    n_tiles = 3 * W // LANES
    outs = pl.pallas_call(
        functools.partial(_inproj_kernel, q_scale=q_scale),
        grid=(B, S // tm),
        in_specs=[tok(D), _mod_spec(mod, l)] + [_layer_spec(a, l) for a in layer_params]
        + [_const_spec(a.shape) for a in consts] + [tok(LANES), tok(LANES)],
        out_specs=[tok(width, tm * rows // S) for rows, width in tok_shapes]
        + [pl.BlockSpec((1,) + vt_shape[1:3] + (tm,), lambda b, i: (b, 0, 0, i))],
        out_shape=[jax.ShapeDtypeStruct((B,) + s, BF16) for s in tok_shapes]
        + [jax.ShapeDtypeStruct(vt_shape, BF16)],
        scratch_shapes=[pltpu.VMEM((n_tiles, tm, LANES), F32),
                        pltpu.VMEM((n_tiles, RESIDUE_STEP, tm // RESIDUE_STEP, LANES), F32)],
        compiler_params=_params(2),
        name="inproj",
    )(x, mod, *layer_params, *consts, cos_t, sin_t)
    n_dil = len(DIL_CONFIGS)
    qd, kd, vd = (outs[o * n_dil:(o + 1) * n_dil] for o in range(3))
    return qd, kd, vd, outs[-3], outs[-2], outs[-1]


def _alibi_slope(h):
    return 2.0 ** (-8.0 * (h + 1) / N_HEADS)


def _halo_rows(prev_ref, main_ref, next_ref, lo, hi):
    rows = main_ref.shape[1]
    parts = []
    if lo < 0:
        parts.append(prev_ref[0])
    parts.append(main_ref[0, max(lo, 0):min(hi, rows), :])
    if hi > rows:
        parts.append(next_ref[0])
    return parts[0] if len(parts) == 1 else jnp.concatenate(parts, axis=0)


def _to_token_order(blk_ref, dil, nat_ref, tmp_ref):
    W = HALF_WIDTH
    step = RESIDUE_STEP
    rows = nat_ref.shape[1]
    for p in range(W // LANES):
        for a in range(step):
            if dil == step:
                part = blk_ref[0, :, a * W + p * LANES:a * W + (p + 1) * LANES]
            else:
                for b in range(step):
                    col = (step * b + a) * W + p * LANES
                    tmp_ref[p, a, pl.ds(b, rows // step ** 2, stride=step), :] = blk_ref[0, :, col:col + LANES]
                part = tmp_ref[p, a]
            nat_ref[p, pl.ds(a, rows // step, stride=step), :] = part


def _dilated_kernel(q_ref, kp_ref, km_ref, kn_ref, vp_ref, vm_ref, vn_ref, pk_ref, *rest,
                    qb, half, length, merge_dils, online):
    step_rows = q_ref.shape[1]
    nkeys = qb + 2 * half
    n = pl.program_id(2)
    merge = bool(merge_dils)
    if merge:
        n_other = 2 * len(merge_dils)
        out_ref, nat_ref, tmp_ref = rest[n_other:]
        for c, dil in enumerate(merge_dils):
            for t in range(2):
                _to_token_order(rest[2 * c + t], dil, nat_ref.at[2 * c + t], tmp_ref.at[2 * c + t])
    row = lax.broadcasted_iota(jnp.int32, (qb, nkeys), 0)
    col = lax.broadcasted_iota(jnp.int32, (qb, nkeys), 1)
    in_band = jnp.abs(col - half - row) <= half
    low = _lane_iota((qb, LANES)) < HEAD_DIM

    n_res = q_ref.shape[2] // HALF_WIDTH
    for res, j in [(res, j) for res in range(n_res) for j in range(step_rows // qb)]:
        rows = slice(j * qb, (j + 1) * qb)
        q = q_ref[0, rows, :]
        k_all = _halo_rows(kp_ref, km_ref, kn_ref, j * qb - half, (j + 1) * qb + half)
        v_all = _halo_rows(vp_ref, vm_ref, vn_ref, j * qb - half, (j + 1) * qb + half)
        pk = pk_ref[0, res, j]
        pq = jnp.sum(jnp.where(col == row + half, pk, 0), axis=-1, keepdims=True)
        key_u = n * step_rows + j * qb - half + col
        mask = in_band & (key_u >= 0) & (key_u < length)
        dist = jnp.where(mask, jnp.abs(pq - pk).astype(F32), MASKED_DISTANCE)

        for p in range(HALF_WIDTH // LANES):
            sl = slice(res * HALF_WIDTH + p * LANES, res * HALF_WIDTH + (p + 1) * LANES)
            qp, kp, vp = q[:, sl], k_all[:, sl], v_all[:, sl]
            zero = jnp.zeros_like(qp)
            s2 = _dot_t(jnp.concatenate([jnp.where(low, qp, zero), jnp.where(low, zero, qp)], axis=0), kp)
            es, stats, scales = [], [], []
            for hh in range(2):
                s = s2[hh * qb:(hh + 1) * qb] + dist * (-LOG2E * _alibi_slope(2 * p + hh))
                if online:
                    m = jnp.max(s, axis=-1, keepdims=True)
                    e = jnp.exp2(s - m)
                    l = jnp.sum(e, axis=-1, keepdims=True)
                    stats.append(m + jnp.log2(l))
                    scales.append(1.0 / l)
                else:
                    e = jnp.exp2(s)
                    stats.append(jnp.sum(e, axis=-1, keepdims=True))
                es.append(e.astype(BF16))
            pv2 = _dot(jnp.concatenate(es, axis=0), vp)
            outs = [pv2[hh * qb:(hh + 1) * qb] * scales[hh] if online else pv2[hh * qb:(hh + 1) * qb]
                    for hh in range(2)]
            o = jnp.where(low, outs[0], outs[1])
            stat = jnp.where(low, stats[0], stats[1])
            if not merge:
                rest[0][0, rows, sl] = o
                rest[1][0, rows, sl] = stat
                continue
            others = [(nat_ref[2 * c, p, rows, :], nat_ref[2 * c + 1, p, rows, :]) for c in range(len(merge_dils))]
            if online:
                top = stat
                for _, lc in others:
                    top = jnp.maximum(top, lc)
                w = jnp.exp2(stat - top)
                num, den = o * w, w
                for oc, lc in others:
                    w = jnp.exp2(lc - top)
                    num, den = num + oc * w, den + w
            else:
                num, den = o, stat
                for oc, lc in others:
                    num, den = num + oc, den + lc
            out_ref[0, rows, sl] = (num / den).astype(out_ref.dtype)


def _strided_positions(positions, dil, qb, half):
    B, S = positions.shape
    L = S // dil
    pos_s = positions.reshape(B, L, dil).transpose(0, 2, 1)
    padded = jnp.pad(pos_s, ((0, 0), (0, 0), (half, qb)))
    nb = L // qb
    parts = [padded[:, :, off:off + L].reshape(B, dil, nb, qb)[..., :w]
             for off, w in ((0, qb), (qb, 2 * half))]
    return jnp.concatenate(parts, axis=-1)[:, :, :, None, :]


def _dilated_config(qd, kd, vd, pos_k, window, dil, online, prev=(), qb=128, units_per_step=8):
    B, L, width = qd.shape
    W = width // dil
    half = window // (2 * dil)
    n_sub = min(units_per_step, L // qb)
    n_res = min(dil, units_per_step // n_sub)
    step_rows = n_sub * qb
    assert L % step_rows == 0 and step_rows % half == 0 and half % BF16_ROWS == 0 and dil % n_res == 0
    halo_per_step = step_rows // half
    last_halo = L // half - 1
    cols = n_res * W
    main = pl.BlockSpec((1, step_rows, cols), lambda b, r, n: (b, n, r))
    before = pl.BlockSpec((1, half, cols), lambda b, r, n: (b, jnp.maximum(n * halo_per_step - 1, 0), r))
    after = pl.BlockSpec((1, half, cols), lambda b, r, n: (b, jnp.minimum((n + 1) * halo_per_step, last_halo), r))

    in_specs = [main, before, main, after, before, main, after,
                pl.BlockSpec((1, n_res, n_sub, 1, qb + 2 * half), lambda b, r, n: (b, r, n, 0, 0))]
    args = [qd] + [kd] * 3 + [vd] * 3 + [pos_k]
    scratch = []
    if prev:
        assert dil == 1
        for d, o, lse in prev:
            in_specs += [pl.BlockSpec((1, step_rows // d, d * W), lambda b, r, n: (b, n, 0))] * 2
            args += [o, lse]
        n_tiles = W // LANES
        scratch = [pltpu.VMEM((2 * len(prev), n_tiles, step_rows, LANES), F32),
                   pltpu.VMEM((2 * len(prev), n_tiles, RESIDUE_STEP, step_rows // RESIDUE_STEP, LANES), F32)]
        out_specs = main
        out_shape = jax.ShapeDtypeStruct((B, L, W), BF16)
    else:
        out_specs = [main, main]
        out_shape = [jax.ShapeDtypeStruct((B, L, dil * W), F32)] * 2
    return pl.pallas_call(
        functools.partial(_dilated_kernel, qb=qb, half=half, length=L, merge_dils=tuple(d for d, _, _ in prev),
                          online=online),
        grid=(B, dil // n_res, L // step_rows),
        in_specs=in_specs, out_specs=out_specs, out_shape=out_shape,
        scratch_shapes=scratch,
        compiler_params=_params(3),
        name=f"dilated_d{dil}" + ("_online" if online else ""),
    )(*args)


def _dilated_attention(qd, kd, vd, pos_views, score_bound, qb):
    def stage(online):
        def run(qd, kd, vd, pos_views):
            prev = []
            for c in range(len(DIL_CONFIGS) - 1, 0, -1):
                window, dil = DIL_CONFIGS[c]
                prev.append((dil, *_dilated_config(qd[c], kd[c], vd[c], pos_views[c], window, dil, online, qb=qb,
                                                   units_per_step=16)))
            window, dil = DIL_CONFIGS[0]
            return _dilated_config(qd[0], kd[0], vd[0], pos_views[0], window, dil, online, prev=tuple(prev), qb=qb)
        return run

    return lax.cond(score_bound <= MAX_RAW_SCORE, stage(False), stage(True), qd, kd, vd, pos_views)


def _ordering_zero(x):
    bits = lax.bitcast_convert_type(x[0:1, 0:LANES], jnp.uint32)
    zero = lax.shift_right_logical(lax.shift_right_logical(bits, jnp.uint32(16)), jnp.uint32(16))
    return lax.bitcast_convert_type(zero, F32).astype(BF16)


def _mla_kernel(q_ref, k_ref, vt_ref, o_ref, *, tq, stabilise, key_chunks, lookahead):
    chunk = k_ref.shape[1] // key_chunks

    def query_tile(i, carry):
        rows = pl.ds(pl.multiple_of(i * tq, tq), tq)
        qs = [q_ref[0, rows, hh * LANES:(hh + 1) * LANES] for hh in range(2)]
        if stabilise:
            sts = [_dot_t(k_ref[0, :, hh * LANES:(hh + 1) * LANES], qs[hh]) for hh in range(2)]
            accs = []
            for hh in range(2):
                st = sts[hh] - jnp.max(sts[hh], axis=0, keepdims=True)
                accs.append(_dot(vt_ref[0, 0, hh * VT_ROWS:(hh + 1) * VT_ROWS, :], jnp.exp2(st).astype(BF16)))
        else:
            accs, history = [None, None], []
            for c in range(key_chunks):
                ks = slice(c * chunk, (c + 1) * chunk)
                sts = []
                for hh in range(2):
                    k = k_ref[0, ks, hh * LANES:(hh + 1) * LANES]
                    if c >= lookahead:
                        k = k + _ordering_zero(history[c - lookahead][hh])
                    sts.append(_dot_t(k, qs[hh]))
                for hh in range(2):
                    pv = _dot(vt_ref[0, 0, hh * VT_ROWS:(hh + 1) * VT_ROWS, ks], jnp.exp2(sts[hh]).astype(BF16))
                    accs[hh] = pv if accs[hh] is None else accs[hh] + pv
                history.append(list(accs))
        ot = jnp.concatenate([a[:HEAD_DIM] / a[HEAD_DIM:HEAD_DIM + 1] for a in accs], axis=0)
        o_ref[0, rows, :] = ot.T.astype(o_ref.dtype)
        return carry

    lax.fori_loop(0, q_ref.shape[1] // tq, query_tile, 0)


def _latent_attention(qc, kc, vt, score_bound, tq=512, tiles_per_step=8):
    B, S, _ = qc.shape
    pair = 2 * LANES
    rows = tq * tiles_per_step

    def call(stabilise, name):
        return pl.pallas_call(
            functools.partial(_mla_kernel, tq=tq, stabilise=stabilise, key_chunks=4, lookahead=2),
            grid=(B, N_HEADS // 2, S // rows),
            in_specs=[pl.BlockSpec((1, rows, pair), lambda b, p, i: (b, i, p)),
                      pl.BlockSpec((1, S, pair), lambda b, p, i: (b, 0, p)),
                      pl.BlockSpec((1, 1, 2 * VT_ROWS, S), lambda b, p, i: (b, p, 0, 0))],
            out_specs=pl.BlockSpec((1, rows, LANES), lambda b, p, i: (b, i, p)),
            out_shape=jax.ShapeDtypeStruct((B, S, HALF_WIDTH), BF16),
            compiler_params=_params(3),
            name=name,
        )

    return lax.cond(score_bound <= MAX_RAW_SCORE,
                    call(False, "latent_attention"), call(True, "latent_attention_stabilised"),
                    qc, kc, vt)


def _mlp_kernel(x_ref, od_ref, om_ref, mod_ref, wout_ref, g_ref, w1_ref, w2_ref, o_ref, *, tf):
    D = x_ref.shape[-1]
    W = od_ref.shape[-1]
    wout_ref, g_ref, w1_ref, w2_ref = (r.at[0] for r in (wout_ref, g_ref, w1_ref, w2_ref))
    mod = mod_ref.at[0, 0]
    mix = _dot(od_ref[0], wout_ref[0:W, :]) + _dot(om_ref[0], wout_ref[W:2 * W, :])
    x1 = x_ref[0] + mod[2:3, :] * mix
    h = _rms(x1, D) * g_ref[...]
    h = (h * (1.0 + mod[4:5, :]) + mod[3:4, :]).astype(BF16)
    y = jnp.zeros_like(x1)
    for c in range(w1_ref.shape[1] // tf):
        a = jnp.maximum(_dot(h, w1_ref[:, c * tf:(c + 1) * tf]), 0.0)
        y = y + _dot((a * a).astype(BF16), w2_ref[c * tf:(c + 1) * tf, :])
    o_ref[0] = x1 + mod[5:6, :] * y


def _outproj_mlp(x, o_dil, o_mla, mod, p, l, tm=512, tf=512):
    B, S, D = x.shape
    tok = lambda width: pl.BlockSpec((1, tm, width), lambda b, i: (b, i, 0))
    layer_params = [p["w_out"], p["g_norm_mlp"], p["w_mlp_in"], p["w_mlp_out"]]
    return pl.pallas_call(
        functools.partial(_mlp_kernel, tf=tf),
        grid=(B, S // tm),
        in_specs=[tok(D), tok(HALF_WIDTH), tok(HALF_WIDTH), _mod_spec(mod, l)]
        + [_layer_spec(a, l) for a in layer_params],
        out_specs=tok(D),
        out_shape=jax.ShapeDtypeStruct((B, S, D), F32),
        compiler_params=_params(2),
        name="outproj_mlp",
    )(x, o_dil, o_mla, mod, *layer_params)


def _prepare_params(g_norm_mix, w_in, g_q_dil, g_k_dil, g_cq, w_q_up, g_ckv, w_kv_up, g_q_nope,
                    g_q_rope, g_k_nope, g_k_rope, w_out, g_norm_mlp, w_mlp_in, w_mlp_out):
    depth, D = g_norm_mix.shape
    W = HALF_WIDTH
    q_lora, kv_lora = g_cq.shape[-1], g_ckv.shape[-1]
    pad = LANES - HEAD_DIM - ROPE_DIM
    o_kr = 3 * W + q_lora + kv_lora
    lead = ((0, 0),) * 2
    w_kr = jnp.pad(w_in[:, :, o_kr:o_kr + ROPE_DIM], lead + ((HEAD_DIM, pad),))
    wq = w_q_up.reshape(depth, q_lora, N_HEADS, HEAD_DIM + ROPE_DIM)
    swap_halves = lambda a: jnp.concatenate([a[..., ROPE_DIM // 2:], a[..., :ROPE_DIM // 2]], axis=-1)
    wq_swapped = jnp.pad(swap_halves(wq[..., HEAD_DIM:]), lead + ((0, 0), (HEAD_DIM, pad)))
    wq_swapped = wq_swapped.reshape(depth, q_lora, N_HEADS * LANES)
    wq = jnp.pad(wq, lead + ((0, 0), (0, pad))).reshape(depth, q_lora, N_HEADS * LANES)
    lane = jnp.arange(LANES)
    nope_blk = (lane[:, None] < HEAD_DIM) & (lane[None, :] < HEAD_DIM)
    rope_rows = (lane >= HEAD_DIM) & (lane < HEAD_DIM + ROPE_DIM)
    rope_blk = rope_rows[:, None] & (lane[None, :] >= HEAD_DIM)
    tile_avg = nope_blk / HEAD_DIM + rope_blk / ROPE_DIM
    q_group_avg = jnp.kron(jnp.eye(2), tile_avg).astype(BF16)
    wkv = w_kv_up.reshape(depth, kv_lora, N_HEADS, 2 * HEAD_DIM)
    wk = jnp.pad(wkv[..., :HEAD_DIM], lead + ((0, 0), (0, LANES - HEAD_DIM))).reshape(depth, kv_lora, N_HEADS * LANES)
    wvt = jnp.pad(wkv[..., HEAD_DIM:].transpose(0, 2, 3, 1), lead + ((0, VT_ROWS - HEAD_DIM), (0, 0)))
    vt_ones = jnp.zeros((N_HEADS, VT_ROWS, 1), F32).at[:, HEAD_DIM].set(1.0)
    gmax2 = lambda gn, gr: HEAD_DIM * jnp.max(gn ** 2, axis=-1) + ROPE_DIM * jnp.max(gr ** 2, axis=-1)
    score_bound = 1.02 * LOG2E * (HEAD_DIM + ROPE_DIM) ** -0.5 * jnp.sqrt(
        gmax2(g_q_nope, g_q_rope) * gmax2(g_k_nope, g_k_rope))
    dil_score_bound = (1.02 * LOG2E * HEAD_DIM ** 0.5
                       * jnp.max(jnp.abs(g_q_dil), axis=-1) * jnp.max(jnp.abs(g_k_dil), axis=-1))
    rows = {
        "norm_mix": g_norm_mix,
        "q_dil": jnp.tile(g_q_dil, (1, 2)),
        "k_dil": jnp.tile(g_k_dil, (1, 2)),
        "cq": g_cq,
        "ckv": g_ckv,
        "q": jnp.tile(jnp.pad(jnp.concatenate([g_q_nope, g_q_rope], axis=-1), ((0, 0), (0, pad))), (1, 2)),
        "q_swapped": jnp.tile(jnp.pad(swap_halves(g_q_rope), ((0, 0), (HEAD_DIM, pad))), (1, 2)),
        "k": jnp.pad(g_k_nope, ((0, 0), (0, LANES - HEAD_DIM))),
        "k_rope": jnp.pad(g_k_rope, ((0, 0), (HEAD_DIM, pad))),
    }
    gains = jnp.stack([jnp.pad(rows[name].astype(F32), ((0, 0), (0, D - rows[name].shape[-1])))
                       for name in GAIN_ROWS], axis=1)
    return {
        "gains": jnp.pad(gains, ((0, 0), (0, -len(GAIN_ROWS) % 8), (0, 0))),
        "w_in": jnp.concatenate([w_in[:, :, :o_kr], w_kr], axis=2).astype(BF16),
        "w_q_up": wq.astype(BF16),
        "w_q_up_swapped": wq_swapped.astype(BF16),
        "q_group_avg": q_group_avg,
        "w_k_up": wk.astype(BF16),
        "w_vt_up": wvt.reshape(depth, N_HEADS * VT_ROWS, kv_lora).astype(BF16),
        "vt_ones": vt_ones.reshape(N_HEADS * VT_ROWS, 1),
        "score_bound": score_bound,
        "dil_score_bound": dil_score_bound,
        "w_out": w_out.astype(BF16),
        "g_norm_mlp": g_norm_mlp.astype(F32)[:, None, :],
        "w_mlp_in": w_mlp_in.astype(BF16),
        "w_mlp_out": w_mlp_out.astype(BF16),
    }


def kernel(x, c, positions, w_mod, b_mod, g_norm_mix, w_in, g_q_dil, g_k_dil, g_cq, w_q_up, g_ckv, w_kv_up, g_q_nope, g_q_rope, g_k_nope, g_k_rope, w_out, g_norm_mlp, w_mlp_in, w_mlp_out):
    B, S, D = x.shape
    depth = w_mod.shape[0]
    qb = 128
    cos_t, sin_t = _rope_tables(positions)
    mod = _modulation(c, w_mod, b_mod).reshape(depth, B, 6, D)
    pos_views = [_strided_positions(positions, dil, qb, window // (2 * dil)) for window, dil in DIL_CONFIGS]
    p = _prepare_params(g_norm_mix, w_in, g_q_dil, g_k_dil, g_cq, w_q_up, g_ckv, w_kv_up, g_q_nope,
                        g_q_rope, g_k_nope, g_k_rope, w_out, g_norm_mlp, w_mlp_in, w_mlp_out)
    for l in range(depth):
        qd, kd, vd, qc, kc, vt = _inproj(x, mod, cos_t, sin_t, p, l)
        o_dil = _dilated_attention(qd, kd, vd, pos_views, p["dil_score_bound"][l], qb)
        o_mla = _latent_attention(qc, kc, vt, p["score_bound"][l])
        x = _outproj_mlp(x, o_dil, o_mla, mod, p, l)
    return x
```
